```python
import math
import jax
import jax.numpy as jnp
from jax import lax
import numpy as np

D_MODEL = 2048
BATCH = 4
SEQ = 8192
DEPTH = 1

MIX_WIDTH = D_MODEL
ATTN_HEAD_DIM = 64
ATTN_WIDTH = MIX_WIDTH // 2
ATTN_Q_HEADS = ATTN_WIDTH // ATTN_HEAD_DIM
ATTN_KV_HEADS = ATTN_Q_HEADS // 8
WINDOW = 128
SSM_WIDTH = MIX_WIDTH - ATTN_WIDTH
SSM_HEAD_DIM = 64
SSM_HEADS = SSM_WIDTH // SSM_HEAD_DIM
SSM_GROUPS = 2
SSM_STATE = 128
SSM_CONV = 4
SSM_CHUNK = 128
N_EXPERTS = 32
TOP_K = 4
D_FF = D_MODEL
SWIGLU_LIMIT = 7.0
SWIGLU_ALPHA = 1.702
MOE_BLOCK = 512
EPS = 1e-6

KV_WIDTH = ATTN_KV_HEADS * ATTN_HEAD_DIM
BC_WIDTH = SSM_GROUPS * SSM_STATE
CONV_WIDTH = SSM_WIDTH + 2 * BC_WIDTH
IN_PROJ_DIM = ATTN_WIDTH + 2 * KV_WIDTH + SSM_WIDTH + CONV_WIDTH + SSM_HEADS
IN_SPLITS = [
    ATTN_WIDTH,
    ATTN_WIDTH + KV_WIDTH,
    ATTN_WIDTH + 2 * KV_WIDTH,
    ATTN_WIDTH + 2 * KV_WIDTH + SSM_WIDTH,
    ATTN_WIDTH + 2 * KV_WIDTH + SSM_WIDTH + CONV_WIDTH,
]

kernel_name = "hymba_swa_sink_ssd_moe_layer"


def rms_norm(x, g):
    xf = x.astype(jnp.float32)
    y = xf * lax.rsqrt(jnp.mean(xf * xf, axis=-1, keepdims=True) + EPS)
    return (y * g.astype(jnp.float32)).astype(x.dtype)


def sliding_window_sink_attention(q, k, v, sinks):
    b, s, hq, dh = q.shape
    hkv = k.shape[2]
    g = hq // hkv
    nb = s // WINDOW
    qb = q.reshape(b, nb, WINDOW, hkv, g, dh)

    def band(t):
        tp = jnp.pad(t, ((0, 0), (WINDOW, 0), (0, 0), (0, 0)))
        prev = tp[:, :s].reshape(b, nb, WINDOW, hkv, dh)
        cur = t.reshape(b, nb, WINDOW, hkv, dh)
        return jnp.concatenate([prev, cur], axis=2)

    kb, vb = band(k), band(v)
    scores = jnp.einsum('bnqhgd,bnkhd->bnhgqk', qb, kb).astype(jnp.float32) * (dh ** -0.5)
    qi = jnp.arange(WINDOW)[:, None]
    kj = jnp.arange(2 * WINDOW)[None, :]
    dist = qi + WINDOW - kj
    key_pos = jnp.arange(nb)[:, None, None] * WINDOW - WINDOW + kj[None]
    allowed = (dist >= 0) & (dist < WINDOW) & (key_pos >= 0)
    scores = jnp.where(allowed[None, :, None, None], scores, -jnp.inf)
    sink = sinks.astype(jnp.float32).reshape(1, 1, hkv, g, 1, 1)
    m = jnp.maximum(jnp.max(scores, axis=-1, keepdims=True), sink)
    p = jnp.exp(scores - m)
    probs = p / (jnp.sum(p, axis=-1, keepdims=True) + jnp.exp(sink - m))
    out = jnp.einsum('bnhgqk,bnkhd->bnqhgd', probs.astype(v.dtype), vb)
    return out.reshape(b, s, hq * dh)


def segsum(a):
    cs = jnp.cumsum(a, axis=-1)
    diff = cs[..., :, None] - cs[..., None, :]
    l = a.shape[-1]
    mask = jnp.tril(jnp.ones((l, l), dtype=bool))
    return jnp.where(mask, diff, -jnp.inf)


def ssd_chunked(xh, dt, a, bmat, cmat):
    b, s, h, p = xh.shape
    g, n = bmat.shape[2], bmat.shape[3]
    r = h // g
    l = SSM_CHUNK
    c = s // l
    f32 = jnp.float32
    X = (xh.astype(f32) * dt[..., None]).reshape(b, c, l, g, r, p)
    dA = jnp.moveaxis((dt * a).reshape(b, c, l, g, r), 2, -1)
    Bc = bmat.astype(f32).reshape(b, c, l, g, n)
    Cc = cmat.astype(f32).reshape(b, c, l, g, n)
    cs = jnp.cumsum(dA, axis=-1)
    L = jnp.exp(segsum(dA))
    CB = jnp.einsum('bclgn,bcsgn->bcgls', Cc, Bc)
    y_diag = jnp.einsum('bcgls,bcgrls,bcsgrp->bclgrp', CB, L, X)
    decay_to_end = jnp.exp(cs[..., -1:] - cs)
    chunk_states = jnp.einsum('bclgn,bcgrl,bclgrp->bcgrpn', Bc, decay_to_end, X)
    chunk_decay = jnp.exp(cs[..., -1])

    def step(state, inp):
        st, dec = inp
        return state * dec[..., None, None] + st, state

    init = jnp.zeros((b, g, r, p, n), f32)
    _, states_in = lax.scan(step, init, (jnp.moveaxis(chunk_states, 1, 0),
                                         jnp.moveaxis(chunk_decay, 1, 0)))
    states_in = jnp.moveaxis(states_in, 0, 1)
    y_off = jnp.einsum('bclgn,bcgrpn,bcgrl->bclgrp', Cc, states_in, jnp.exp(cs))
    return (y_diag + y_off).reshape(b, s, h, p).astype(xh.dtype)


def causal_depthwise_conv(x, w, bias):
    out = lax.conv_general_dilated(
        x, w[:, None, :], window_strides=(1,), padding=[(SSM_CONV - 1, 0)],
        dimension_numbers=('NWC', 'WIO', 'NWC'), feature_group_count=x.shape[-1])
    return out + bias


def ssd_mixer(z, xbc, dt_raw, conv_w, conv_b, dt_bias, a_log, d_skip, norm_w):
    b, s, _ = xbc.shape
    xbc = jax.nn.silu(causal_depthwise_conv(xbc, conv_w, conv_b))
    xs, bm, cm = jnp.split(xbc, [SSM_WIDTH, SSM_WIDTH + BC_WIDTH], axis=-1)
    xh = xs.reshape(b, s, SSM_HEADS, SSM_HEAD_DIM)
    bm = bm.reshape(b, s, SSM_GROUPS, SSM_STATE)
    cm = cm.reshape(b, s, SSM_GROUPS, SSM_STATE)
    dt = jax.nn.softplus(dt_raw.astype(jnp.float32) + dt_bias.astype(jnp.float32))
    a = -jnp.exp(a_log.astype(jnp.float32))
    y = ssd_chunked(xh, dt, a, bm, cm) + xh * d_skip[:, None]
    gated = (y.reshape(b, s, SSM_WIDTH) * jax.nn.silu(z)).reshape(b, s, SSM_GROUPS, SSM_WIDTH // SSM_GROUPS)
    return rms_norm(gated, norm_w.reshape(SSM_GROUPS, SSM_WIDTH // SSM_GROUPS)).reshape(b, s, SSM_WIDTH)


def moe_ffn(h, w_router, b_router, w_gate, b_gate, w_up, b_up, w_down, b_down):
    bsz, s, d = h.shape
    t = bsz * s
    tokens = h.reshape(t, d)
    logits = (tokens @ w_router + b_router).astype(jnp.float32)
    top_vals, top_idx = lax.top_k(logits, TOP_K)
    gates = jax.nn.softmax(top_vals, axis=-1)
    n_assign = t * TOP_K
    e_flat = top_idx.reshape(-1)
    tok_flat = jnp.arange(n_assign, dtype=jnp.int32) // TOP_K
    gate_flat = gates.reshape(-1)
    order = jnp.argsort(e_flat)
    se, stok, sg = e_flat[order], tok_flat[order], gate_flat[order]
    counts = jnp.bincount(e_flat, length=N_EXPERTS)
    padded = (counts + MOE_BLOCK - 1) // MOE_BLOCK * MOE_BLOCK
    starts = jnp.cumsum(counts) - counts
    pends = jnp.cumsum(padded)
    pstarts = pends - padded
    dest = pstarts[se] + jnp.arange(n_assign) - starts[se]
    n_blocks = -(-n_assign // MOE_BLOCK) + N_EXPERTS
    n_slots = n_blocks * MOE_BLOCK
    slot_tok = jnp.full((n_slots,), t, dtype=jnp.int32).at[dest].set(stok.astype(jnp.int32))
    slot_gate = jnp.zeros((n_slots,), jnp.float32).at[dest].set(sg)
    block_expert = jnp.minimum(
        jnp.searchsorted(pends, jnp.arange(n_blocks) * MOE_BLOCK, side='right'), N_EXPERTS - 1)
    tokens_pad = jnp.concatenate([tokens, jnp.zeros((1, d), tokens.dtype)], axis=0)

    def block_step(acc, inp):
        e, tok_ids, gw = inp
        xb = tokens_pad[tok_ids]
        gl = jnp.minimum(xb @ w_gate[e] + b_gate[e], SWIGLU_LIMIT)
        li = jnp.clip(xb @ w_up[e] + b_up[e], -SWIGLU_LIMIT, SWIGLU_LIMIT)
        hb = gl * jax.nn.sigmoid(SWIGLU_ALPHA * gl) * (li + 1.0)
        yb = hb @ w_down[e] + b_down[e]
        return acc.at[tok_ids].add(yb * gw[:, None].astype(yb.dtype)), None

    acc0 = jnp.zeros((t + 1, d), tokens.dtype)
    acc, _ = lax.scan(block_step, acc0, (block_expert,
                                         slot_tok.reshape(n_blocks, MOE_BLOCK),
                                         slot_gate.reshape(n_blocks, MOE_BLOCK)))
    return acc[:t].reshape(bsz, s, d)


def hybrid_layer(x, attn_norm_w, w_in, q_norm_w, k_norm_w, attn_sinks, conv_w, conv_b,
                 dt_bias, a_log, d_skip, ssm_norm_w, attn_out_norm_w, w_out, ffn_norm_w,
                 w_router, b_router, w_gate, b_gate, w_up, b_up, w_down, b_down):
    b, s, _ = x.shape
    hn = rms_norm(x, attn_norm_w)
    proj = hn @ w_in
    q, k, v, z, xbc, dt_raw = jnp.split(proj, IN_SPLITS, axis=-1)
    q = rms_norm(q.reshape(b, s, ATTN_Q_HEADS, ATTN_HEAD_DIM), q_norm_w)
    k = rms_norm(k.reshape(b, s, ATTN_KV_HEADS, ATTN_HEAD_DIM), k_norm_w)
    v = v.reshape(b, s, ATTN_KV_HEADS, ATTN_HEAD_DIM)
    attn = rms_norm(sliding_window_sink_attention(q, k, v, attn_sinks), attn_out_norm_w)
    ssm = ssd_mixer(z, xbc, dt_raw, conv_w, conv_b, dt_bias, a_log, d_skip, ssm_norm_w)
    x = x + jnp.concatenate([attn, ssm], axis=-1) @ w_out
    hf = rms_norm(x, ffn_norm_w)
    return x + moe_ffn(hf, w_router, b_router, w_gate, b_gate, w_up, b_up, w_down, b_down)


def setup_inputs(seed: int = 0) -> dict:
    key = jax.random.key(seed)
    ks = jax.random.split(key, 24)

    def nrm(k, shape, scale):
        return jax.random.normal(k, shape, jnp.float32) * scale

    def gain(k, shape):
        return 1.0 + nrm(k, shape, 0.05)

    dt0 = jnp.exp(jax.random.uniform(ks[8], (DEPTH, SSM_HEADS), jnp.float32,
                                     minval=math.log(1e-3), maxval=math.log(1e-1)))
    return {
        "x": nrm(ks[0], (BATCH, SEQ, D_MODEL), 1.0),
        "attn_norm_w": gain(ks[1], (DEPTH, D_MODEL)),
        "w_in": nrm(ks[2], (DEPTH, D_MODEL, IN_PROJ_DIM), D_MODEL ** -0.5),
        "q_norm_w": gain(ks[3], (DEPTH, ATTN_HEAD_DIM)),
        "k_norm_w": gain(ks[4], (DEPTH, ATTN_HEAD_DIM)),
        "attn_sinks": nrm(ks[5], (DEPTH, ATTN_Q_HEADS), 0.5),
        "conv_w": nrm(ks[6], (DEPTH, SSM_CONV, CONV_WIDTH), SSM_CONV ** -0.5),
        "conv_b": nrm(ks[7], (DEPTH, CONV_WIDTH), 0.02),
        "dt_bias": dt0 + jnp.log(-jnp.expm1(-dt0)),
        "a_log": jnp.log(jax.random.uniform(ks[9], (DEPTH, SSM_HEADS), jnp.float32, minval=1.0, maxval=16.0)),
        "d_skip": gain(ks[10], (DEPTH, SSM_HEADS)),
        "ssm_norm_w": gain(ks[11], (DEPTH, SSM_WIDTH)),
        "attn_out_norm_w": gain(ks[12], (DEPTH, ATTN_WIDTH)),
        "w_out": nrm(ks[13], (DEPTH, MIX_WIDTH, D_MODEL), MIX_WIDTH ** -0.5),
        "ffn_norm_w": gain(ks[14], (DEPTH, D_MODEL)),
        "w_router": nrm(ks[15], (DEPTH, D_MODEL, N_EXPERTS), D_MODEL ** -0.5),
        "b_router": nrm(ks[16], (DEPTH, N_EXPERTS), 0.01),
        "w_gate": nrm(ks[17], (DEPTH, N_EXPERTS, D_MODEL, D_FF), D_MODEL ** -0.5),
        "b_gate": nrm(ks[18], (DEPTH, N_EXPERTS, D_FF), 0.02),
        "w_up": nrm(ks[19], (DEPTH, N_EXPERTS, D_MODEL, D_FF), D_MODEL ** -0.5),
        "b_up": nrm(ks[20], (DEPTH, N_EXPERTS, D_FF), 0.02),
        "w_down": nrm(ks[21], (DEPTH, N_EXPERTS, D_FF, D_MODEL), D_FF ** -0.5),
        "b_down": nrm(ks[22], (DEPTH, N_EXPERTS, D_MODEL), 0.02),
    }


def reference(x, attn_norm_w, w_in, q_norm_w, k_norm_w, attn_sinks, conv_w, conv_b,
              dt_bias, a_log, d_skip, ssm_norm_w, attn_out_norm_w, w_out, ffn_norm_w,
              w_router, b_router, w_gate, b_gate, w_up, b_up, w_down, b_down):
    for i in range(DEPTH):
        x = hybrid_layer(x, attn_norm_w[i], w_in[i], q_norm_w[i], k_norm_w[i], attn_sinks[i],
                         conv_w[i], conv_b[i], dt_bias[i], a_log[i], d_skip[i], ssm_norm_w[i],
                         attn_out_norm_w[i], w_out[i], ffn_norm_w[i], w_router[i], b_router[i],
                         w_gate[i], b_gate[i], w_up[i], b_up[i], w_down[i], b_down[i])
    return x
```

```python
import functools

import jax
import jax.numpy as jnp
from jax import lax
from jax.experimental import pallas as pl
from jax.experimental.pallas import tpu as pltpu

F32 = jnp.float32
BF16 = jnp.bfloat16

D_MODEL = 2048
HEAD_DIM = 64
ATTN_WIDTH = 1024
KV_WIDTH = 128
WINDOW = 128
SSM_WIDTH = 1024
SSM_HEADS = 16
SSM_GROUPS = 2
SSM_STATE = 128
SSM_CONV = 4
SSM_CHUNK = 128
BC_WIDTH = SSM_GROUPS * SSM_STATE
CONV_WIDTH = SSM_WIDTH + 2 * BC_WIDTH
N_EXPERTS = 32
TOP_K = 4
D_FF = 2048
SWIGLU_LIMIT = 7.0
SWIGLU_ALPHA = 1.702
EPS = 1e-6

LANES = 128
SUBLANES = 8
VMEM_LIMIT = 56 * 1024 * 1024

PROJ_WIDTH = 4096
COL_Q = 0
COL_Z = ATTN_WIDTH
COL_XBC = COL_Z + SSM_WIDTH
COL_K = COL_XBC + CONV_WIDTH
COL_V = COL_K + KV_WIDTH
COL_DT = COL_V + KV_WIDTH
ROW_TILES = D_MODEL // LANES

NEG_BIG = -1e30


def _params(*sem):
    return pltpu.CompilerParams(dimension_semantics=sem, vmem_limit_bytes=VMEM_LIMIT)


def _split2(v):
    hi = v.astype(BF16)
    lo = (v - hi.astype(F32)).astype(BF16)
    return hi, lo


def _split3(v):
    hi = v.astype(BF16)
    r = v - hi.astype(F32)
    mid = r.astype(BF16)
    lo = (r - mid.astype(F32)).astype(BF16)
    return hi, mid, lo


def _dot(a, b):
    return jnp.dot(a, b, preferred_element_type=F32)


def _dot_exact_lhs(parts, m):
    acc = _dot(parts[0], m)
    for p in parts[1:]:
        acc = acc + _dot(p, m)
    return acc


def _sigmoid(v):
    return 1.0 / (1.0 + jnp.exp(-v))


def _iota(shape, dim):
    return lax.broadcasted_iota(jnp.int32, shape, dim)


def _in_proj_body(x_ref, g_ref, w_ref, o_ref, hn_ref):
    @pl.when(pl.program_id(1) == 0)
    def _():
        x = x_ref[...]
        ms = jnp.mean(x * x, axis=-1, keepdims=True)
        hn_ref[...] = (x * lax.rsqrt(ms + EPS) * g_ref[...]).astype(BF16)

    o_ref[...] = _dot(hn_ref[...], w_ref[...])


def in_proj(x2d, g, w_p):
    t, d = x2d.shape
    n = w_p.shape[1]
    tm = min(1024, t)
    tn = 1024
    return pl.pallas_call(
        _in_proj_body,
        grid=(t // tm, n // tn),
        in_specs=[
            pl.BlockSpec((tm, d), lambda i, j: (i, 0)),
            pl.BlockSpec((1, d), lambda i, j: (0, 0)),
            pl.BlockSpec((d, tn), lambda i, j: (0, j)),
        ],
        out_specs=pl.BlockSpec((tm, tn), lambda i, j: (i, j)),
        out_shape=jax.ShapeDtypeStruct((t, n), F32),
        scratch_shapes=[pltpu.VMEM((tm, d), BF16)],
        compiler_params=_params("arbitrary", "arbitrary"),
        name="in_proj",
    )(x2d, g, w_p)


def _head_pair_norm(v, w, blockdiag):
    ss = _dot_exact_lhs(_split2(v * v), blockdiag)
    return v * lax.rsqrt(ss * (1.0 / HEAD_DIM) + EPS) * w


def _attn_body(sinks_ref, q_ref, kvc_ref, kvp_ref, qw_ref, kw_ref, onw_ref, o_ref, acc_ref):
    i = pl.program_id(1)
    tq = q_ref.shape[0]
    nsub = tq // WINDOW
    nk = tq + WINDOW
    lo_lanes = _iota((1, LANES), 1) < HEAD_DIM
    blockdiag = jnp.where(
        _iota((LANES, LANES), 0) // HEAD_DIM == _iota((LANES, LANES), 1) // HEAD_DIM, 1.0, 0.0
    ).astype(BF16)

    kv_all = jnp.concatenate([kvp_ref[...], kvc_ref[...]], axis=0)
    k_all = _head_pair_norm(kv_all[:, :KV_WIDTH], kw_ref[...], blockdiag)
    v_all = kv_all[:, KV_WIDTH:]
    kt = k_all.T
    zero_half = jnp.zeros((HEAD_DIM, nk), F32)
    kt_var = [
        [jnp.concatenate([kt[j * HEAD_DIM:(j + 1) * HEAD_DIM], zero_half], axis=0).astype(BF16),
         jnp.concatenate([zero_half, kt[j * HEAD_DIM:(j + 1) * HEAD_DIM]], axis=0).astype(BF16)]
        for j in range(2)
    ]
    v_sw = pltpu.roll(v_all, HEAD_DIM, axis=1)
    v_var = [
        [jnp.where(lo_lanes, v_all, 0.0).astype(BF16), jnp.where(lo_lanes, 0.0, v_sw).astype(BF16)],
        [jnp.where(lo_lanes, v_sw, 0.0).astype(BF16), jnp.where(lo_lanes, 0.0, v_all).astype(BF16)],
    ]

    qn = [
        _head_pair_norm(q_ref[:, c * LANES:(c + 1) * LANES], qw_ref[...], blockdiag).astype(BF16)
        for c in range(ATTN_WIDTH // LANES)
    ]

    rows = 4 * WINDOW
    qi = _iota((rows, 2 * WINDOW), 0) % WINDOW
    kj = _iota((rows, 2 * WINDOW), 1)
    band = (kj > qi) & (kj <= qi + WINDOW)
    first_min = jnp.where(i == 0, WINDOW, 0)
    band_first = band & (kj >= first_min)

    for s in range(nsub):
        mask = band_first if s == 0 else band
        for j in range(2):
            qs = jnp.concatenate([qn[4 * j + c][s * WINDOW:(s + 1) * WINDOW] for c in range(4)], axis=0)
            out = None
            for par in range(2):
                sink = jnp.concatenate(
                    [jnp.full((WINDOW, 1), sinks_ref[8 * j + 2 * c + par], F32) for c in range(4)], axis=0)
                sc = _dot(qs, kt_var[j][par][:, s * WINDOW:s * WINDOW + 2 * WINDOW])
                sc = jnp.where(mask, sc, NEG_BIG)
                m = jnp.maximum(jnp.max(sc, axis=-1, keepdims=True), sink)
                p = jnp.exp(sc - m)
                den = jnp.sum(p, axis=-1, keepdims=True) + jnp.exp(sink - m)
                pv = _dot(p.astype(BF16), v_var[j][par][s * WINDOW:s * WINDOW + 2 * WINDOW])
                pv = pv * (1.0 / den)
                out = pv if out is None else out + pv
            for c in range(4):
                acc_ref[s * WINDOW:(s + 1) * WINDOW, (4 * j + c) * LANES:(4 * j + c + 1) * LANES] = (
                    out[c * WINDOW:(c + 1) * WINDOW])

    a = acc_ref[...]
    ms = jnp.mean(a * a, axis=-1, keepdims=True)
    o_ref[...] = (a * lax.rsqrt(ms + EPS) * onw_ref[...]).astype(BF16)


def attention(proj, sinks, qw2, kw2, onw, batch, seq):
    t = proj.shape[0]
    tq = min(512, seq)
    nq = seq // tq
    sub = tq // WINDOW
    kv_col = COL_K // (2 * KV_WIDTH)
    grid_spec = pltpu.PrefetchScalarGridSpec(
        num_scalar_prefetch=1,
        grid=(batch, nq),
        in_specs=[
            pl.BlockSpec((tq, ATTN_WIDTH), lambda b, i, s: (b * nq + i, COL_Q // ATTN_WIDTH)),
            pl.BlockSpec((tq, 2 * KV_WIDTH), lambda b, i, s: (b * nq + i, kv_col)),
            pl.BlockSpec((WINDOW, 2 * KV_WIDTH),
                         lambda b, i, s: (b * nq * sub + jnp.maximum(i * sub - 1, 0), kv_col)),
            pl.BlockSpec((1, LANES), lambda b, i, s: (0, 0)),
            pl.BlockSpec((1, LANES), lambda b, i, s: (0, 0)),
            pl.BlockSpec((1, ATTN_WIDTH), lambda b, i, s: (0, 0)),
        ],
        out_specs=pl.BlockSpec((tq, ATTN_WIDTH), lambda b, i, s: (b * nq + i, 0)),
        scratch_shapes=[pltpu.VMEM((tq, ATTN_WIDTH), F32)],
    )
    return pl.pallas_call(
        _attn_body,
        grid_spec=grid_spec,
        out_shape=jax.ShapeDtypeStruct((t, ATTN_WIDTH), BF16),
        compiler_params=_params("arbitrary", "arbitrary"),
        name="attention",
    )(sinks, proj, proj, proj, qw2, kw2, onw)


def _ssd_body(z_ref, xs_ref, bc_ref, dt_ref, cwx_ref, cwb_ref, cbx_ref, cbb_ref, dtb_ref, alog_ref,
              dsk_ref, nw_ref, o_ref, xpx_ref, xpb_ref, st_ref, y_ref):
    i = pl.program_id(1)
    ts = xs_ref.shape[0]
    nch = ts // SSM_CHUNK
    gw = SSM_WIDTH // SSM_GROUPS
    pad = SUBLANES

    @pl.when(i == 0)
    def _():
        xpx_ref[0:pad, :] = jnp.zeros((pad, SSM_WIDTH), F32)
        xpb_ref[0:pad, :] = jnp.zeros((pad, 2 * BC_WIDTH), F32)
        st_ref[...] = jnp.zeros(st_ref.shape, F32)

    xpx_ref[pad:pad + ts, :] = xs_ref[...]
    xpb_ref[pad:pad + ts, :] = bc_ref[...]

    def conv_silu(xp_ref, w_ref, b_ref):
        acc = b_ref[...] + w_ref[SSM_CONV - 1:SSM_CONV, :] * xp_ref[pad:pad + ts, :]
        for k in range(SSM_CONV - 1):
            off = pad - (SSM_CONV - 1) + k
            acc = acc + w_ref[k:k + 1, :] * xp_ref[off:off + ts, :]
        return acc * _sigmoid(acc)

    xc = conv_silu(xpx_ref, cwx_ref, cbx_ref)
    bcc = conv_silu(xpb_ref, cwb_ref, cbb_ref)
    xpx_ref[0:pad, :] = xs_ref[ts - pad:ts, :]
    xpb_ref[0:pad, :] = bc_ref[ts - pad:ts, :]

    dtr = dt_ref[...] + dtb_ref[...]
    dt = jnp.maximum(dtr, 0.0) + jnp.log(1.0 + jnp.exp(-jnp.abs(dtr)))
    a = -jnp.exp(alog_ref[...])
    da = dt * a

    ri = _iota((ts, ts), 0)
    ci = _iota((ts, ts), 1)
    same = (ri // SSM_CHUNK) == (ci // SSM_CHUNK)
    tri_blk = jnp.where(same & (ci <= ri), 1.0, 0.0).astype(BF16)
    all_blk = jnp.where(same, 1.0, 0.0).astype(BF16)
    da3 = _split3(da)
    cs = _dot(tri_blk, da3[0]) + _dot(tri_blk, da3[1]) + _dot(tri_blk, da3[2])
    cl = _dot(all_blk, da3[0]) + _dot(all_blk, da3[1]) + _dot(all_blk, da3[2])
    ecs = jnp.exp(cs)
    dte = jnp.exp(cl - cs)

    expand = jnp.where(_iota((LANES, SSM_WIDTH), 1) // HEAD_DIM == _iota((LANES, SSM_WIDTH), 0),
                       1.0, 0.0).astype(BF16)
    dt_e = _dot_exact_lhs(_split2(dt), expand)
    dtd_e = _dot_exact_lhs(_split2(dt * dte), expand)
    ecs_e = _dot_exact_lhs(_split2(ecs), expand)
    xf = xc * dt_e
    xd = (xc * dtd_e).astype(BF16)

    lo_lanes = _iota((1, LANES), 1) < HEAD_DIM
    li = _iota((SSM_CHUNK, SSM_CHUNK), 0)
    lj = _iota((SSM_CHUNK, SSM_CHUNK), 1)
    tril = lj <= li
    upper = jnp.where(li <= lj, 1.0, 0.0).astype(BF16)

    for c in range(nch):
        r0 = c * SSM_CHUNK
        da_t = da[r0:r0 + SSM_CHUNK].T
        cs_t = _dot_exact_lhs(_split3(da_t), upper)
        cs_c = cs[r0:r0 + SSM_CHUNK]
        for g in range(SSM_GROUPS):
            b_c = bcc[r0:r0 + SSM_CHUNK, g * SSM_STATE:(g + 1) * SSM_STATE]
            c_c = bcc[r0:r0 + SSM_CHUNK, BC_WIDTH + g * SSM_STATE:BC_WIDTH + (g + 1) * SSM_STATE].astype(BF16)
            b_t = b_c.T.astype(BF16)
            cb = _dot(c_c, b_t)
            st = st_ref[g]
            y_off = _dot(c_c, st.astype(BF16)) * ecs_e[r0:r0 + SSM_CHUNK, g * gw:(g + 1) * gw]
            s_new = _dot(b_t, xd[r0:r0 + SSM_CHUNK, g * gw:(g + 1) * gw])
            parts = []
            for p in range(4):
                h0 = 8 * g + 2 * p
                ms = []
                for h in (h0, h0 + 1):
                    diff = cs_c[:, h:h + 1] - cs_t[h:h + 1, :]
                    ms.append((cb * jnp.exp(jnp.where(tril, diff, NEG_BIG))).astype(BF16))
                m2 = jnp.concatenate(ms, axis=1)
                x2 = xf[r0:r0 + SSM_CHUNK, h0 * HEAD_DIM:(h0 + 2) * HEAD_DIM]
                xbd = jnp.concatenate([jnp.where(lo_lanes, x2, 0.0), jnp.where(lo_lanes, 0.0, x2)],
                                      axis=0).astype(BF16)
                parts.append(_dot(m2, xbd))
            y_ref[r0:r0 + SSM_CHUNK, g * gw:(g + 1) * gw] = jnp.concatenate(parts, axis=1) + y_off
            dec = ecs_e[r0 + SSM_CHUNK - 1:r0 + SSM_CHUNK, g * gw:(g + 1) * gw]
            st_ref[g] = st * dec + s_new

    y = y_ref[...] + xc * dsk_ref[...]
    z = z_ref[...]
    gated = y * (z * _sigmoid(z))
    outs = []
    for g in range(SSM_GROUPS):
        gg = gated[:, g * gw:(g + 1) * gw]
        ms = jnp.mean(gg * gg, axis=-1, keepdims=True)
        outs.append(gg * lax.rsqrt(ms + EPS) * nw_ref[:, g * gw:(g + 1) * gw])
    o_ref[...] = jnp.concatenate(outs, axis=1).astype(BF16)


def ssd(proj, cwx, cwb, cbx, cbb, dtb, alog, dsk, nw, batch, seq):
    t = proj.shape[0]
    ts = min(512, seq)
    ns = seq // ts
    row = lambda b, i: b * ns + i
    const = lambda b, i: (0, 0)
    return pl.pallas_call(
        _ssd_body,
        grid=(batch, ns),
        in_specs=[
            pl.BlockSpec((ts, SSM_WIDTH), lambda b, i: (row(b, i), COL_Z // SSM_WIDTH)),
            pl.BlockSpec((ts, SSM_WIDTH), lambda b, i: (row(b, i), COL_XBC // SSM_WIDTH)),
            pl.BlockSpec((ts, 2 * BC_WIDTH), lambda b, i: (row(b, i), (COL_XBC + SSM_WIDTH) // (2 * BC_WIDTH))),
            pl.BlockSpec((ts, LANES), lambda b, i: (row(b, i), COL_DT // LANES)),
            pl.BlockSpec((SSM_CONV, SSM_WIDTH), const),
            pl.BlockSpec((SSM_CONV, 2 * BC_WIDTH), const),
            pl.BlockSpec((1, SSM_WIDTH), const),
            pl.BlockSpec((1, 2 * BC_WIDTH), const),
            pl.BlockSpec((1, LANES), const),
            pl.BlockSpec((1, LANES), const),
            pl.BlockSpec((1, SSM_WIDTH), const),
            pl.BlockSpec((1, SSM_WIDTH), const),
        ],
        out_specs=pl.BlockSpec((ts, SSM_WIDTH), lambda b, i: (row(b, i), 0)),
        out_shape=jax.ShapeDtypeStruct((t, SSM_WIDTH), BF16),
        scratch_shapes=[
            pltpu.VMEM((ts + SUBLANES, SSM_WIDTH), F32),
            pltpu.VMEM((ts + SUBLANES, 2 * BC_WIDTH), F32),
            pltpu.VMEM((SSM_GROUPS, SSM_STATE, SSM_WIDTH // SSM_GROUPS), F32),
            pltpu.VMEM((ts, SSM_WIDTH), F32),
        ],
        compiler_params=_params("arbitrary", "arbitrary"),
        name="ssd",
    )(proj, proj, proj, proj, cwx, cwb, cbx, cbb, dtb, alog, dsk, nw)


def _out_body(x_ref, a_ref, s_ref, wo_ref, fg_ref, wr_ref, br_ref, h_ref, hf_ref, idx_ref, gate_ref):
    tm = x_ref.shape[0]
    h = x_ref[...] + _dot(a_ref[...], wo_ref[0:ATTN_WIDTH, :]) + _dot(s_ref[...], wo_ref[ATTN_WIDTH:, :])
    h_ref[...] = h
    ms = jnp.mean(h * h, axis=-1, keepdims=True)
    hf = h * lax.rsqrt(ms + EPS) * fg_ref[...]
    for s in range(ROW_TILES):
        hf_ref[pl.ds(s, tm, stride=ROW_TILES), :] = hf[:, s * LANES:(s + 1) * LANES]

    hi, lo = _split2(hf)
    whi, wlo = _split2(wr_ref[...])
    logits = _dot(hi, whi) + _dot(hi, wlo) + _dot(lo, whi) + br_ref[...]
    lane = _iota((tm, LANES), 1)
    vals, idxs = [], []
    for _ in range(TOP_K):
        m = jnp.max(logits, axis=-1, keepdims=True)
        am = jnp.min(jnp.where(logits == m, lane, LANES), axis=-1, keepdims=True)
        vals.append(m)
        idxs.append(am)
        logits = jnp.where(lane == am, NEG_BIG * 2.0, logits)
    es = [jnp.exp(v - vals[0]) for v in vals]
    tot = es[0] + es[1] + es[2] + es[3]
    idx_out = jnp.zeros((tm, LANES), jnp.int32)
    gate_out = jnp.zeros((tm, LANES), F32)
    for k in range(TOP_K):
        idx_out = jnp.where(lane == k, idxs[k], idx_out)
        gate_out = jnp.where(lane == k, es[k] / tot, gate_out)
    idx_ref[...] = idx_out
    gate_ref[...] = gate_out


def out_proj_router(x2d, attn, ssm, wo, fg, wr, br):
    t, d = x2d.shape
    tm = min(256, t)
    const = lambda i: (0, 0)
    return pl.pallas_call(
        _out_body,
        grid=(t // tm,),
        in_specs=[
            pl.BlockSpec((tm, d), lambda i: (i, 0)),
            pl.BlockSpec((tm, ATTN_WIDTH), lambda i: (i, 0)),
            pl.BlockSpec((tm, SSM_WIDTH), lambda i: (i, 0)),
            pl.BlockSpec((d, d), const),
            pl.BlockSpec((1, d), const),
            pl.BlockSpec((d, LANES), const),
            pl.BlockSpec((1, LANES), const),
        ],
        out_specs=[
            pl.BlockSpec((tm, d), lambda i: (i, 0)),
            pl.BlockSpec((tm * ROW_TILES, LANES), lambda i: (i, 0)),
            pl.BlockSpec((tm, LANES), lambda i: (i, 0)),
            pl.BlockSpec((tm, LANES), lambda i: (i, 0)),
        ],
        out_shape=[
            jax.ShapeDtypeStruct((t, d), F32),
            jax.ShapeDtypeStruct((t * ROW_TILES, LANES), F32),
            jax.ShapeDtypeStruct((t, LANES), jnp.int32),
            jax.ShapeDtypeStruct((t, LANES), F32),
        ],
        compiler_params=_params("arbitrary"),
        name="out_proj_router",
    )(x2d, attn, ssm, wo, fg, wr, br)


MOE_ROWS = 512
FF_TILE = 512


def _expert_body(be_ref, nv_ref, tok_ref, hf_ref, wg_ref, bg_ref, wu_ref, bu_ref, wd_ref, bd_ref, o_ref,
                 xrow_ref, x_ref, acc_ref, sem):
    b = pl.program_id(0)
    f = pl.program_id(1)
    nf = pl.num_programs(1)
    tm = x_ref.shape[0]

    def row_copy(r, tok):
        return pltpu.make_async_copy(hf_ref.at[pl.ds(tok * ROW_TILES, ROW_TILES)],
                                     xrow_ref.at[pl.ds(r * ROW_TILES, ROW_TILES)], sem)

    @pl.when(b < nv_ref[0])
    def _():
        @pl.when(f == 0)
        def _():
            def issue(r, carry):
                row_copy(r, tok_ref[0, 0, r]).start()
                return carry
            lax.fori_loop(0, tm, issue, 0)

            def drain(r, carry):
                row_copy(r, 0).wait()
                return carry
            lax.fori_loop(0, tm, drain, 0)
            for s in range(ROW_TILES):
                x_ref[:, s * LANES:(s + 1) * LANES] = xrow_ref[pl.ds(s, tm, stride=ROW_TILES), :].astype(BF16)

        x = x_ref[...]
        gl = jnp.minimum(_dot(x, wg_ref[0]) + bg_ref[0], SWIGLU_LIMIT)
        li = jnp.clip(_dot(x, wu_ref[0]) + bu_ref[0], -SWIGLU_LIMIT, SWIGLU_LIMIT)
        hb = gl * _sigmoid(SWIGLU_ALPHA * gl) * (li + 1.0)
        y = _dot(hb.astype(BF16), wd_ref[0])

        @pl.when(f == 0)
        def _():
            acc_ref[...] = y + bd_ref[0]

        @pl.when(f > 0)
        def _():
            acc_ref[...] += y

        @pl.when(f == nf - 1)
        def _():
            for s in range(ROW_TILES):
                o_ref[pl.ds(s, tm, stride=ROW_TILES), :] = acc_ref[:, s * LANES:(s + 1) * LANES]


def experts(block_expert, n_valid, slot_tok, hf_rows, wg, bg, wu, bu, wd, bd):
    n_blocks = block_expert.shape[0]
    tm = MOE_ROWS
    d = D_MODEL
    nf = D_FF // FF_TILE

    def blk(b, nv):
        return jnp.minimum(b, nv[0] - 1)

    def ffi(b, f, nv):
        return jnp.where(b < nv[0], f, nf - 1)

    grid_spec = pltpu.PrefetchScalarGridSpec(
        num_scalar_prefetch=2,
        grid=(n_blocks, nf),
        in_specs=[
            pl.BlockSpec((1, 1, tm), lambda b, f, be, nv: (blk(b, nv), 0, 0), memory_space=pltpu.SMEM),
            pl.BlockSpec(memory_space=pl.ANY),
            pl.BlockSpec((1, d, FF_TILE), lambda b, f, be, nv: (be[blk(b, nv)], 0, ffi(b, f, nv))),
            pl.BlockSpec((1, 1, FF_TILE), lambda b, f, be, nv: (be[blk(b, nv)], 0, ffi(b, f, nv))),
            pl.BlockSpec((1, d, FF_TILE), lambda b, f, be, nv: (be[blk(b, nv)], 0, ffi(b, f, nv))),
            pl.BlockSpec((1, 1, FF_TILE), lambda b, f, be, nv: (be[blk(b, nv)], 0, ffi(b, f, nv))),
            pl.BlockSpec((1, FF_TILE, d), lambda b, f, be, nv: (be[blk(b, nv)], ffi(b, f, nv), 0)),
            pl.BlockSpec((1, 1, d), lambda b, f, be, nv: (be[blk(b, nv)], 0, 0)),
        ],
        out_specs=pl.BlockSpec((tm * ROW_TILES, LANES), lambda b, f, be, nv: (blk(b, nv), 0)),
        scratch_shapes=[
            pltpu.VMEM((tm * ROW_TILES, LANES), F32),
            pltpu.VMEM((tm, d), BF16),
            pltpu.VMEM((tm, d), F32),
            pltpu.SemaphoreType.DMA,
        ],
    )
    return pl.pallas_call(
        _expert_body,
        grid_spec=grid_spec,
        out_shape=jax.ShapeDtypeStruct((n_blocks * tm * ROW_TILES, LANES), F32),
        compiler_params=_params("arbitrary", "arbitrary"),
        name="experts",
    )(block_expert, n_valid, slot_tok, hf_rows, wg, bg, wu, bu, wd, bd)


def _combine_body(pos_ref, h_ref, gate_ref, ys_ref, o_ref, buf_ref, sem):
    tm = h_ref.shape[0]

    def row_copy(k, r, slot):
        return pltpu.make_async_copy(ys_ref.at[pl.ds(slot * ROW_TILES, ROW_TILES)],
                                     buf_ref.at[k, pl.ds(r * ROW_TILES, ROW_TILES)], sem)

    def issue(r, carry):
        for k in range(TOP_K):
            row_copy(k, r, pos_ref[0, 0, r * TOP_K + k]).start()
        return carry
    lax.fori_loop(0, tm, issue, 0)

    def drain(r, carry):
        for k in range(TOP_K):
            row_copy(k, r, 0).wait()
        return carry
    lax.fori_loop(0, tm, drain, 0)

    gates = gate_ref[...]
    for s in range(ROW_TILES):
        acc = h_ref[:, s * LANES:(s + 1) * LANES]
        for k in range(TOP_K):
            acc = acc + gates[:, k:k + 1] * buf_ref[k, pl.ds(s, tm, stride=ROW_TILES), :]
        o_ref[:, s * LANES:(s + 1) * LANES] = acc


def combine(pos, h, gates, ys_rows):
    t, d = h.shape
    tm = min(256, t)
    return pl.pallas_call(
        _combine_body,
        grid=(t // tm,),
        in_specs=[
            pl.BlockSpec((1, 1, tm * TOP_K), lambda i: (i, 0, 0), memory_space=pltpu.SMEM),
            pl.BlockSpec((tm, d), lambda i: (i, 0)),
            pl.BlockSpec((tm, LANES), lambda i: (i, 0)),
            pl.BlockSpec(memory_space=pl.ANY),
        ],
        out_specs=pl.BlockSpec((tm, d), lambda i: (i, 0)),
        out_shape=jax.ShapeDtypeStruct((t, d), F32),
        scratch_shapes=[
            pltpu.VMEM((TOP_K, tm * ROW_TILES, LANES), F32),
            pltpu.SemaphoreType.DMA,
        ],
        compiler_params=_params("arbitrary"),
        name="combine",
    )(pos, h, gates, ys_rows)


def routing_tables(top_idx, t):
    n_assign = t * TOP_K
    e_flat = top_idx.reshape(-1)
    order = jnp.argsort(e_flat).astype(jnp.int32)
    se = e_flat[order]
    counts = jnp.bincount(e_flat, length=N_EXPERTS).astype(jnp.int32)
    padded = (counts + MOE_ROWS - 1) // MOE_ROWS * MOE_ROWS
    starts = jnp.cumsum(counts) - counts
    pends = jnp.cumsum(padded)
    pstarts = pends - padded
    dest = pstarts[se] + jnp.arange(n_assign, dtype=jnp.int32) - starts[se]
    n_blocks = -(-n_assign // MOE_ROWS) + N_EXPERTS
    n_slots = n_blocks * MOE_ROWS
    slot_tok = jnp.zeros((n_slots,), jnp.int32).at[dest].set(order // TOP_K)
    pos = jnp.zeros((n_assign,), jnp.int32).at[order].set(dest)
    block_expert = jnp.minimum(
        jnp.searchsorted(pends, jnp.arange(n_blocks, dtype=jnp.int32) * MOE_ROWS, side="right"),
        N_EXPERTS - 1).astype(jnp.int32)
    n_valid = (pends[-1:] // MOE_ROWS).astype(jnp.int32)
    return block_expert, n_valid, slot_tok.reshape(n_blocks, 1, MOE_ROWS), pos


def _pad_lanes(v, width, value=0.0):
    return jnp.pad(v, ((0, 0), (0, width - v.shape[1])), constant_values=value)


def hybrid_layer(x, attn_norm_w, w_in, q_norm_w, k_norm_w, attn_sinks, conv_w, conv_b, dt_bias, a_log, d_skip,
                 ssm_norm_w, attn_out_norm_w, w_out, ffn_norm_w, w_router, b_router, w_gate, b_gate, w_up, b_up,
                 w_down, b_down):
    batch, seq, d = x.shape
    t = batch * seq
    x2d = x.reshape(t, d)

    q_w, k_w, v_w, z_w, xbc_w, dt_w = jnp.split(
        w_in, [ATTN_WIDTH, ATTN_WIDTH + KV_WIDTH, ATTN_WIDTH + 2 * KV_WIDTH,
               ATTN_WIDTH + 2 * KV_WIDTH + SSM_WIDTH, ATTN_WIDTH + 2 * KV_WIDTH + SSM_WIDTH + CONV_WIDTH], axis=1)
    w_p = jnp.concatenate([q_w, z_w, xbc_w, k_w, v_w, dt_w], axis=1)
    w_p = _pad_lanes(w_p, PROJ_WIDTH).astype(BF16)
    proj = in_proj(x2d, attn_norm_w.reshape(1, d), w_p)

    qw2 = (jnp.tile(q_norm_w, 2) * (HEAD_DIM ** -0.5)).reshape(1, LANES)
    kw2 = jnp.tile(k_norm_w, 2).reshape(1, LANES)
    attn = attention(proj, attn_sinks, qw2, kw2, attn_out_norm_w.reshape(1, ATTN_WIDTH), batch, seq)

    ssm = ssd(
        proj,
        conv_w[:, :SSM_WIDTH], conv_w[:, SSM_WIDTH:],
        conv_b[:SSM_WIDTH].reshape(1, -1), conv_b[SSM_WIDTH:].reshape(1, -1),
        _pad_lanes(dt_bias.reshape(1, -1), LANES), _pad_lanes(a_log.reshape(1, -1), LANES),
        jnp.repeat(d_skip, HEAD_DIM).reshape(1, SSM_WIDTH), ssm_norm_w.reshape(1, SSM_WIDTH),
        batch, seq)

    h, hf_rows, idx_pad, gate_pad = out_proj_router(
        x2d, attn, ssm, w_out.astype(BF16), ffn_norm_w.reshape(1, d),
        _pad_lanes(w_router, LANES), _pad_lanes(b_router.reshape(1, -1), LANES, NEG_BIG))

    block_expert, n_valid, slot_tok, pos = routing_tables(idx_pad[:, :TOP_K], t)
    ys_rows = experts(
        block_expert, n_valid, slot_tok, hf_rows,
        w_gate.astype(BF16), b_gate.reshape(N_EXPERTS, 1, D_FF),
        w_up.astype(BF16), b_up.reshape(N_EXPERTS, 1, D_FF),
        w_down.astype(BF16), b_down.reshape(N_EXPERTS, 1, D_MODEL))

    tm_c = min(256, t)
    out = combine(pos.reshape(t // tm_c, 1, tm_c * TOP_K), h, gate_pad, ys_rows)
    return out.reshape(batch, seq, d)


def kernel(x, attn_norm_w, w_in, q_norm_w, k_norm_w, attn_sinks, conv_w, conv_b, dt_bias, a_log, d_skip, ssm_norm_w,
           attn_out_norm_w, w_out, ffn_norm_w, w_router, b_router, w_gate, b_gate, w_up, b_up, w_down, b_down):
    for i in range(attn_norm_w.shape[0]):
        x = hybrid_layer(x, attn_norm_w[i], w_in[i], q_norm_w[i], k_norm_w[i], attn_sinks[i], conv_w[i], conv_b[i],
                         dt_bias[i], a_log[i], d_skip[i], ssm_norm_w[i], attn_out_norm_w[i], w_out[i], ffn_norm_w[i],
                         w_router[i], b_router[i], w_gate[i], b_gate[i], w_up[i], b_up[i], w_down[i], b_down[i])
    return x
```

```python
import functools

import jax
import jax.numpy as jnp
from jax import lax
from jax.experimental import pallas as pl
from jax.experimental.pallas import tpu as pltpu

F32 = jnp.float32
BF16 = jnp.bfloat16

D_MODEL = 2048
HEAD_DIM = 64
ATTN_WIDTH = 1024
KV_WIDTH = 128
WINDOW = 128
SSM_WIDTH = 1024
SSM_HEADS = 16
SSM_GROUPS = 2
SSM_STATE = 128
SSM_CONV = 4
SSM_CHUNK = 128
BC_WIDTH = SSM_GROUPS * SSM_STATE
CONV_WIDTH = SSM_WIDTH + 2 * BC_WIDTH
N_EXPERTS = 32
TOP_K = 4
D_FF = 2048
SWIGLU_LIMIT = 7.0
SWIGLU_ALPHA = 1.702
EPS = 1e-6

LANES = 128
SUBLANES = 8
VMEM_LIMIT = 56 * 1024 * 1024

PROJ_WIDTH = 4096
COL_Q = 0
COL_Z = ATTN_WIDTH
COL_XBC = COL_Z + SSM_WIDTH
COL_K = COL_XBC + CONV_WIDTH
COL_V = COL_K + KV_WIDTH
COL_DT = COL_V + KV_WIDTH
NEG_BIG = -1e30


def _params(*sem):
    return pltpu.CompilerParams(dimension_semantics=sem, vmem_limit_bytes=VMEM_LIMIT)


def _split2(v):
    hi = v.astype(BF16)
    lo = (v - hi.astype(F32)).astype(BF16)
    return hi, lo


def _split3(v):
    hi = v.astype(BF16)
    r = v - hi.astype(F32)
    mid = r.astype(BF16)
    lo = (r - mid.astype(F32)).astype(BF16)
    return hi, mid, lo


def _dot(a, b):
    return jnp.dot(a, b, preferred_element_type=F32)


def _dot_exact_lhs(parts, m):
    acc = _dot(parts[0], m)
    for p in parts[1:]:
        acc = acc + _dot(p, m)
    return acc


def _sigmoid(v):
    return 1.0 / (1.0 + jnp.exp(-v))


def _iota(shape, dim):
    return lax.broadcasted_iota(jnp.int32, shape, dim)


def _in_proj_body(x_ref, g_ref, w_ref, o_ref, hn_ref):
    @pl.when(pl.program_id(1) == 0)
    def _():
        x = x_ref[...]
        ms = jnp.mean(x * x, axis=-1, keepdims=True)
        hn_ref[...] = (x * lax.rsqrt(ms + EPS) * g_ref[...]).astype(BF16)

    o_ref[...] = _dot(hn_ref[...], w_ref[...])


def in_proj(x2d, g, w_p):
    t, d = x2d.shape
    n = w_p.shape[1]
    tm = min(1024, t)
    tn = 1024
    return pl.pallas_call(
        _in_proj_body,
        grid=(t // tm, n // tn),
        in_specs=[
            pl.BlockSpec((tm, d), lambda i, j: (i, 0)),
            pl.BlockSpec((1, d), lambda i, j: (0, 0)),
            pl.BlockSpec((d, tn), lambda i, j: (0, j)),
        ],
        out_specs=pl.BlockSpec((tm, tn), lambda i, j: (i, j)),
        out_shape=jax.ShapeDtypeStruct((t, n), F32),
        scratch_shapes=[pltpu.VMEM((tm, d), BF16)],
        compiler_params=_params("arbitrary", "arbitrary"),
        name="in_proj",
    )(x2d, g, w_p)


def _head_pair_norm(v, w, blockdiag):
    ss = _dot_exact_lhs(_split2(v * v), blockdiag)
    return v * lax.rsqrt(ss * (1.0 / HEAD_DIM) + EPS) * w


def _attn_body(sinks_ref, q_ref, kvc_ref, kvp_ref, qw_ref, kw_ref, onw_ref, o_ref, acc_ref):
    i = pl.program_id(1)
    tq = q_ref.shape[0]
    nsub = tq // WINDOW
    nk = tq + WINDOW
    lo_lanes = _iota((1, LANES), 1) < HEAD_DIM
    blockdiag = jnp.where(
        _iota((LANES, LANES), 0) // HEAD_DIM == _iota((LANES, LANES), 1) // HEAD_DIM, 1.0, 0.0
    ).astype(BF16)

    kv_all = jnp.concatenate([kvp_ref[...], kvc_ref[...]], axis=0)
    k_all = _head_pair_norm(kv_all[:, :KV_WIDTH], kw_ref[...], blockdiag)
    v_all = kv_all[:, KV_WIDTH:]
    kt = k_all.T
    zero_half = jnp.zeros((HEAD_DIM, nk), F32)
    kt_var = [
        [jnp.concatenate([kt[j * HEAD_DIM:(j + 1) * HEAD_DIM], zero_half], axis=0).astype(BF16),
         jnp.concatenate([zero_half, kt[j * HEAD_DIM:(j + 1) * HEAD_DIM]], axis=0).astype(BF16)]
        for j in range(2)
    ]
    v_sw = pltpu.roll(v_all, HEAD_DIM, axis=1)
    v_var = [
        [jnp.where(lo_lanes, v_all, 0.0).astype(BF16), jnp.where(lo_lanes, 0.0, v_sw).astype(BF16)],
        [jnp.where(lo_lanes, v_sw, 0.0).astype(BF16), jnp.where(lo_lanes, 0.0, v_all).astype(BF16)],
    ]

    qn = [
        _head_pair_norm(q_ref[:, c * LANES:(c + 1) * LANES], qw_ref[...], blockdiag).astype(BF16)
        for c in range(ATTN_WIDTH // LANES)
    ]

    rows = 4 * WINDOW
    qi = _iota((rows, 2 * WINDOW), 0) % WINDOW
    kj = _iota((rows, 2 * WINDOW), 1)
    band = (kj > qi) & (kj <= qi + WINDOW)
    first_min = jnp.where(i == 0, WINDOW, 0)
    band_first = band & (kj >= first_min)

    for s in range(nsub):
        mask = band_first if s == 0 else band
        for j in range(2):
            qs = jnp.concatenate([qn[4 * j + c][s * WINDOW:(s + 1) * WINDOW] for c in range(4)], axis=0)
            out = None
            for par in range(2):
                sink = jnp.concatenate(
                    [jnp.full((WINDOW, 1), sinks_ref[8 * j + 2 * c + par], F32) for c in range(4)], axis=0)
                sc = _dot(qs, kt_var[j][par][:, s * WINDOW:s * WINDOW + 2 * WINDOW])
                sc = jnp.where(mask, sc, NEG_BIG)
                m = jnp.maximum(jnp.max(sc, axis=-1, keepdims=True), sink)
                p = jnp.exp(sc - m)
                den = jnp.sum(p, axis=-1, keepdims=True) + jnp.exp(sink - m)
                pv = _dot(p.astype(BF16), v_var[j][par][s * WINDOW:s * WINDOW + 2 * WINDOW])
                pv = pv * (1.0 / den)
                out = pv if out is None else out + pv
            for c in range(4):
                acc_ref[s * WINDOW:(s + 1) * WINDOW, (4 * j + c) * LANES:(4 * j + c + 1) * LANES] = (
                    out[c * WINDOW:(c + 1) * WINDOW])

    a = acc_ref[...]
    ms = jnp.mean(a * a, axis=-1, keepdims=True)
    o_ref[...] = (a * lax.rsqrt(ms + EPS) * onw_ref[...]).astype(BF16)


def attention(proj, sinks, qw2, kw2, onw, batch, seq):
    t = proj.shape[0]
    tq = min(512, seq)
    nq = seq // tq
    sub = tq // WINDOW
    kv_col = COL_K // (2 * KV_WIDTH)
    grid_spec = pltpu.PrefetchScalarGridSpec(
        num_scalar_prefetch=1,
        grid=(batch, nq),
        in_specs=[
            pl.BlockSpec((tq, ATTN_WIDTH), lambda b, i, s: (b * nq + i, COL_Q // ATTN_WIDTH)),
            pl.BlockSpec((tq, 2 * KV_WIDTH), lambda b, i, s: (b * nq + i, kv_col)),
            pl.BlockSpec((WINDOW, 2 * KV_WIDTH),
                         lambda b, i, s: (b * nq * sub + jnp.maximum(i * sub - 1, 0), kv_col)),
            pl.BlockSpec((1, LANES), lambda b, i, s: (0, 0)),
            pl.BlockSpec((1, LANES), lambda b, i, s: (0, 0)),
            pl.BlockSpec((1, ATTN_WIDTH), lambda b, i, s: (0, 0)),
        ],
        out_specs=pl.BlockSpec((tq, ATTN_WIDTH), lambda b, i, s: (b * nq + i, 0)),
        scratch_shapes=[pltpu.VMEM((tq, ATTN_WIDTH), F32)],
    )
    return pl.pallas_call(
        _attn_body,
        grid_spec=grid_spec,
        out_shape=jax.ShapeDtypeStruct((t, ATTN_WIDTH), BF16),
        compiler_params=_params("arbitrary", "arbitrary"),
        name="attention",
    )(sinks, proj, proj, proj, qw2, kw2, onw)


def _ssd_body(z_ref, xs_ref, bc_ref, dt_ref, cwx_ref, cwb_ref, cbx_ref, cbb_ref, dtb_ref, alog_ref,
              dsk_ref, nw_ref, o_ref, xpx_ref, xpb_ref, st_ref, y_ref):
    i = pl.program_id(1)
    ts = xs_ref.shape[0]
    nch = ts // SSM_CHUNK
    gw = SSM_WIDTH // SSM_GROUPS
    pad = SUBLANES

    @pl.when(i == 0)
    def _():
        xpx_ref[0:pad, :] = jnp.zeros((pad, SSM_WIDTH), F32)
        xpb_ref[0:pad, :] = jnp.zeros((pad, 2 * BC_WIDTH), F32)
        st_ref[...] = jnp.zeros(st_ref.shape, F32)

    xpx_ref[pad:pad + ts, :] = xs_ref[...]
    xpb_ref[pad:pad + ts, :] = bc_ref[...]

    def conv_silu(xp_ref, w_ref, b_ref):
        acc = b_ref[...] + w_ref[SSM_CONV - 1:SSM_CONV, :] * xp_ref[pad:pad + ts, :]
        for k in range(SSM_CONV - 1):
            off = pad - (SSM_CONV - 1) + k
            acc = acc + w_ref[k:k + 1, :] * xp_ref[off:off + ts, :]
        return acc * _sigmoid(acc)

    xc = conv_silu(xpx_ref, cwx_ref, cbx_ref)
    bcc = conv_silu(xpb_ref, cwb_ref, cbb_ref)
    xpx_ref[0:pad, :] = xs_ref[ts - pad:ts, :]
    xpb_ref[0:pad, :] = bc_ref[ts - pad:ts, :]

    dtr = dt_ref[...] + dtb_ref[...]
    dt = jnp.maximum(dtr, 0.0) + jnp.log(1.0 + jnp.exp(-jnp.abs(dtr)))
    a = -jnp.exp(alog_ref[...])
    da = dt * a

    ri = _iota((ts, ts), 0)
    ci = _iota((ts, ts), 1)
    same = (ri // SSM_CHUNK) == (ci // SSM_CHUNK)
    tri_blk = jnp.where(same & (ci <= ri), 1.0, 0.0).astype(BF16)
    all_blk = jnp.where(same, 1.0, 0.0).astype(BF16)
    da3 = _split3(da)
    cs = _dot(tri_blk, da3[0]) + _dot(tri_blk, da3[1]) + _dot(tri_blk, da3[2])
    cl = _dot(all_blk, da3[0]) + _dot(all_blk, da3[1]) + _dot(all_blk, da3[2])
    ecs = jnp.exp(cs)
    dte = jnp.exp(cl - cs)

    expand = jnp.where(_iota((LANES, SSM_WIDTH), 1) // HEAD_DIM == _iota((LANES, SSM_WIDTH), 0),
                       1.0, 0.0).astype(BF16)
    dt_e = _dot_exact_lhs(_split2(dt), expand)
    dtd_e = _dot_exact_lhs(_split2(dt * dte), expand)
    ecs_e = _dot_exact_lhs(_split2(ecs), expand)
    xf = xc * dt_e
    xd = (xc * dtd_e).astype(BF16)

    lo_lanes = _iota((1, LANES), 1) < HEAD_DIM
    li = _iota((SSM_CHUNK, SSM_CHUNK), 0)
    lj = _iota((SSM_CHUNK, SSM_CHUNK), 1)
    tril = lj <= li
    upper = jnp.where(li <= lj, 1.0, 0.0).astype(BF16)

    for c in range(nch):
        r0 = c * SSM_CHUNK
        da_t = da[r0:r0 + SSM_CHUNK].T
        cs_t = _dot_exact_lhs(_split3(da_t), upper)
        cs_c = cs[r0:r0 + SSM_CHUNK]
        for g in range(SSM_GROUPS):
            b_c = bcc[r0:r0 + SSM_CHUNK, g * SSM_STATE:(g + 1) * SSM_STATE]
            c_c = bcc[r0:r0 + SSM_CHUNK, BC_WIDTH + g * SSM_STATE:BC_WIDTH + (g + 1) * SSM_STATE].astype(BF16)
            b_t = b_c.T.astype(BF16)
            cb = _dot(c_c, b_t)
            st = st_ref[g]
            y_off = _dot(c_c, st.astype(BF16)) * ecs_e[r0:r0 + SSM_CHUNK, g * gw:(g + 1) * gw]
            s_new = _dot(b_t, xd[r0:r0 + SSM_CHUNK, g * gw:(g + 1) * gw])
            parts = []
            for p in range(4):
                h0 = 8 * g + 2 * p
                ms = []
                for h in (h0, h0 + 1):
                    diff = cs_c[:, h:h + 1] - cs_t[h:h + 1, :]
                    ms.append((cb * jnp.exp(jnp.where(tril, diff, NEG_BIG))).astype(BF16))
                m2 = jnp.concatenate(ms, axis=1)
                x2 = xf[r0:r0 + SSM_CHUNK, h0 * HEAD_DIM:(h0 + 2) * HEAD_DIM]
                xbd = jnp.concatenate([jnp.where(lo_lanes, x2, 0.0), jnp.where(lo_lanes, 0.0, x2)],
                                      axis=0).astype(BF16)
                parts.append(_dot(m2, xbd))
            y_ref[r0:r0 + SSM_CHUNK, g * gw:(g + 1) * gw] = jnp.concatenate(parts, axis=1) + y_off
            dec = ecs_e[r0 + SSM_CHUNK - 1:r0 + SSM_CHUNK, g * gw:(g + 1) * gw]
            st_ref[g] = st * dec + s_new

    y = y_ref[...] + xc * dsk_ref[...]
    z = z_ref[...]
    gated = y * (z * _sigmoid(z))
    outs = []
    for g in range(SSM_GROUPS):
        gg = gated[:, g * gw:(g + 1) * gw]
        ms = jnp.mean(gg * gg, axis=-1, keepdims=True)
        outs.append(gg * lax.rsqrt(ms + EPS) * nw_ref[:, g * gw:(g + 1) * gw])
    o_ref[...] = jnp.concatenate(outs, axis=1).astype(BF16)


def ssd(proj, cwx, cwb, cbx, cbb, dtb, alog, dsk, nw, batch, seq):
    t = proj.shape[0]
    ts = min(512, seq)
    ns = seq // ts
    row = lambda b, i: b * ns + i
    const = lambda b, i: (0, 0)
    return pl.pallas_call(
        _ssd_body,
        grid=(batch, ns),
        in_specs=[
            pl.BlockSpec((ts, SSM_WIDTH), lambda b, i: (row(b, i), COL_Z // SSM_WIDTH)),
            pl.BlockSpec((ts, SSM_WIDTH), lambda b, i: (row(b, i), COL_XBC // SSM_WIDTH)),
            pl.BlockSpec((ts, 2 * BC_WIDTH), lambda b, i: (row(b, i), (COL_XBC + SSM_WIDTH) // (2 * BC_WIDTH))),
            pl.BlockSpec((ts, LANES), lambda b, i: (row(b, i), COL_DT // LANES)),
            pl.BlockSpec((SSM_CONV, SSM_WIDTH), const),
            pl.BlockSpec((SSM_CONV, 2 * BC_WIDTH), const),
            pl.BlockSpec((1, SSM_WIDTH), const),
            pl.BlockSpec((1, 2 * BC_WIDTH), const),
            pl.BlockSpec((1, LANES), const),
            pl.BlockSpec((1, LANES), const),
            pl.BlockSpec((1, SSM_WIDTH), const),
            pl.BlockSpec((1, SSM_WIDTH), const),
        ],
        out_specs=pl.BlockSpec((ts, SSM_WIDTH), lambda b, i: (row(b, i), 0)),
        out_shape=jax.ShapeDtypeStruct((t, SSM_WIDTH), BF16),
        scratch_shapes=[
            pltpu.VMEM((ts + SUBLANES, SSM_WIDTH), F32),
            pltpu.VMEM((ts + SUBLANES, 2 * BC_WIDTH), F32),
            pltpu.VMEM((SSM_GROUPS, SSM_STATE, SSM_WIDTH // SSM_GROUPS), F32),
            pltpu.VMEM((ts, SSM_WIDTH), F32),
        ],
        compiler_params=_params("arbitrary", "arbitrary"),
        name="ssd",
    )(proj, proj, proj, proj, cwx, cwb, cbx, cbb, dtb, alog, dsk, nw)


def _out_body(x_ref, a_ref, s_ref, wo_ref, fg_ref, wr_ref, br_ref, h_ref, hf_ref, idx_ref, gate_ref):
    tm = x_ref.shape[0]
    h = x_ref[...] + _dot(a_ref[...], wo_ref[0:ATTN_WIDTH, :]) + _dot(s_ref[...], wo_ref[ATTN_WIDTH:, :])
    h_ref[...] = h
    ms = jnp.mean(h * h, axis=-1, keepdims=True)
    hf = h * lax.rsqrt(ms + EPS) * fg_ref[...]
    hi, lo = _split2(hf)
    hi32 = hi.astype(F32)
    hf_ref[...] = (pltpu.bitcast(hi32[:, D_MODEL // 2:], jnp.uint32)
                   | (pltpu.bitcast(hi32[:, :D_MODEL // 2], jnp.uint32) >> 16))

    whi, wlo = _split2(wr_ref[...])
    logits = _dot(hi, whi) + _dot(hi, wlo) + _dot(lo, whi) + br_ref[...]
    lane = _iota((tm, LANES), 1)
    vals, idxs = [], []
    for _ in range(TOP_K):
        m = jnp.max(logits, axis=-1, keepdims=True)
        am = jnp.min(jnp.where(logits == m, lane, LANES), axis=-1, keepdims=True)
        vals.append(m)
        idxs.append(am)
        logits = jnp.where(lane == am, NEG_BIG * 2.0, logits)
    es = [jnp.exp(v - vals[0]) for v in vals]
    tot = es[0] + es[1] + es[2] + es[3]
    idx_out = jnp.zeros((tm, LANES), jnp.int32)
    gate_out = jnp.zeros((tm, LANES), F32)
    for k in range(TOP_K):
        idx_out = jnp.where(lane == k, idxs[k], idx_out)
        gate_out = jnp.where(lane == k, es[k] / tot, gate_out)
    idx_ref[...] = idx_out
    gate_ref[...] = gate_out


def out_proj_router(x2d, attn, ssm, wo, fg, wr, br):
    t, d = x2d.shape
    tm = min(256, t)
    const = lambda i: (0, 0)
    return pl.pallas_call(
        _out_body,
        grid=(t // tm,),
        in_specs=[
            pl.BlockSpec((tm, d), lambda i: (i, 0)),
            pl.BlockSpec((tm, ATTN_WIDTH), lambda i: (i, 0)),
            pl.BlockSpec((tm, SSM_WIDTH), lambda i: (i, 0)),
            pl.BlockSpec((d, d), const),
            pl.BlockSpec((1, d), const),
            pl.BlockSpec((d, LANES), const),
            pl.BlockSpec((1, LANES), const),
        ],
        out_specs=[
            pl.BlockSpec((tm, d), lambda i: (i, 0)),
            pl.BlockSpec((tm, d // 2), lambda i: (i, 0)),
            pl.BlockSpec((tm, LANES), lambda i: (i, 0)),
            pl.BlockSpec((tm, LANES), lambda i: (i, 0)),
        ],
        out_shape=[
            jax.ShapeDtypeStruct((t, d), F32),
            jax.ShapeDtypeStruct((t, d // 2), jnp.uint32),
            jax.ShapeDtypeStruct((t, LANES), jnp.int32),
            jax.ShapeDtypeStruct((t, LANES), F32),
        ],
        compiler_params=_params("arbitrary"),
        name="out_proj_router",
    )(x2d, attn, ssm, wo, fg, wr, br)


MOE_ROWS = 512
FF_TILE = 512
N_FF_TILES = D_FF // FF_TILE


def _expert_body(be_ref, nv_ref, tok_ref, tokn_ref, dst_ref, hf_ref, wg_ref, wu_ref, wd_ref, bg_ref, bu_ref, bd_ref,
                 yk_ref, xrow_ref, x_ref, h_ref, y_ref, gu_ref, dn_ref, row_sem, out_sem, gu_sem, dn_sem):
    b = pl.program_id(0)
    nv = nv_ref[0]
    tm = x_ref.shape[0]

    def rows_start(table_ref, slot):
        def issue(r, carry):
            tok = table_ref[0, 0, r]
            pltpu.make_async_copy(hf_ref.at[pl.ds(tok, 1)], xrow_ref.at[slot, pl.ds(r, 1)], row_sem.at[slot]).start()
            return carry
        lax.fori_loop(0, tm, issue, 0, unroll=8)

    def rows_wait(slot):
        pltpu.make_async_copy(hf_ref.at[pl.ds(0, tm)], xrow_ref.at[slot], row_sem.at[slot]).wait()

    def out_start():
        def issue(r, carry):
            pltpu.make_async_copy(y_ref.at[pl.ds(r, 1)], yk_ref.at[pl.ds(dst_ref[0, 0, r], 1)], out_sem).start()
            return carry
        lax.fori_loop(0, tm, issue, 0, unroll=8)

    def out_wait():
        pltpu.make_async_copy(y_ref, yk_ref.at[pl.ds(0, tm)], out_sem).wait()

    def gu_copies(e, j, slot):
        cols = pl.ds(j * FF_TILE, FF_TILE)
        return (pltpu.make_async_copy(wg_ref.at[e, :, cols], gu_ref.at[slot, 0], gu_sem.at[slot]),
                pltpu.make_async_copy(wu_ref.at[e, :, cols], gu_ref.at[slot, 1], gu_sem.at[slot]))

    def dn_copy(e, j, slot):
        return pltpu.make_async_copy(wd_ref.at[e, :, pl.ds(j * FF_TILE, FF_TILE)], dn_ref.at[slot], dn_sem.at[slot])

    @pl.when(b < nv)
    def _():
        e = be_ref[b]
        slot = b % 2

        @pl.when(b == 0)
        def _():
            rows_start(tok_ref, 0)
            for cp in gu_copies(e, 0, 0):
                cp.start()

        rows_wait(slot)
        w = xrow_ref[slot]
        half = D_MODEL // 2
        x_ref[:, 0:half] = pltpu.bitcast(w << 16, F32).astype(BF16)
        x_ref[:, half:] = pltpu.bitcast(w & jnp.uint32(0xFFFF0000), F32).astype(BF16)

        @pl.when(b + 1 < nv)
        def _():
            rows_start(tokn_ref, 1 - slot)

        x = x_ref[...]
        for j in range(N_FF_TILES):
            for cp in gu_copies(e, j, j % 2):
                cp.wait()
            if j + 1 < N_FF_TILES:
                for cp in gu_copies(e, j + 1, (j + 1) % 2):
                    cp.start()
            else:
                dn_copy(e, 0, 0).start()
            cols = slice(j * FF_TILE, (j + 1) * FF_TILE)
            gl = jnp.minimum(_dot(x, gu_ref[j % 2, 0]) + bg_ref[0, :, cols], SWIGLU_LIMIT)
            li = jnp.clip(_dot(x, gu_ref[j % 2, 1]) + bu_ref[0, :, cols], -SWIGLU_LIMIT, SWIGLU_LIMIT)
            h_ref[:, cols] = (gl * _sigmoid(SWIGLU_ALPHA * gl) * (li + 1.0)).astype(BF16)

        @pl.when(b > 0)
        def _():
            out_wait()

        h = h_ref[...]
        for j in range(N_FF_TILES):
            dn_copy(e, j, j % 2).wait()
            if j + 1 < N_FF_TILES:
                dn_copy(e, j + 1, (j + 1) % 2).start()
            else:
                @pl.when(b + 1 < nv)
                def _():
                    for cp in gu_copies(be_ref[b + 1], 0, 0):
                        cp.start()
            cols = slice(j * FF_TILE, (j + 1) * FF_TILE)
            y_ref[:, cols] = _dot(h, dn_ref[j % 2]) + bd_ref[0, :, cols]

        out_start()

        @pl.when(b + 1 == nv)
        def _():
            out_wait()


def experts(block_expert, n_valid, slot_tok, slot_dst, hf_packed, wg, bg, wu, bu, wd, bd, n_out_rows):
    n_blocks = block_expert.shape[0]
    tm = MOE_ROWS
    d = D_MODEL

    def blk(b, nv):
        return jnp.minimum(b, nv[0] - 1)

    any_spec = pl.BlockSpec(memory_space=pl.ANY)
    grid_spec = pltpu.PrefetchScalarGridSpec(
        num_scalar_prefetch=2,
        grid=(n_blocks,),
        in_specs=[
            pl.BlockSpec((1, 1, tm), lambda b, be, nv: (blk(b, nv), 0, 0), memory_space=pltpu.SMEM),
            pl.BlockSpec((1, 1, tm), lambda b, be, nv: (blk(b + 1, nv), 0, 0), memory_space=pltpu.SMEM),
            pl.BlockSpec((1, 1, tm), lambda b, be, nv: (blk(b, nv), 0, 0), memory_space=pltpu.SMEM),
            any_spec, any_spec, any_spec, any_spec,
            pl.BlockSpec((1, 1, D_FF), lambda b, be, nv: (be[blk(b, nv)], 0, 0)),
            pl.BlockSpec((1, 1, D_FF), lambda b, be, nv: (be[blk(b, nv)], 0, 0)),
            pl.BlockSpec((1, 1, d), lambda b, be, nv: (be[blk(b, nv)], 0, 0)),
        ],
        out_specs=any_spec,
        scratch_shapes=[
            pltpu.VMEM((2, tm, d // 2), jnp.uint32),
            pltpu.VMEM((tm, d), BF16),
            pltpu.VMEM((tm, D_FF), BF16),
            pltpu.VMEM((tm, d), F32),
            pltpu.VMEM((2, 2, d, FF_TILE), BF16),
            pltpu.VMEM((2, D_FF, FF_TILE), BF16),
            pltpu.SemaphoreType.DMA((2,)),
            pltpu.SemaphoreType.DMA,
            pltpu.SemaphoreType.DMA((2,)),
            pltpu.SemaphoreType.DMA((2,)),
        ],
    )
    return pl.pallas_call(
        _expert_body,
        grid_spec=grid_spec,
        out_shape=jax.ShapeDtypeStruct((n_out_rows, d), F32),
        compiler_params=_params("arbitrary"),
        name="experts",
    )(block_expert, n_valid, slot_tok, slot_tok, slot_dst, hf_packed, wg, wu, wd, bg, bu, bd)


COMBINE_ROWS = 256


def _combine_body(h_ref, gate_ref, yk_ref, o_ref):
    tm = h_ref.shape[0]
    gates = gate_ref[...]
    acc = h_ref[...]
    for k in range(TOP_K):
        acc = acc + gates[:, k:k + 1] * yk_ref[k * tm:(k + 1) * tm, :]
    o_ref[...] = acc


def combine(h, gates, yk):
    t, d = h.shape
    tm = min(COMBINE_ROWS, t)
    return pl.pallas_call(
        _combine_body,
        grid=(t // tm,),
        in_specs=[
            pl.BlockSpec((tm, d), lambda i: (i, 0)),
            pl.BlockSpec((tm, LANES), lambda i: (i, 0)),
            pl.BlockSpec((TOP_K * tm, d), lambda i: (i, 0)),
        ],
        out_specs=pl.BlockSpec((tm, d), lambda i: (i, 0)),
        out_shape=jax.ShapeDtypeStruct((t, d), F32),
        compiler_params=_params("arbitrary"),
        name="combine",
    )(h, gates, yk)


def routing_tables(top_idx, t):
    n_assign = t * TOP_K
    tm_c = min(COMBINE_ROWS, t)
    e_flat = top_idx.reshape(-1).astype(jnp.int32)
    se, order = lax.sort((e_flat, jnp.arange(n_assign, dtype=jnp.int32)), num_keys=1, is_stable=True)
    bounds = jnp.searchsorted(se, jnp.arange(N_EXPERTS + 1, dtype=jnp.int32), side="left").astype(jnp.int32)
    starts = bounds[:-1]
    counts = bounds[1:] - starts
    padded = (counts + MOE_ROWS - 1) // MOE_ROWS * MOE_ROWS
    pends = jnp.cumsum(padded)
    pstarts = pends - padded
    n_blocks = -(-n_assign // MOE_ROWS) + N_EXPERTS
    block_expert = jnp.minimum(
        jnp.searchsorted(pends, jnp.arange(n_blocks, dtype=jnp.int32) * MOE_ROWS, side="right"),
        N_EXPERTS - 1).astype(jnp.int32)
    n_valid = (pends[-1:] // MOE_ROWS).astype(jnp.int32)

    row = jnp.arange(MOE_ROWS, dtype=jnp.int32)[None, :]
    e_b = block_expert[:, None]
    local = jnp.arange(n_blocks, dtype=jnp.int32)[:, None] * MOE_ROWS + row - pstarts[e_b]
    used = (local >= 0) & (local < counts[e_b])
    asg = order[jnp.clip(starts[e_b] + local, 0, n_assign - 1)]
    tok = asg // TOP_K
    dst = (tok // tm_c) * (TOP_K * tm_c) + (asg % TOP_K) * tm_c + tok % tm_c
    slot_tok = jnp.where(used, tok, 0).reshape(n_blocks, 1, MOE_ROWS)
    slot_dst = jnp.where(used, dst, n_assign + row).reshape(n_blocks, 1, MOE_ROWS)
    return block_expert, n_valid, slot_tok, slot_dst


def _pad_lanes(v, width, value=0.0):
    return jnp.pad(v, ((0, 0), (0, width - v.shape[1])), constant_values=value)


def hybrid_layer(x, attn_norm_w, w_in, q_norm_w, k_norm_w, attn_sinks, conv_w, conv_b, dt_bias, a_log, d_skip,
                 ssm_norm_w, attn_out_norm_w, w_out, ffn_norm_w, w_router, b_router, w_gate, b_gate, w_up, b_up,
                 w_down, b_down):
    batch, seq, d = x.shape
    t = batch * seq
    x2d = x.reshape(t, d)

    q_w, k_w, v_w, z_w, xbc_w, dt_w = jnp.split(
        w_in, [ATTN_WIDTH, ATTN_WIDTH + KV_WIDTH, ATTN_WIDTH + 2 * KV_WIDTH,
               ATTN_WIDTH + 2 * KV_WIDTH + SSM_WIDTH, ATTN_WIDTH + 2 * KV_WIDTH + SSM_WIDTH + CONV_WIDTH], axis=1)
    w_p = jnp.concatenate([q_w, z_w, xbc_w, k_w, v_w, dt_w], axis=1)
    w_p = _pad_lanes(w_p, PROJ_WIDTH).astype(BF16)
    proj = in_proj(x2d, attn_norm_w.reshape(1, d), w_p)

    qw2 = (jnp.tile(q_norm_w, 2) * (HEAD_DIM ** -0.5)).reshape(1, LANES)
    kw2 = jnp.tile(k_norm_w, 2).reshape(1, LANES)
    attn = attention(proj, attn_sinks, qw2, kw2, attn_out_norm_w.reshape(1, ATTN_WIDTH), batch, seq)

    ssm = ssd(
        proj,
        conv_w[:, :SSM_WIDTH], conv_w[:, SSM_WIDTH:],
        conv_b[:SSM_WIDTH].reshape(1, -1), conv_b[SSM_WIDTH:].reshape(1, -1),
        _pad_lanes(dt_bias.reshape(1, -1), LANES), _pad_lanes(a_log.reshape(1, -1), LANES),
        jnp.repeat(d_skip, HEAD_DIM).reshape(1, SSM_WIDTH), ssm_norm_w.reshape(1, SSM_WIDTH),
        batch, seq)

    h, hf_packed, idx_pad, gate_pad = out_proj_router(
        x2d, attn, ssm, w_out.astype(BF16), ffn_norm_w.reshape(1, d),
        _pad_lanes(w_router, LANES), _pad_lanes(b_router.reshape(1, -1), LANES, NEG_BIG))

    block_expert, n_valid, slot_tok, slot_dst = routing_tables(idx_pad[:, :TOP_K], t)
    yk = experts(
        block_expert, n_valid, slot_tok, slot_dst, hf_packed,
        w_gate.astype(BF16), b_gate.reshape(N_EXPERTS, 1, D_FF),
        w_up.astype(BF16), b_up.reshape(N_EXPERTS, 1, D_FF),
        w_down.astype(BF16), b_down.reshape(N_EXPERTS, 1, D_MODEL),
        t * TOP_K + MOE_ROWS)

    out = combine(h, gate_pad, yk)
    return out.reshape(batch, seq, d)


def kernel(x, attn_norm_w, w_in, q_norm_w, k_norm_w, attn_sinks, conv_w, conv_b, dt_bias, a_log, d_skip, ssm_norm_w,
           attn_out_norm_w, w_out, ffn_norm_w, w_router, b_router, w_gate, b_gate, w_up, b_up, w_down, b_down):
    for i in range(attn_norm_w.shape[0]):
        x = hybrid_layer(x, attn_norm_w[i], w_in[i], q_norm_w[i], k_norm_w[i], attn_sinks[i], conv_w[i], conv_b[i],
                         dt_bias[i], a_log[i], d_skip[i], ssm_norm_w[i], attn_out_norm_w[i], w_out[i], ffn_norm_w[i],
                         w_router[i], b_router[i], w_gate[i], b_gate[i], w_up[i], b_up[i], w_down[i], b_down[i])
    return x
```

```python
import functools

import jax
import jax.numpy as jnp
from jax import lax
from jax.experimental import pallas as pl
from jax.experimental.pallas import tpu as pltpu

F32 = jnp.float32
BF16 = jnp.bfloat16

D_MODEL = 2048
HEAD_DIM = 64
ATTN_WIDTH = 1024
KV_WIDTH = 128
WINDOW = 128
SSM_WIDTH = 1024
SSM_HEADS = 16
SSM_GROUPS = 2
SSM_STATE = 128
SSM_CONV = 4
SSM_CHUNK = 128
BC_WIDTH = SSM_GROUPS * SSM_STATE
CONV_WIDTH = SSM_WIDTH + 2 * BC_WIDTH
N_EXPERTS = 32
TOP_K = 4
D_FF = 2048
SWIGLU_LIMIT = 7.0
SWIGLU_ALPHA = 1.702
EPS = 1e-6

LANES = 128
SUBLANES = 8
VMEM_LIMIT = 56 * 1024 * 1024

PROJ_WIDTH = 4096
COL_Q = 0
COL_Z = ATTN_WIDTH
COL_XBC = COL_Z + SSM_WIDTH
COL_K = COL_XBC + CONV_WIDTH
COL_V = COL_K + KV_WIDTH
COL_DT = COL_V + KV_WIDTH
NEG_BIG = -1e30


def _params(*sem):
    return pltpu.CompilerParams(dimension_semantics=sem, vmem_limit_bytes=VMEM_LIMIT)


def _split2(v):
    hi = v.astype(BF16)
    lo = (v - hi.astype(F32)).astype(BF16)
    return hi, lo


def _split3(v):
    hi = v.astype(BF16)
    r = v - hi.astype(F32)
    mid = r.astype(BF16)
    lo = (r - mid.astype(F32)).astype(BF16)
    return hi, mid, lo


def _dot(a, b):
    return jnp.dot(a, b, preferred_element_type=F32)


def _dot_exact_lhs(parts, m):
    acc = _dot(parts[0], m)
    for p in parts[1:]:
        acc = acc + _dot(p, m)
    return acc


def _sigmoid(v):
    return 1.0 / (1.0 + jnp.exp(-v))


def _iota(shape, dim):
    return lax.broadcasted_iota(jnp.int32, shape, dim)


def _in_proj_body(x_ref, g_ref, w_ref, o_ref, hn_ref):
    @pl.when(pl.program_id(1) == 0)
    def _():
        x = x_ref[...]
        ms = jnp.mean(x * x, axis=-1, keepdims=True)
        hn_ref[...] = (x * lax.rsqrt(ms + EPS) * g_ref[...]).astype(BF16)

    o_ref[...] = _dot(hn_ref[...], w_ref[...])


def in_proj(x2d, g, w_p):
    t, d = x2d.shape
    n = w_p.shape[1]
    tm = min(1024, t)
    tn = 1024
    return pl.pallas_call(
        _in_proj_body,
        grid=(t // tm, n // tn),
        in_specs=[
            pl.BlockSpec((tm, d), lambda i, j: (i, 0)),
            pl.BlockSpec((1, d), lambda i, j: (0, 0)),
            pl.BlockSpec((d, tn), lambda i, j: (0, j)),
        ],
        out_specs=pl.BlockSpec((tm, tn), lambda i, j: (i, j)),
        out_shape=jax.ShapeDtypeStruct((t, n), F32),
        scratch_shapes=[pltpu.VMEM((tm, d), BF16)],
        compiler_params=_params("arbitrary", "arbitrary"),
        name="in_proj",
    )(x2d, g, w_p)


def _head_pair_norm(v, w, blockdiag):
    ss = _dot_exact_lhs(_split2(v * v), blockdiag)
    return v * lax.rsqrt(ss * (1.0 / HEAD_DIM) + EPS) * w


def _attn_body(sinks_ref, q_ref, kvc_ref, kvp_ref, qw_ref, kw_ref, onw_ref, o_ref, acc_ref):
    i = pl.program_id(1)
    tq = q_ref.shape[0]
    nsub = tq // WINDOW
    nk = tq + WINDOW
    lo_lanes = _iota((1, LANES), 1) < HEAD_DIM
    blockdiag = jnp.where(
        _iota((LANES, LANES), 0) // HEAD_DIM == _iota((LANES, LANES), 1) // HEAD_DIM, 1.0, 0.0
    ).astype(BF16)

    kv_all = jnp.concatenate([kvp_ref[...], kvc_ref[...]], axis=0)
    k_all = _head_pair_norm(kv_all[:, :KV_WIDTH], kw_ref[...], blockdiag)
    v_all = kv_all[:, KV_WIDTH:]
    kt = k_all.T
    zero_half = jnp.zeros((HEAD_DIM, nk), F32)
    kt_var = [
        [jnp.concatenate([kt[j * HEAD_DIM:(j + 1) * HEAD_DIM], zero_half], axis=0).astype(BF16),
         jnp.concatenate([zero_half, kt[j * HEAD_DIM:(j + 1) * HEAD_DIM]], axis=0).astype(BF16)]
        for j in range(2)
    ]
    v_sw = pltpu.roll(v_all, HEAD_DIM, axis=1)
    v_var = [
        [jnp.where(lo_lanes, v_all, 0.0).astype(BF16), jnp.where(lo_lanes, 0.0, v_sw).astype(BF16)],
        [jnp.where(lo_lanes, v_sw, 0.0).astype(BF16), jnp.where(lo_lanes, 0.0, v_all).astype(BF16)],
    ]

    qn = [
        _head_pair_norm(q_ref[:, c * LANES:(c + 1) * LANES], qw_ref[...], blockdiag).astype(BF16)
        for c in range(ATTN_WIDTH // LANES)
    ]

    rows = 4 * WINDOW
    qi = _iota((rows, 2 * WINDOW), 0) % WINDOW
    kj = _iota((rows, 2 * WINDOW), 1)
    band = (kj > qi) & (kj <= qi + WINDOW)
    first_min = jnp.where(i == 0, WINDOW, 0)
    band_first = band & (kj >= first_min)

    for s in range(nsub):
        mask = band_first if s == 0 else band
        for j in range(2):
            qs = jnp.concatenate([qn[4 * j + c][s * WINDOW:(s + 1) * WINDOW] for c in range(4)], axis=0)
            out = None
            for par in range(2):
                sink = jnp.concatenate(
                    [jnp.full((WINDOW, 1), sinks_ref[8 * j + 2 * c + par], F32) for c in range(4)], axis=0)
                sc = _dot(qs, kt_var[j][par][:, s * WINDOW:s * WINDOW + 2 * WINDOW])
                sc = jnp.where(mask, sc, NEG_BIG)
                m = jnp.maximum(jnp.max(sc, axis=-1, keepdims=True), sink)
                p = jnp.exp(sc - m)
                den = jnp.sum(p, axis=-1, keepdims=True) + jnp.exp(sink - m)
                pv = _dot(p.astype(BF16), v_var[j][par][s * WINDOW:s * WINDOW + 2 * WINDOW])
                pv = pv * (1.0 / den)
                out = pv if out is None else out + pv
            for c in range(4):
                acc_ref[s * WINDOW:(s + 1) * WINDOW, (4 * j + c) * LANES:(4 * j + c + 1) * LANES] = (
                    out[c * WINDOW:(c + 1) * WINDOW])

    a = acc_ref[...]
    ms = jnp.mean(a * a, axis=-1, keepdims=True)
    o_ref[...] = (a * lax.rsqrt(ms + EPS) * onw_ref[...]).astype(BF16)


def attention(proj, sinks, qw2, kw2, onw, batch, seq):
    t = proj.shape[0]
    tq = min(512, seq)
    nq = seq // tq
    sub = tq // WINDOW
    kv_col = COL_K // (2 * KV_WIDTH)
    grid_spec = pltpu.PrefetchScalarGridSpec(
        num_scalar_prefetch=1,
        grid=(batch, nq),
        in_specs=[
            pl.BlockSpec((tq, ATTN_WIDTH), lambda b, i, s: (b * nq + i, COL_Q // ATTN_WIDTH)),
            pl.BlockSpec((tq, 2 * KV_WIDTH), lambda b, i, s: (b * nq + i, kv_col)),
            pl.BlockSpec((WINDOW, 2 * KV_WIDTH),
                         lambda b, i, s: (b * nq * sub + jnp.maximum(i * sub - 1, 0), kv_col)),
            pl.BlockSpec((1, LANES), lambda b, i, s: (0, 0)),
            pl.BlockSpec((1, LANES), lambda b, i, s: (0, 0)),
            pl.BlockSpec((1, ATTN_WIDTH), lambda b, i, s: (0, 0)),
        ],
        out_specs=pl.BlockSpec((tq, ATTN_WIDTH), lambda b, i, s: (b * nq + i, 0)),
        scratch_shapes=[pltpu.VMEM((tq, ATTN_WIDTH), F32)],
    )
    return pl.pallas_call(
        _attn_body,
        grid_spec=grid_spec,
        out_shape=jax.ShapeDtypeStruct((t, ATTN_WIDTH), BF16),
        compiler_params=_params("arbitrary", "arbitrary"),
        name="attention",
    )(sinks, proj, proj, proj, qw2, kw2, onw)


def _ssd_body(z_ref, xs_ref, bc_ref, dt_ref, cwx_ref, cwb_ref, cbx_ref, cbb_ref, dtb_ref, alog_ref,
              dsk_ref, nw_ref, o_ref, xpx_ref, xpb_ref, st_ref, y_ref):
    i = pl.program_id(1)
    ts = xs_ref.shape[0]
    nch = ts // SSM_CHUNK
    gw = SSM_WIDTH // SSM_GROUPS
    pad = SUBLANES

    @pl.when(i == 0)
    def _():
        xpx_ref[0:pad, :] = jnp.zeros((pad, SSM_WIDTH), F32)
        xpb_ref[0:pad, :] = jnp.zeros((pad, 2 * BC_WIDTH), F32)
        st_ref[...] = jnp.zeros(st_ref.shape, F32)

    xpx_ref[pad:pad + ts, :] = xs_ref[...]
    xpb_ref[pad:pad + ts, :] = bc_ref[...]

    def conv_silu(xp_ref, w_ref, b_ref):
        acc = b_ref[...] + w_ref[SSM_CONV - 1:SSM_CONV, :] * xp_ref[pad:pad + ts, :]
        for k in range(SSM_CONV - 1):
            off = pad - (SSM_CONV - 1) + k
            acc = acc + w_ref[k:k + 1, :] * xp_ref[off:off + ts, :]
        return acc * _sigmoid(acc)

    xc = conv_silu(xpx_ref, cwx_ref, cbx_ref)
    bcc = conv_silu(xpb_ref, cwb_ref, cbb_ref)
    xpx_ref[0:pad, :] = xs_ref[ts - pad:ts, :]
    xpb_ref[0:pad, :] = bc_ref[ts - pad:ts, :]

    dtr = dt_ref[...] + dtb_ref[...]
    dt = jnp.maximum(dtr, 0.0) + jnp.log(1.0 + jnp.exp(-jnp.abs(dtr)))
    a = -jnp.exp(alog_ref[...])
    da = dt * a

    ri = _iota((ts, ts), 0)
    ci = _iota((ts, ts), 1)
    same = (ri // SSM_CHUNK) == (ci // SSM_CHUNK)
    tri_blk = jnp.where(same & (ci <= ri), 1.0, 0.0).astype(BF16)
    all_blk = jnp.where(same, 1.0, 0.0).astype(BF16)
    da3 = _split3(da)
    cs = _dot(tri_blk, da3[0]) + _dot(tri_blk, da3[1]) + _dot(tri_blk, da3[2])
    cl = _dot(all_blk, da3[0]) + _dot(all_blk, da3[1]) + _dot(all_blk, da3[2])
    ecs = jnp.exp(cs)
    dte = jnp.exp(cl - cs)

    expand = jnp.where(_iota((LANES, SSM_WIDTH), 1) // HEAD_DIM == _iota((LANES, SSM_WIDTH), 0),
                       1.0, 0.0).astype(BF16)
    dt_e = _dot_exact_lhs(_split2(dt), expand)
    dtd_e = _dot_exact_lhs(_split2(dt * dte), expand)
    ecs_e = _dot_exact_lhs(_split2(ecs), expand)
    xf = xc * dt_e
    xd = (xc * dtd_e).astype(BF16)

    lo_lanes = _iota((1, LANES), 1) < HEAD_DIM
    li = _iota((SSM_CHUNK, SSM_CHUNK), 0)
    lj = _iota((SSM_CHUNK, SSM_CHUNK), 1)
    tril = lj <= li
    upper = jnp.where(li <= lj, 1.0, 0.0).astype(BF16)

    for c in range(nch):
        r0 = c * SSM_CHUNK
        da_t = da[r0:r0 + SSM_CHUNK].T
        cs_t = _dot_exact_lhs(_split3(da_t), upper)
        cs_c = cs[r0:r0 + SSM_CHUNK]
        for g in range(SSM_GROUPS):
            b_c = bcc[r0:r0 + SSM_CHUNK, g * SSM_STATE:(g + 1) * SSM_STATE]
            c_c = bcc[r0:r0 + SSM_CHUNK, BC_WIDTH + g * SSM_STATE:BC_WIDTH + (g + 1) * SSM_STATE].astype(BF16)
            b_t = b_c.T.astype(BF16)
            cb = _dot(c_c, b_t)
            st = st_ref[g]
            y_off = _dot(c_c, st.astype(BF16)) * ecs_e[r0:r0 + SSM_CHUNK, g * gw:(g + 1) * gw]
            s_new = _dot(b_t, xd[r0:r0 + SSM_CHUNK, g * gw:(g + 1) * gw])
            parts = []
            for p in range(4):
                h0 = 8 * g + 2 * p
                ms = []
                for h in (h0, h0 + 1):
                    diff = cs_c[:, h:h + 1] - cs_t[h:h + 1, :]
                    ms.append((cb * jnp.exp(jnp.where(tril, diff, NEG_BIG))).astype(BF16))
                m2 = jnp.concatenate(ms, axis=1)
                x2 = xf[r0:r0 + SSM_CHUNK, h0 * HEAD_DIM:(h0 + 2) * HEAD_DIM]
                xbd = jnp.concatenate([jnp.where(lo_lanes, x2, 0.0), jnp.where(lo_lanes, 0.0, x2)],
                                      axis=0).astype(BF16)
                parts.append(_dot(m2, xbd))
            y_ref[r0:r0 + SSM_CHUNK, g * gw:(g + 1) * gw] = jnp.concatenate(parts, axis=1) + y_off
            dec = ecs_e[r0 + SSM_CHUNK - 1:r0 + SSM_CHUNK, g * gw:(g + 1) * gw]
            st_ref[g] = st * dec + s_new

    y = y_ref[...] + xc * dsk_ref[...]
    z = z_ref[...]
    gated = y * (z * _sigmoid(z))
    outs = []
    for g in range(SSM_GROUPS):
        gg = gated[:, g * gw:(g + 1) * gw]
        ms = jnp.mean(gg * gg, axis=-1, keepdims=True)
        outs.append(gg * lax.rsqrt(ms + EPS) * nw_ref[:, g * gw:(g + 1) * gw])
    o_ref[...] = jnp.concatenate(outs, axis=1).astype(BF16)


def ssd(proj, cwx, cwb, cbx, cbb, dtb, alog, dsk, nw, batch, seq):
    t = proj.shape[0]
    ts = min(512, seq)
    ns = seq // ts
    row = lambda b, i: b * ns + i
    const = lambda b, i: (0, 0)
    return pl.pallas_call(
        _ssd_body,
        grid=(batch, ns),
        in_specs=[
            pl.BlockSpec((ts, SSM_WIDTH), lambda b, i: (row(b, i), COL_Z // SSM_WIDTH)),
            pl.BlockSpec((ts, SSM_WIDTH), lambda b, i: (row(b, i), COL_XBC // SSM_WIDTH)),
            pl.BlockSpec((ts, 2 * BC_WIDTH), lambda b, i: (row(b, i), (COL_XBC + SSM_WIDTH) // (2 * BC_WIDTH))),
            pl.BlockSpec((ts, LANES), lambda b, i: (row(b, i), COL_DT // LANES)),
            pl.BlockSpec((SSM_CONV, SSM_WIDTH), const),
            pl.BlockSpec((SSM_CONV, 2 * BC_WIDTH), const),
            pl.BlockSpec((1, SSM_WIDTH), const),
            pl.BlockSpec((1, 2 * BC_WIDTH), const),
            pl.BlockSpec((1, LANES), const),
            pl.BlockSpec((1, LANES), const),
            pl.BlockSpec((1, SSM_WIDTH), const),
            pl.BlockSpec((1, SSM_WIDTH), const),
        ],
        out_specs=pl.BlockSpec((ts, SSM_WIDTH), lambda b, i: (row(b, i), 0)),
        out_shape=jax.ShapeDtypeStruct((t, SSM_WIDTH), BF16),
        scratch_shapes=[
            pltpu.VMEM((ts + SUBLANES, SSM_WIDTH), F32),
            pltpu.VMEM((ts + SUBLANES, 2 * BC_WIDTH), F32),
            pltpu.VMEM((SSM_GROUPS, SSM_STATE, SSM_WIDTH // SSM_GROUPS), F32),
            pltpu.VMEM((ts, SSM_WIDTH), F32),
        ],
        compiler_params=_params("arbitrary", "arbitrary"),
        name="ssd",
    )(proj, proj, proj, proj, cwx, cwb, cbx, cbb, dtb, alog, dsk, nw)


def _out_body(x_ref, a_ref, s_ref, wo_ref, fg_ref, wr_ref, br_ref, h_ref, hf_ref, idx_ref, gate_ref):
    tm = x_ref.shape[0]
    h = x_ref[...] + _dot(a_ref[...], wo_ref[0:ATTN_WIDTH, :]) + _dot(s_ref[...], wo_ref[ATTN_WIDTH:, :])
    h_ref[...] = h
    ms = jnp.mean(h * h, axis=-1, keepdims=True)
    hf = h * lax.rsqrt(ms + EPS) * fg_ref[...]
    hi, lo = _split2(hf)
    hi32 = hi.astype(F32)
    hf_ref[...] = (pltpu.bitcast(hi32[:, D_MODEL // 2:], jnp.uint32)
                   | (pltpu.bitcast(hi32[:, :D_MODEL // 2], jnp.uint32) >> 16))

    whi, wlo = _split2(wr_ref[...])
    hw = _dot(hi, jnp.concatenate([whi, wlo], axis=1))
    logits = hw[:, :LANES] + hw[:, LANES:] + _dot(lo, whi) + br_ref[...]
    lane = _iota((tm, LANES), 1)
    vals, idxs = [], []
    for _ in range(TOP_K):
        m = jnp.max(logits, axis=-1, keepdims=True)
        am = jnp.min(jnp.where(logits == m, lane, LANES), axis=-1, keepdims=True)
        vals.append(m)
        idxs.append(am)
        logits = jnp.where(lane == am, NEG_BIG * 2.0, logits)
    es = [jnp.exp(v - vals[0]) for v in vals]
    tot = es[0] + es[1] + es[2] + es[3]
    idx_out = jnp.zeros((tm, LANES), jnp.int32)
    gate_out = jnp.zeros((tm, LANES), F32)
    for k in range(TOP_K):
        idx_out = jnp.where(lane == k, idxs[k], idx_out)
        gate_out = jnp.where(lane == k, es[k] / tot, gate_out)
    idx_ref[...] = idx_out
    gate_ref[...] = gate_out


def out_proj_router(x2d, attn, ssm, wo, fg, wr, br):
    t, d = x2d.shape
    tm = min(512, t)
    const = lambda i: (0, 0)
    once = pl.Buffered(1)
    return pl.pallas_call(
        _out_body,
        grid=(t // tm,),
        in_specs=[
            pl.BlockSpec((tm, d), lambda i: (i, 0)),
            pl.BlockSpec((tm, ATTN_WIDTH), lambda i: (i, 0)),
            pl.BlockSpec((tm, SSM_WIDTH), lambda i: (i, 0)),
            pl.BlockSpec((d, d), const, pipeline_mode=once),
            pl.BlockSpec((1, d), const, pipeline_mode=once),
            pl.BlockSpec((d, LANES), const, pipeline_mode=once),
            pl.BlockSpec((1, LANES), const, pipeline_mode=once),
        ],
        out_specs=[
            pl.BlockSpec((tm, d), lambda i: (i, 0)),
            pl.BlockSpec((tm, d // 2), lambda i: (i, 0)),
            pl.BlockSpec((tm, LANES), lambda i: (i, 0)),
            pl.BlockSpec((tm, LANES), lambda i: (i, 0)),
        ],
        out_shape=[
            jax.ShapeDtypeStruct((t, d), F32),
            jax.ShapeDtypeStruct((t, d // 2), jnp.uint32),
            jax.ShapeDtypeStruct((t, LANES), jnp.int32),
            jax.ShapeDtypeStruct((t, LANES), F32),
        ],
        compiler_params=_params("arbitrary"),
        name="out_proj_router",
    )(x2d, attn, ssm, wo, fg, wr, br)


MOE_ROWS = 512
FF_TILE = 512
N_FF_TILES = D_FF // FF_TILE


def _expert_body(be_ref, nv_ref, tok_ref, tokn_ref, dst_ref, dstp_ref, hf_ref, wg_ref, wu_ref, wd_ref, bg_ref, bu_ref,
                 bd_ref, yk_ref, xrow_ref, x_ref, h_ref, y_ref, gu_ref, dn_ref, row_sem, out_sem, gu_sem, dn_sem):
    b = pl.program_id(0)
    nv = nv_ref[0]
    tm = h_ref.shape[0]
    half = D_MODEL // 2

    def row_copy(table_ref, r):
        return pltpu.make_async_copy(hf_ref.at[pl.ds(table_ref[0, 0, r], 1)], xrow_ref.at[pl.ds(r, 1)], row_sem)

    def rows_wait():
        pltpu.make_async_copy(hf_ref.at[pl.ds(0, tm)], xrow_ref, row_sem).wait()

    def unpack_rows(slot):
        w = xrow_ref[...]
        x_ref[slot, :, 0:half] = pltpu.bitcast(w << 16, F32).astype(BF16)
        x_ref[slot, :, half:] = pltpu.bitcast(w & jnp.uint32(0xFFFF0000), F32).astype(BF16)

    def out_copy(table_ref, r):
        return pltpu.make_async_copy(y_ref.at[pl.ds(r, 1)], yk_ref.at[pl.ds(table_ref[0, 0, r], 1)], out_sem)

    def out_wait():
        pltpu.make_async_copy(y_ref, yk_ref.at[pl.ds(0, tm)], out_sem).wait()

    def gu_copies(e, j, slot):
        cols = pl.ds(j * FF_TILE, FF_TILE)
        return (pltpu.make_async_copy(wg_ref.at[e, :, cols], gu_ref.at[slot, 0], gu_sem.at[slot]),
                pltpu.make_async_copy(wu_ref.at[e, :, cols], gu_ref.at[slot, 1], gu_sem.at[slot]))

    def dn_copy(e, j, slot):
        return pltpu.make_async_copy(wd_ref.at[e, :, pl.ds(j * FF_TILE, FF_TILE)], dn_ref.at[slot], dn_sem.at[slot])

    per_batch = tm // N_FF_TILES

    @pl.when(b < nv)
    def _():
        e = be_ref[b]
        slot = b % 2

        @pl.when(b == 0)
        def _():
            def issue(r, carry):
                row_copy(tok_ref, r).start()
                return carry
            lax.fori_loop(0, tm, issue, 0, unroll=8)
            for cp in gu_copies(e, 0, 0):
                cp.start()
            y_ref[...] = jnp.zeros(y_ref.shape, F32)
            rows_wait()
            unpack_rows(0)

        x = x_ref[slot]
        for j in range(N_FF_TILES):
            for cp in gu_copies(e, j, j % 2):
                cp.wait()
            if j + 1 < N_FF_TILES:
                for cp in gu_copies(e, j + 1, (j + 1) % 2):
                    cp.start()
            else:
                dn_copy(e, 0, 0).start()
            cols = slice(j * FF_TILE, (j + 1) * FF_TILE)
            g_acc = _dot(x, gu_ref[j % 2, 0])
            u_acc = _dot(x, gu_ref[j % 2, 1])
            for r in range(j * per_batch, (j + 1) * per_batch):
                row_copy(tokn_ref, r).start()
                out_copy(dstp_ref, r).start()
            gl = jnp.minimum(g_acc + bg_ref[0, :, cols], SWIGLU_LIMIT)
            li = jnp.clip(u_acc + bu_ref[0, :, cols], -SWIGLU_LIMIT, SWIGLU_LIMIT)
            h_ref[:, cols] = (gl * _sigmoid(SWIGLU_ALPHA * gl) * (li + 1.0)).astype(BF16)

        out_wait()
        rows_wait()

        h = h_ref[...]
        for j in range(N_FF_TILES):
            dn_copy(e, j, j % 2).wait()
            if j + 1 < N_FF_TILES:
                dn_copy(e, j + 1, (j + 1) % 2).start()
            else:
                @pl.when(b + 1 < nv)
                def _():
                    for cp in gu_copies(be_ref[b + 1], 0, 0):
                        cp.start()
            if j == 0:
                unpack_rows(1 - slot)
            cols = slice(j * FF_TILE, (j + 1) * FF_TILE)
            y_ref[:, cols] = _dot(h, dn_ref[j % 2]) + bd_ref[0, :, cols]

        @pl.when(b + 1 == nv)
        def _():
            def issue(r, carry):
                out_copy(dst_ref, r).start()
                return carry
            lax.fori_loop(0, tm, issue, 0, unroll=8)
            out_wait()


def experts(block_expert, n_valid, slot_tok, slot_dst, hf_packed, wg, bg, wu, bu, wd, bd, n_out_rows):
    n_blocks = block_expert.shape[0]
    tm = MOE_ROWS
    d = D_MODEL

    def blk(b, nv):
        return jnp.minimum(b, nv[0] - 1)

    any_spec = pl.BlockSpec(memory_space=pl.ANY)
    grid_spec = pltpu.PrefetchScalarGridSpec(
        num_scalar_prefetch=2,
        grid=(n_blocks,),
        in_specs=[
            pl.BlockSpec((1, 1, tm), lambda b, be, nv: (blk(b, nv), 0, 0), memory_space=pltpu.SMEM),
            pl.BlockSpec((1, 1, tm), lambda b, be, nv: (blk(b + 1, nv), 0, 0), memory_space=pltpu.SMEM),
            pl.BlockSpec((1, 1, tm), lambda b, be, nv: (blk(b, nv) + 1, 0, 0), memory_space=pltpu.SMEM),
            pl.BlockSpec((1, 1, tm), lambda b, be, nv: (blk(b, nv), 0, 0), memory_space=pltpu.SMEM),
            any_spec, any_spec, any_spec, any_spec,
            pl.BlockSpec((1, 1, D_FF), lambda b, be, nv: (be[blk(b, nv)], 0, 0)),
            pl.BlockSpec((1, 1, D_FF), lambda b, be, nv: (be[blk(b, nv)], 0, 0)),
            pl.BlockSpec((1, 1, d), lambda b, be, nv: (be[blk(b, nv)], 0, 0)),
        ],
        out_specs=any_spec,
        scratch_shapes=[
            pltpu.VMEM((tm, d // 2), jnp.uint32),
            pltpu.VMEM((2, tm, d), BF16),
            pltpu.VMEM((tm, D_FF), BF16),
            pltpu.VMEM((tm, d), F32),
            pltpu.VMEM((2, 2, d, FF_TILE), BF16),
            pltpu.VMEM((2, D_FF, FF_TILE), BF16),
            pltpu.SemaphoreType.DMA,
            pltpu.SemaphoreType.DMA,
            pltpu.SemaphoreType.DMA((2,)),
            pltpu.SemaphoreType.DMA((2,)),
        ],
    )
    return pl.pallas_call(
        _expert_body,
        grid_spec=grid_spec,
        out_shape=jax.ShapeDtypeStruct((n_out_rows, d), F32),
        compiler_params=_params("arbitrary"),
        name="experts",
    )(block_expert, n_valid, slot_tok, slot_tok, slot_dst, slot_dst, hf_packed, wg, wu, wd, bg, bu, bd)


COMBINE_ROWS = 256


def _combine_body(h_ref, gate_ref, yk_ref, o_ref):
    tm = h_ref.shape[0]
    gates = gate_ref[...]
    acc = h_ref[...]
    for k in range(TOP_K):
        acc = acc + gates[:, k:k + 1] * yk_ref[k * tm:(k + 1) * tm, :]
    o_ref[...] = acc


def combine(h, gates, yk):
    t, d = h.shape
    tm = min(COMBINE_ROWS, t)
    return pl.pallas_call(
        _combine_body,
        grid=(t // tm,),
        in_specs=[
            pl.BlockSpec((tm, d), lambda i: (i, 0)),
            pl.BlockSpec((tm, LANES), lambda i: (i, 0)),
            pl.BlockSpec((TOP_K * tm, d), lambda i: (i, 0)),
        ],
        out_specs=pl.BlockSpec((tm, d), lambda i: (i, 0)),
        out_shape=jax.ShapeDtypeStruct((t, d), F32),
        compiler_params=_params("arbitrary"),
        name="combine",
    )(h, gates, yk)


def routing_tables(top_idx, t):
    n_assign = t * TOP_K
    tm_c = min(COMBINE_ROWS, t)
    e_flat = top_idx.reshape(-1).astype(jnp.int32)
    _, order = lax.sort((e_flat, jnp.arange(n_assign, dtype=jnp.int32)), num_keys=1, is_stable=True)
    experts_iota = jnp.arange(N_EXPERTS, dtype=jnp.int32)
    counts = jnp.sum((e_flat[None, :] == experts_iota[:, None]).astype(jnp.int32), axis=1)
    starts = jnp.cumsum(counts) - counts
    padded = (counts + MOE_ROWS - 1) // MOE_ROWS * MOE_ROWS
    pends = jnp.cumsum(padded)
    pstarts = pends - padded
    n_blocks = -(-n_assign // MOE_ROWS) + N_EXPERTS
    block_start = jnp.arange(n_blocks, dtype=jnp.int32) * MOE_ROWS
    block_expert = jnp.minimum(
        jnp.sum((pends[None, :] <= block_start[:, None]).astype(jnp.int32), axis=1), N_EXPERTS - 1)
    n_valid = (pends[-1:] // MOE_ROWS).astype(jnp.int32)

    row = jnp.arange(MOE_ROWS, dtype=jnp.int32)[None, :]
    e_b = block_expert[:, None]
    local = jnp.arange(n_blocks, dtype=jnp.int32)[:, None] * MOE_ROWS + row - pstarts[e_b]
    used = (local >= 0) & (local < counts[e_b])
    asg = order[jnp.clip(starts[e_b] + local, 0, n_assign - 1)]
    tok = asg // TOP_K
    dst = (tok // tm_c) * (TOP_K * tm_c) + (asg % TOP_K) * tm_c + tok % tm_c
    slot_tok = jnp.where(used, tok, 0).reshape(n_blocks, 1, MOE_ROWS)
    slot_dst = jnp.concatenate([n_assign + row, jnp.where(used, dst, n_assign + row)], axis=0)
    return block_expert, n_valid, slot_tok, slot_dst.reshape(n_blocks + 1, 1, MOE_ROWS)


def _pad_lanes(v, width, value=0.0):
    return jnp.pad(v, ((0, 0), (0, width - v.shape[1])), constant_values=value)


def hybrid_layer(x, attn_norm_w, w_in, q_norm_w, k_norm_w, attn_sinks, conv_w, conv_b, dt_bias, a_log, d_skip,
                 ssm_norm_w, attn_out_norm_w, w_out, ffn_norm_w, w_router, b_router, w_gate, b_gate, w_up, b_up,
                 w_down, b_down):
    batch, seq, d = x.shape
    t = batch * seq
    x2d = x.reshape(t, d)

    q_w, k_w, v_w, z_w, xbc_w, dt_w = jnp.split(
        w_in, [ATTN_WIDTH, ATTN_WIDTH + KV_WIDTH, ATTN_WIDTH + 2 * KV_WIDTH,
               ATTN_WIDTH + 2 * KV_WIDTH + SSM_WIDTH, ATTN_WIDTH + 2 * KV_WIDTH + SSM_WIDTH + CONV_WIDTH], axis=1)
    w_p = jnp.concatenate([q_w, z_w, xbc_w, k_w, v_w, dt_w], axis=1)
    w_p = _pad_lanes(w_p, PROJ_WIDTH).astype(BF16)
    proj = in_proj(x2d, attn_norm_w.reshape(1, d), w_p)

    qw2 = (jnp.tile(q_norm_w, 2) * (HEAD_DIM ** -0.5)).reshape(1, LANES)
    kw2 = jnp.tile(k_norm_w, 2).reshape(1, LANES)
    attn = attention(proj, attn_sinks, qw2, kw2, attn_out_norm_w.reshape(1, ATTN_WIDTH), batch, seq)

    ssm = ssd(
        proj,
        conv_w[:, :SSM_WIDTH], conv_w[:, SSM_WIDTH:],
        conv_b[:SSM_WIDTH].reshape(1, -1), conv_b[SSM_WIDTH:].reshape(1, -1),
        _pad_lanes(dt_bias.reshape(1, -1), LANES), _pad_lanes(a_log.reshape(1, -1), LANES),
        jnp.repeat(d_skip, HEAD_DIM).reshape(1, SSM_WIDTH), ssm_norm_w.reshape(1, SSM_WIDTH),
        batch, seq)

    h, hf_packed, idx_pad, gate_pad = out_proj_router(
        x2d, attn, ssm, w_out.astype(BF16), ffn_norm_w.reshape(1, d),
        _pad_lanes(w_router, LANES), _pad_lanes(b_router.reshape(1, -1), LANES, NEG_BIG))

    block_expert, n_valid, slot_tok, slot_dst = routing_tables(idx_pad[:, :TOP_K], t)
    yk = experts(
        block_expert, n_valid, slot_tok, slot_dst, hf_packed,
        w_gate.astype(BF16), b_gate.reshape(N_EXPERTS, 1, D_FF),
        w_up.astype(BF16), b_up.reshape(N_EXPERTS, 1, D_FF),
        w_down.astype(BF16), b_down.reshape(N_EXPERTS, 1, D_MODEL),
        t * TOP_K + MOE_ROWS)

    out = combine(h, gate_pad, yk)
    return out.reshape(batch, seq, d)


def kernel(x, attn_norm_w, w_in, q_norm_w, k_norm_w, attn_sinks, conv_w, conv_b, dt_bias, a_log, d_skip, ssm_norm_w,
           attn_out_norm_w, w_out, ffn_norm_w, w_router, b_router, w_gate, b_gate, w_up, b_up, w_down, b_down):
    for i in range(attn_norm_w.shape[0]):
        x = hybrid_layer(x, attn_norm_w[i], w_in[i], q_norm_w[i], k_norm_w[i], attn_sinks[i], conv_w[i], conv_b[i],
                         dt_bias[i], a_log[i], d_skip[i], ssm_norm_w[i], attn_out_norm_w[i], w_out[i], ffn_norm_w[i],
                         w_router[i], b_router[i], w_gate[i], b_gate[i], w_up[i], b_up[i], w_down[i], b_down[i])
    return x
```

```python
import functools

import jax
import jax.numpy as jnp
from jax import lax
from jax.experimental import pallas as pl
from jax.experimental.pallas import tpu as pltpu

F32 = jnp.float32
BF16 = jnp.bfloat16

D_MODEL = 2048
HEAD_DIM = 64
ATTN_WIDTH = 1024
KV_WIDTH = 128
WINDOW = 128
SSM_WIDTH = 1024
SSM_HEADS = 16
SSM_GROUPS = 2
SSM_STATE = 128
SSM_CONV = 4
SSM_CHUNK = 128
BC_WIDTH = SSM_GROUPS * SSM_STATE
CONV_WIDTH = SSM_WIDTH + 2 * BC_WIDTH
N_EXPERTS = 32
TOP_K = 4
D_FF = 2048
SWIGLU_LIMIT = 7.0
SWIGLU_ALPHA = 1.702
EPS = 1e-6

LANES = 128
SUBLANES = 8
VMEM_LIMIT = 56 * 1024 * 1024

PROJ_WIDTH = 4096
COL_Q = 0
COL_Z = ATTN_WIDTH
COL_XBC = COL_Z + SSM_WIDTH
COL_K = COL_XBC + CONV_WIDTH
COL_V = COL_K + KV_WIDTH
COL_DT = COL_V + KV_WIDTH
NEG_BIG = -1e30


def _params(*sem):
    return pltpu.CompilerParams(dimension_semantics=sem, vmem_limit_bytes=VMEM_LIMIT)


def _split2(v):
    hi = v.astype(BF16)
    lo = (v - hi.astype(F32)).astype(BF16)
    return hi, lo


def _split3(v):
    hi = v.astype(BF16)
    r = v - hi.astype(F32)
    mid = r.astype(BF16)
    lo = (r - mid.astype(F32)).astype(BF16)
    return hi, mid, lo


def _dot(a, b):
    return jnp.dot(a, b, preferred_element_type=F32)


def _dot_exact_lhs(parts, m):
    acc = _dot(parts[0], m)
    for p in parts[1:]:
        acc = acc + _dot(p, m)
    return acc


def _sigmoid(v):
    return 1.0 / (1.0 + jnp.exp(-v))


def _iota(shape, dim):
    return lax.broadcasted_iota(jnp.int32, shape, dim)


def _in_proj_body(x0_ref, xn_ref, g_ref, w_ref, o_ref, hn_ref):
    i = pl.program_id(0)
    j = pl.program_id(1)
    part = xn_ref.shape[0] // pl.num_programs(1)

    def norm(x):
        ms = jnp.mean(x * x, axis=-1, keepdims=True)
        return (x * lax.rsqrt(ms + EPS) * g_ref[...]).astype(BF16)

    @pl.when((i == 0) & (j == 0))
    def _():
        hn_ref[0] = norm(x0_ref[...])

    slot = i % 2
    rows = pl.ds(pl.multiple_of(j * part, part), part)
    hn_ref[1 - slot, rows, :] = norm(xn_ref[rows, :])
    o_ref[...] = _dot(hn_ref[slot], w_ref[...])


def in_proj(x2d, g, w_p):
    t, d = x2d.shape
    n = w_p.shape[1]
    tm = min(1024, t)
    tn = 1024
    ni = t // tm
    return pl.pallas_call(
        _in_proj_body,
        grid=(ni, n // tn),
        in_specs=[
            pl.BlockSpec((tm, d), lambda i, j: (0, 0), pipeline_mode=pl.Buffered(1)),
            pl.BlockSpec((tm, d), lambda i, j: (jnp.minimum(i + 1, ni - 1), 0)),
            pl.BlockSpec((1, d), lambda i, j: (0, 0)),
            pl.BlockSpec((d, tn), lambda i, j: (0, j)),
        ],
        out_specs=pl.BlockSpec((tm, tn), lambda i, j: (i, j)),
        out_shape=jax.ShapeDtypeStruct((t, n), F32),
        scratch_shapes=[pltpu.VMEM((2, tm, d), BF16)],
        compiler_params=_params("arbitrary", "arbitrary"),
        name="in_proj",
    )(x2d, x2d, g, w_p)


def _head_pair_norm(v, w, blockdiag):
    ss = _dot_exact_lhs(_split2(v * v), blockdiag)
    return v * lax.rsqrt(ss * (1.0 / HEAD_DIM) + EPS) * w


def _attn_body(sinks_ref, q_ref, kvc_ref, kvp_ref, qw_ref, kw_ref, onw_ref, o_ref, acc_ref):
    i = pl.program_id(1)
    tq = q_ref.shape[0]
    nsub = tq // WINDOW
    nk = tq + WINDOW
    lo_lanes = _iota((1, LANES), 1) < HEAD_DIM
    blockdiag = jnp.where(
        _iota((LANES, LANES), 0) // HEAD_DIM == _iota((LANES, LANES), 1) // HEAD_DIM, 1.0, 0.0
    ).astype(BF16)

    kv_all = jnp.concatenate([kvp_ref[...], kvc_ref[...]], axis=0)
    k_all = _head_pair_norm(kv_all[:, :KV_WIDTH], kw_ref[...], blockdiag)
    v_all = kv_all[:, KV_WIDTH:]
    kt = k_all.T
    zero_half = jnp.zeros((HEAD_DIM, nk), F32)
    kt_var = [
        [jnp.concatenate([kt[j * HEAD_DIM:(j + 1) * HEAD_DIM], zero_half], axis=0).astype(BF16),
         jnp.concatenate([zero_half, kt[j * HEAD_DIM:(j + 1) * HEAD_DIM]], axis=0).astype(BF16)]
        for j in range(2)
    ]
    v_sw = pltpu.roll(v_all, HEAD_DIM, axis=1)
    v_var = [
        [jnp.where(lo_lanes, v_all, 0.0).astype(BF16), jnp.where(lo_lanes, 0.0, v_sw).astype(BF16)],
        [jnp.where(lo_lanes, v_sw, 0.0).astype(BF16), jnp.where(lo_lanes, 0.0, v_all).astype(BF16)],
    ]

    qn = [
        _head_pair_norm(q_ref[:, c * LANES:(c + 1) * LANES], qw_ref[...], blockdiag).astype(BF16)
        for c in range(ATTN_WIDTH // LANES)
    ]

    rows = 4 * WINDOW
    qi = _iota((rows, 2 * WINDOW), 0) % WINDOW
    kj = _iota((rows, 2 * WINDOW), 1)
    band = (kj > qi) & (kj <= qi + WINDOW)
    first_min = jnp.where(i == 0, WINDOW, 0)
    band_first = band & (kj >= first_min)

    for s in range(nsub):
        mask = band_first if s == 0 else band
        for j in range(2):
            qs = jnp.concatenate([qn[4 * j + c][s * WINDOW:(s + 1) * WINDOW] for c in range(4)], axis=0)
            out = None
            for par in range(2):
                sink = jnp.concatenate(
                    [jnp.full((WINDOW, 1), sinks_ref[8 * j + 2 * c + par], F32) for c in range(4)], axis=0)
                sc = _dot(qs, kt_var[j][par][:, s * WINDOW:s * WINDOW + 2 * WINDOW])
                sc = jnp.where(mask, sc, NEG_BIG)
                m = jnp.maximum(jnp.max(sc, axis=-1, keepdims=True), sink)
                p = jnp.exp(sc - m)
                den = jnp.sum(p, axis=-1, keepdims=True) + jnp.exp(sink - m)
                pv = _dot(p.astype(BF16), v_var[j][par][s * WINDOW:s * WINDOW + 2 * WINDOW])
                pv = pv * (1.0 / den)
                out = pv if out is None else out + pv
            for c in range(4):
                acc_ref[s * WINDOW:(s + 1) * WINDOW, (4 * j + c) * LANES:(4 * j + c + 1) * LANES] = (
                    out[c * WINDOW:(c + 1) * WINDOW])

    a = acc_ref[...]
    ms = jnp.mean(a * a, axis=-1, keepdims=True)
    o_ref[...] = (a * lax.rsqrt(ms + EPS) * onw_ref[...]).astype(BF16)


def attention(proj, sinks, qw2, kw2, onw, batch, seq):
    t = proj.shape[0]
    tq = min(512, seq)
    nq = seq // tq
    sub = tq // WINDOW
    kv_col = COL_K // (2 * KV_WIDTH)
    grid_spec = pltpu.PrefetchScalarGridSpec(
        num_scalar_prefetch=1,
        grid=(batch, nq),
        in_specs=[
            pl.BlockSpec((tq, ATTN_WIDTH), lambda b, i, s: (b * nq + i, COL_Q // ATTN_WIDTH)),
            pl.BlockSpec((tq, 2 * KV_WIDTH), lambda b, i, s: (b * nq + i, kv_col)),
            pl.BlockSpec((WINDOW, 2 * KV_WIDTH),
                         lambda b, i, s: (b * nq * sub + jnp.maximum(i * sub - 1, 0), kv_col)),
            pl.BlockSpec((1, LANES), lambda b, i, s: (0, 0)),
            pl.BlockSpec((1, LANES), lambda b, i, s: (0, 0)),
            pl.BlockSpec((1, ATTN_WIDTH), lambda b, i, s: (0, 0)),
        ],
        out_specs=pl.BlockSpec((tq, ATTN_WIDTH), lambda b, i, s: (b * nq + i, 0)),
        scratch_shapes=[pltpu.VMEM((tq, ATTN_WIDTH), F32)],
    )
    return pl.pallas_call(
        _attn_body,
        grid_spec=grid_spec,
        out_shape=jax.ShapeDtypeStruct((t, ATTN_WIDTH), BF16),
        compiler_params=_params("arbitrary", "arbitrary"),
        name="attention",
    )(sinks, proj, proj, proj, qw2, kw2, onw)


def _ssd_body(z_ref, xs_ref, bc_ref, dt_ref, cwx_ref, cwb_ref, cbx_ref, cbb_ref, dtb_ref, alog_ref,
              dsk_ref, nw_ref, o_ref, xpx_ref, xpb_ref, st_ref, y_ref):
    i = pl.program_id(1)
    ts = xs_ref.shape[0]
    nch = ts // SSM_CHUNK
    gw = SSM_WIDTH // SSM_GROUPS
    pad = SUBLANES

    @pl.when(i == 0)
    def _():
        xpx_ref[0:pad, :] = jnp.zeros((pad, SSM_WIDTH), F32)
        xpb_ref[0:pad, :] = jnp.zeros((pad, 2 * BC_WIDTH), F32)
        st_ref[...] = jnp.zeros(st_ref.shape, F32)

    xpx_ref[pad:pad + ts, :] = xs_ref[...]
    xpb_ref[pad:pad + ts, :] = bc_ref[...]

    def conv_silu(xp_ref, w_ref, b_ref):
        acc = b_ref[...] + w_ref[SSM_CONV - 1:SSM_CONV, :] * xp_ref[pad:pad + ts, :]
        for k in range(SSM_CONV - 1):
            off = pad - (SSM_CONV - 1) + k
            acc = acc + w_ref[k:k + 1, :] * xp_ref[off:off + ts, :]
        return acc * _sigmoid(acc)

    xc = conv_silu(xpx_ref, cwx_ref, cbx_ref)
    bcc = conv_silu(xpb_ref, cwb_ref, cbb_ref)
    xpx_ref[0:pad, :] = xs_ref[ts - pad:ts, :]
    xpb_ref[0:pad, :] = bc_ref[ts - pad:ts, :]

    dtr = dt_ref[...] + dtb_ref[...]
    dt = jnp.maximum(dtr, 0.0) + jnp.log(1.0 + jnp.exp(-jnp.abs(dtr)))
    a = -jnp.exp(alog_ref[...])
    da = dt * a

    ri = _iota((ts, ts), 0)
    ci = _iota((ts, ts), 1)
    same = (ri // SSM_CHUNK) == (ci // SSM_CHUNK)
    tri_blk = jnp.where(same & (ci <= ri), 1.0, 0.0).astype(BF16)
    all_blk = jnp.where(same, 1.0, 0.0).astype(BF16)
    da3 = _split3(da)
    cs = _dot(tri_blk, da3[0]) + _dot(tri_blk, da3[1]) + _dot(tri_blk, da3[2])
    cl = _dot(all_blk, da3[0]) + _dot(all_blk, da3[1]) + _dot(all_blk, da3[2])
    ecs = jnp.exp(cs)
    dte = jnp.exp(cl - cs)

    expand = jnp.where(_iota((LANES, SSM_WIDTH), 1) // HEAD_DIM == _iota((LANES, SSM_WIDTH), 0),
                       1.0, 0.0).astype(BF16)
    dt_e = _dot_exact_lhs(_split2(dt), expand)
    dtd_e = _dot_exact_lhs(_split2(dt * dte), expand)
    ecs_e = _dot_exact_lhs(_split2(ecs), expand)
    xf = xc * dt_e
    xd = (xc * dtd_e).astype(BF16)

    lo_lanes = _iota((1, LANES), 1) < HEAD_DIM
    li = _iota((SSM_CHUNK, SSM_CHUNK), 0)
    lj = _iota((SSM_CHUNK, SSM_CHUNK), 1)
    tril = lj <= li
    upper = jnp.where(li <= lj, 1.0, 0.0).astype(BF16)

    for c in range(nch):
        r0 = c * SSM_CHUNK
        da_t = da[r0:r0 + SSM_CHUNK].T
        cs_t = _dot_exact_lhs(_split3(da_t), upper)
        cs_c = cs[r0:r0 + SSM_CHUNK]
        for g in range(SSM_GROUPS):
            b_c = bcc[r0:r0 + SSM_CHUNK, g * SSM_STATE:(g + 1) * SSM_STATE]
            c_c = bcc[r0:r0 + SSM_CHUNK, BC_WIDTH + g * SSM_STATE:BC_WIDTH + (g + 1) * SSM_STATE].astype(BF16)
            b_t = b_c.T.astype(BF16)
            cb = _dot(c_c, b_t)
            st = st_ref[g]
            y_off = _dot(c_c, st.astype(BF16)) * ecs_e[r0:r0 + SSM_CHUNK, g * gw:(g + 1) * gw]
            s_new = _dot(b_t, xd[r0:r0 + SSM_CHUNK, g * gw:(g + 1) * gw])
            parts = []
            for p in range(4):
                h0 = 8 * g + 2 * p
                ms = []
                for h in (h0, h0 + 1):
                    diff = cs_c[:, h:h + 1] - cs_t[h:h + 1, :]
                    ms.append((cb * jnp.exp(jnp.where(tril, diff, NEG_BIG))).astype(BF16))
                m2 = jnp.concatenate(ms, axis=1)
                x2 = xf[r0:r0 + SSM_CHUNK, h0 * HEAD_DIM:(h0 + 2) * HEAD_DIM]
                xbd = jnp.concatenate([jnp.where(lo_lanes, x2, 0.0), jnp.where(lo_lanes, 0.0, x2)],
                                      axis=0).astype(BF16)
                parts.append(_dot(m2, xbd))
            y_ref[r0:r0 + SSM_CHUNK, g * gw:(g + 1) * gw] = jnp.concatenate(parts, axis=1) + y_off
            dec = ecs_e[r0 + SSM_CHUNK - 1:r0 + SSM_CHUNK, g * gw:(g + 1) * gw]
            st_ref[g] = st * dec + s_new

    y = y_ref[...] + xc * dsk_ref[...]
    z = z_ref[...]
    gated = y * (z * _sigmoid(z))
    outs = []
    for g in range(SSM_GROUPS):
        gg = gated[:, g * gw:(g + 1) * gw]
        ms = jnp.mean(gg * gg, axis=-1, keepdims=True)
        outs.append(gg * lax.rsqrt(ms + EPS) * nw_ref[:, g * gw:(g + 1) * gw])
    o_ref[...] = jnp.concatenate(outs, axis=1).astype(BF16)


def ssd(proj, cwx, cwb, cbx, cbb, dtb, alog, dsk, nw, batch, seq):
    t = proj.shape[0]
    ts = min(512, seq)
    ns = seq // ts
    row = lambda b, i: b * ns + i
    const = lambda b, i: (0, 0)
    return pl.pallas_call(
        _ssd_body,
        grid=(batch, ns),
        in_specs=[
            pl.BlockSpec((ts, SSM_WIDTH), lambda b, i: (row(b, i), COL_Z // SSM_WIDTH)),
            pl.BlockSpec((ts, SSM_WIDTH), lambda b, i: (row(b, i), COL_XBC // SSM_WIDTH)),
            pl.BlockSpec((ts, 2 * BC_WIDTH), lambda b, i: (row(b, i), (COL_XBC + SSM_WIDTH) // (2 * BC_WIDTH))),
            pl.BlockSpec((ts, LANES), lambda b, i: (row(b, i), COL_DT // LANES)),
            pl.BlockSpec((SSM_CONV, SSM_WIDTH), const),
            pl.BlockSpec((SSM_CONV, 2 * BC_WIDTH), const),
            pl.BlockSpec((1, SSM_WIDTH), const),
            pl.BlockSpec((1, 2 * BC_WIDTH), const),
            pl.BlockSpec((1, LANES), const),
            pl.BlockSpec((1, LANES), const),
            pl.BlockSpec((1, SSM_WIDTH), const),
            pl.BlockSpec((1, SSM_WIDTH), const),
        ],
        out_specs=pl.BlockSpec((ts, SSM_WIDTH), lambda b, i: (row(b, i), 0)),
        out_shape=jax.ShapeDtypeStruct((t, SSM_WIDTH), BF16),
        scratch_shapes=[
            pltpu.VMEM((ts + SUBLANES, SSM_WIDTH), F32),
            pltpu.VMEM((ts + SUBLANES, 2 * BC_WIDTH), F32),
            pltpu.VMEM((SSM_GROUPS, SSM_STATE, SSM_WIDTH // SSM_GROUPS), F32),
            pltpu.VMEM((ts, SSM_WIDTH), F32),
        ],
        compiler_params=_params("arbitrary", "arbitrary"),
        name="ssd",
    )(proj, proj, proj, proj, cwx, cwb, cbx, cbb, dtb, alog, dsk, nw)


def _out_body(x_ref, a_ref, s_ref, wo_ref, fg_ref, wr_ref, br_ref, h_ref, hf_ref, idx_ref, gate_ref):
    tm = x_ref.shape[0]
    h = x_ref[...] + _dot(a_ref[...], wo_ref[0:ATTN_WIDTH, :]) + _dot(s_ref[...], wo_ref[ATTN_WIDTH:, :])
    h_ref[...] = h
    ms = jnp.mean(h * h, axis=-1, keepdims=True)
    hf = h * lax.rsqrt(ms + EPS) * fg_ref[...]
    hi, lo = _split2(hf)
    hi32 = hi.astype(F32)
    hf_ref[...] = (pltpu.bitcast(hi32[:, D_MODEL // 2:], jnp.uint32)
                   | (pltpu.bitcast(hi32[:, :D_MODEL // 2], jnp.uint32) >> 16))

    whi, wlo = _split2(wr_ref[...])
    hw = _dot(hi, jnp.concatenate([whi, wlo], axis=1))
    logits = hw[:, :LANES] + hw[:, LANES:] + _dot(lo, whi) + br_ref[...]
    lane = _iota((tm, LANES), 1)
    vals, idxs = [], []
    for _ in range(TOP_K):
        m = jnp.max(logits, axis=-1, keepdims=True)
        am = jnp.min(jnp.where(logits == m, lane, LANES), axis=-1, keepdims=True)
        vals.append(m)
        idxs.append(am)
        logits = jnp.where(lane == am, NEG_BIG * 2.0, logits)
    es = [jnp.exp(v - vals[0]) for v in vals]
    tot = es[0] + es[1] + es[2] + es[3]
    idx_out = jnp.zeros((tm, LANES), jnp.int32)
    gate_out = jnp.zeros((tm, LANES), F32)
    for k in range(TOP_K):
        idx_out = jnp.where(lane == k, idxs[k], idx_out)
        gate_out = jnp.where(lane == k, es[k] / tot, gate_out)
    idx_ref[...] = idx_out
    gate_ref[...] = gate_out


def out_proj_router(x2d, attn, ssm, wo, fg, wr, br):
    t, d = x2d.shape
    tm = min(512, t)
    const = lambda i: (0, 0)
    once = pl.Buffered(1)
    return pl.pallas_call(
        _out_body,
        grid=(t // tm,),
        in_specs=[
            pl.BlockSpec((tm, d), lambda i: (i, 0)),
            pl.BlockSpec((tm, ATTN_WIDTH), lambda i: (i, 0)),
            pl.BlockSpec((tm, SSM_WIDTH), lambda i: (i, 0)),
            pl.BlockSpec((d, d), const, pipeline_mode=once),
            pl.BlockSpec((1, d), const, pipeline_mode=once),
            pl.BlockSpec((d, LANES), const, pipeline_mode=once),
            pl.BlockSpec((1, LANES), const, pipeline_mode=once),
        ],
        out_specs=[
            pl.BlockSpec((tm, d), lambda i: (i, 0)),
            pl.BlockSpec((tm, d // 2), lambda i: (i, 0)),
            pl.BlockSpec((tm, LANES), lambda i: (i, 0)),
            pl.BlockSpec((tm, LANES), lambda i: (i, 0)),
        ],
        out_shape=[
            jax.ShapeDtypeStruct((t, d), F32),
            jax.ShapeDtypeStruct((t, d // 2), jnp.uint32),
            jax.ShapeDtypeStruct((t, LANES), jnp.int32),
            jax.ShapeDtypeStruct((t, LANES), F32),
        ],
        compiler_params=_params("arbitrary"),
        name="out_proj_router",
    )(x2d, attn, ssm, wo, fg, wr, br)


MOE_ROWS = 512
FF_TILE = 512
N_FF_TILES = D_FF // FF_TILE
WEIGHT_DMA_PRIORITY = 1


def _expert_body(be_ref, nv_ref, tok_ref, tokn_ref, dst_ref, dstp_ref, hf_ref, wg_ref, wu_ref, wd_ref, bg_ref, bu_ref,
                 bd_ref, yk_ref, xrow_ref, x_ref, h_ref, y_ref, gu_ref, dn_ref, row_sem, out_sem, gu_sem, dn_sem):
    b = pl.program_id(0)
    nv = nv_ref[0]
    tm = h_ref.shape[0]
    half = D_MODEL // 2

    def row_copy(table_ref, r):
        return pltpu.make_async_copy(hf_ref.at[pl.ds(table_ref[0, 0, r], 1)], xrow_ref.at[pl.ds(r, 1)], row_sem)

    def rows_wait():
        pltpu.make_async_copy(hf_ref.at[pl.ds(0, tm)], xrow_ref, row_sem).wait()

    def unpack_rows(slot):
        w = xrow_ref[...]
        x_ref[slot, :, 0:half] = pltpu.bitcast(w << 16, F32).astype(BF16)
        x_ref[slot, :, half:] = pltpu.bitcast(w & jnp.uint32(0xFFFF0000), F32).astype(BF16)

    def out_copy(table_ref, r):
        return pltpu.make_async_copy(y_ref.at[pl.ds(r, 1)], yk_ref.at[pl.ds(table_ref[0, 0, r], 1)], out_sem)

    def out_wait():
        pltpu.make_async_copy(y_ref, yk_ref.at[pl.ds(0, tm)], out_sem).wait()

    def gu_copies(e, j, slot):
        cols = pl.ds(j * FF_TILE, FF_TILE)
        return (pltpu.make_async_copy(wg_ref.at[e, :, cols], gu_ref.at[slot, 0], gu_sem.at[slot]),
                pltpu.make_async_copy(wu_ref.at[e, :, cols], gu_ref.at[slot, 1], gu_sem.at[slot]))

    def dn_copy(e, j, slot):
        return pltpu.make_async_copy(wd_ref.at[e, :, pl.ds(j * FF_TILE, FF_TILE)], dn_ref.at[slot], dn_sem.at[slot])

    per_batch = tm // N_FF_TILES

    @pl.when(b < nv)
    def _():
        e = be_ref[b]
        slot = b % 2

        @pl.when(b == 0)
        def _():
            def issue(r, carry):
                row_copy(tok_ref, r).start()
                return carry
            lax.fori_loop(0, tm, issue, 0, unroll=8)
            for cp in gu_copies(e, 0, 0):
                cp.start(priority=WEIGHT_DMA_PRIORITY)
            y_ref[...] = jnp.zeros(y_ref.shape, F32)
            rows_wait()
            unpack_rows(0)

        x = x_ref[slot]
        for j in range(N_FF_TILES):
            for cp in gu_copies(e, j, j % 2):
                cp.wait()
            if j + 1 < N_FF_TILES:
                for cp in gu_copies(e, j + 1, (j + 1) % 2):
                    cp.start(priority=WEIGHT_DMA_PRIORITY)
            else:
                dn_copy(e, 0, 0).start(priority=WEIGHT_DMA_PRIORITY)
            cols = slice(j * FF_TILE, (j + 1) * FF_TILE)
            g_acc = _dot(x, gu_ref[j % 2, 0])
            u_acc = _dot(x, gu_ref[j % 2, 1])
            for r in range(j * per_batch, (j + 1) * per_batch):
                row_copy(tokn_ref, r).start()
                out_copy(dstp_ref, r).start()
            gl = jnp.minimum(g_acc + bg_ref[0, :, cols], SWIGLU_LIMIT)
            li = jnp.clip(u_acc + bu_ref[0, :, cols], -SWIGLU_LIMIT, SWIGLU_LIMIT)
            h_ref[:, cols] = (gl * _sigmoid(SWIGLU_ALPHA * gl) * (li + 1.0)).astype(BF16)

        out_wait()
        rows_wait()

        h = h_ref[...]
        for j in range(N_FF_TILES):
            dn_copy(e, j, j % 2).wait()
            if j + 1 < N_FF_TILES:
                dn_copy(e, j + 1, (j + 1) % 2).start(priority=WEIGHT_DMA_PRIORITY)
            else:
                @pl.when(b + 1 < nv)
                def _():
                    for cp in gu_copies(be_ref[b + 1], 0, 0):
                        cp.start(priority=WEIGHT_DMA_PRIORITY)
            if j == 0:
                unpack_rows(1 - slot)
            cols = slice(j * FF_TILE, (j + 1) * FF_TILE)
            y_ref[:, cols] = _dot(h, dn_ref[j % 2]) + bd_ref[0, :, cols]

        @pl.when(b + 1 == nv)
        def _():
            def issue(r, carry):
                out_copy(dst_ref, r).start()
                return carry
            lax.fori_loop(0, tm, issue, 0, unroll=8)
            out_wait()


def experts(block_expert, n_valid, slot_tok, slot_dst, hf_packed, wg, bg, wu, bu, wd, bd, n_out_rows):
    n_blocks = block_expert.shape[0]
    tm = MOE_ROWS
    d = D_MODEL

    def blk(b, nv):
        return jnp.minimum(b, nv[0] - 1)

    any_spec = pl.BlockSpec(memory_space=pl.ANY)
    grid_spec = pltpu.PrefetchScalarGridSpec(
        num_scalar_prefetch=2,
        grid=(n_blocks,),
        in_specs=[
            pl.BlockSpec((1, 1, tm), lambda b, be, nv: (blk(b, nv), 0, 0), memory_space=pltpu.SMEM),
            pl.BlockSpec((1, 1, tm), lambda b, be, nv: (blk(b + 1, nv), 0, 0), memory_space=pltpu.SMEM),
            pl.BlockSpec((1, 1, tm), lambda b, be, nv: (blk(b, nv) + 1, 0, 0), memory_space=pltpu.SMEM),
            pl.BlockSpec((1, 1, tm), lambda b, be, nv: (blk(b, nv), 0, 0), memory_space=pltpu.SMEM),
            any_spec, any_spec, any_spec, any_spec,
            pl.BlockSpec((1, 1, D_FF), lambda b, be, nv: (be[blk(b, nv)], 0, 0)),
            pl.BlockSpec((1, 1, D_FF), lambda b, be, nv: (be[blk(b, nv)], 0, 0)),
            pl.BlockSpec((1, 1, d), lambda b, be, nv: (be[blk(b, nv)], 0, 0)),
        ],
        out_specs=any_spec,
        scratch_shapes=[
            pltpu.VMEM((tm, d // 2), jnp.uint32),
            pltpu.VMEM((2, tm, d), BF16),
            pltpu.VMEM((tm, D_FF), BF16),
            pltpu.VMEM((tm, d), F32),
            pltpu.VMEM((2, 2, d, FF_TILE), BF16),
            pltpu.VMEM((2, D_FF, FF_TILE), BF16),
            pltpu.SemaphoreType.DMA,
            pltpu.SemaphoreType.DMA,
            pltpu.SemaphoreType.DMA((2,)),
            pltpu.SemaphoreType.DMA((2,)),
        ],
    )
    return pl.pallas_call(
        _expert_body,
        grid_spec=grid_spec,
        out_shape=jax.ShapeDtypeStruct((n_out_rows, d), F32),
        compiler_params=_params("arbitrary"),
        name="experts",
    )(block_expert, n_valid, slot_tok, slot_tok, slot_dst, slot_dst, hf_packed, wg, wu, wd, bg, bu, bd)


COMBINE_ROWS = 256


def _combine_body(h_ref, gate_ref, yk_ref, o_ref):
    tm = h_ref.shape[0]
    gates = gate_ref[...]
    acc = h_ref[...]
    for k in range(TOP_K):
        acc = acc + gates[:, k:k + 1] * yk_ref[k * tm:(k + 1) * tm, :]
    o_ref[...] = acc


def combine(h, gates, yk):
    t, d = h.shape
    tm = min(COMBINE_ROWS, t)
    return pl.pallas_call(
        _combine_body,
        grid=(t // tm,),
        in_specs=[
            pl.BlockSpec((tm, d), lambda i: (i, 0)),
            pl.BlockSpec((tm, LANES), lambda i: (i, 0)),
            pl.BlockSpec((TOP_K * tm, d), lambda i: (i, 0)),
        ],
        out_specs=pl.BlockSpec((tm, d), lambda i: (i, 0)),
        out_shape=jax.ShapeDtypeStruct((t, d), F32),
        compiler_params=_params("arbitrary"),
        name="combine",
    )(h, gates, yk)


def routing_tables(top_idx, t):
    n_assign = t * TOP_K
    tm_c = min(COMBINE_ROWS, t)
    e_flat = top_idx.reshape(-1).astype(jnp.int32)
    _, order = lax.sort((e_flat, jnp.arange(n_assign, dtype=jnp.int32)), num_keys=1, is_stable=True)
    experts_iota = jnp.arange(N_EXPERTS, dtype=jnp.int32)
    counts = jnp.sum((e_flat[None, :] == experts_iota[:, None]).astype(jnp.int32), axis=1)
    starts = jnp.cumsum(counts) - counts
    padded = (counts + MOE_ROWS - 1) // MOE_ROWS * MOE_ROWS
    pends = jnp.cumsum(padded)
    pstarts = pends - padded
    n_blocks = -(-n_assign // MOE_ROWS) + N_EXPERTS
    block_start = jnp.arange(n_blocks, dtype=jnp.int32) * MOE_ROWS
    block_expert = jnp.minimum(
        jnp.sum((pends[None, :] <= block_start[:, None]).astype(jnp.int32), axis=1), N_EXPERTS - 1)
    n_valid = (pends[-1:] // MOE_ROWS).astype(jnp.int32)

    row = jnp.arange(MOE_ROWS, dtype=jnp.int32)[None, :]
    e_b = block_expert[:, None]
    local = jnp.arange(n_blocks, dtype=jnp.int32)[:, None] * MOE_ROWS + row - pstarts[e_b]
    used = (local >= 0) & (local < counts[e_b])
    asg = order[jnp.clip(starts[e_b] + local, 0, n_assign - 1)]
    tok = asg // TOP_K
    dst = (tok // tm_c) * (TOP_K * tm_c) + (asg % TOP_K) * tm_c + tok % tm_c
    slot_tok = jnp.where(used, tok, 0).reshape(n_blocks, 1, MOE_ROWS)
    slot_dst = jnp.concatenate([n_assign + row, jnp.where(used, dst, n_assign + row)], axis=0)
    return block_expert, n_valid, slot_tok, slot_dst.reshape(n_blocks + 1, 1, MOE_ROWS)


def _pad_lanes(v, width, value=0.0):
    return jnp.pad(v, ((0, 0), (0, width - v.shape[1])), constant_values=value)


def hybrid_layer(x, attn_norm_w, w_in, q_norm_w, k_norm_w, attn_sinks, conv_w, conv_b, dt_bias, a_log, d_skip,
                 ssm_norm_w, attn_out_norm_w, w_out, ffn_norm_w, w_router, b_router, w_gate, b_gate, w_up, b_up,
                 w_down, b_down):
    batch, seq, d = x.shape
    t = batch * seq
    x2d = x.reshape(t, d)

    q_w, k_w, v_w, z_w, xbc_w, dt_w = jnp.split(
        w_in, [ATTN_WIDTH, ATTN_WIDTH + KV_WIDTH, ATTN_WIDTH + 2 * KV_WIDTH,
               ATTN_WIDTH + 2 * KV_WIDTH + SSM_WIDTH, ATTN_WIDTH + 2 * KV_WIDTH + SSM_WIDTH + CONV_WIDTH], axis=1)
    w_p = jnp.concatenate([q_w, z_w, xbc_w, k_w, v_w, dt_w], axis=1)
    w_p = _pad_lanes(w_p, PROJ_WIDTH).astype(BF16)
    proj = in_proj(x2d, attn_norm_w.reshape(1, d), w_p)

    qw2 = (jnp.tile(q_norm_w, 2) * (HEAD_DIM ** -0.5)).reshape(1, LANES)
    kw2 = jnp.tile(k_norm_w, 2).reshape(1, LANES)
    attn = attention(proj, attn_sinks, qw2, kw2, attn_out_norm_w.reshape(1, ATTN_WIDTH), batch, seq)

    ssm = ssd(
        proj,
        conv_w[:, :SSM_WIDTH], conv_w[:, SSM_WIDTH:],
        conv_b[:SSM_WIDTH].reshape(1, -1), conv_b[SSM_WIDTH:].reshape(1, -1),
        _pad_lanes(dt_bias.reshape(1, -1), LANES), _pad_lanes(a_log.reshape(1, -1), LANES),
        jnp.repeat(d_skip, HEAD_DIM).reshape(1, SSM_WIDTH), ssm_norm_w.reshape(1, SSM_WIDTH),
        batch, seq)

    h, hf_packed, idx_pad, gate_pad = out_proj_router(
        x2d, attn, ssm, w_out.astype(BF16), ffn_norm_w.reshape(1, d),
        _pad_lanes(w_router, LANES), _pad_lanes(b_router.reshape(1, -1), LANES, NEG_BIG))

    block_expert, n_valid, slot_tok, slot_dst = routing_tables(idx_pad[:, :TOP_K], t)
    yk = experts(
        block_expert, n_valid, slot_tok, slot_dst, hf_packed,
        w_gate.astype(BF16), b_gate.reshape(N_EXPERTS, 1, D_FF),
        w_up.astype(BF16), b_up.reshape(N_EXPERTS, 1, D_FF),
        w_down.astype(BF16), b_down.reshape(N_EXPERTS, 1, D_MODEL),
        t * TOP_K + MOE_ROWS)

    out = combine(h, gate_pad, yk)
    return out.reshape(batch, seq, d)


def kernel(x, attn_norm_w, w_in, q_norm_w, k_norm_w, attn_sinks, conv_w, conv_b, dt_bias, a_log, d_skip, ssm_norm_w,
           attn_out_norm_w, w_out, ffn_norm_w, w_router, b_router, w_gate, b_gate, w_up, b_up, w_down, b_down):
    for i in range(attn_norm_w.shape[0]):
        x = hybrid_layer(x, attn_norm_w[i], w_in[i], q_norm_w[i], k_norm_w[i], attn_sinks[i], conv_w[i], conv_b[i],
                         dt_bias[i], a_log[i], d_skip[i], ssm_norm_w[i], attn_out_norm_w[i], w_out[i], ffn_norm_w[i],
                         w_router[i], b_router[i], w_gate[i], b_gate[i], w_up[i], b_up[i], w_down[i], b_down[i])
    return x
```

```python
import functools

import jax
import jax.numpy as jnp
from jax import lax
from jax.experimental import pallas as pl
from jax.experimental.pallas import tpu as pltpu

F32 = jnp.float32
BF16 = jnp.bfloat16

D_MODEL = 2048
HEAD_DIM = 64
ATTN_WIDTH = 1024
KV_WIDTH = 128
WINDOW = 128
SSM_WIDTH = 1024
SSM_HEADS = 16
SSM_GROUPS = 2
SSM_STATE = 128
SSM_CONV = 4
SSM_CHUNK = 128
BC_WIDTH = SSM_GROUPS * SSM_STATE
CONV_WIDTH = SSM_WIDTH + 2 * BC_WIDTH
N_EXPERTS = 32
TOP_K = 4
D_FF = 2048
SWIGLU_LIMIT = 7.0
SWIGLU_ALPHA = 1.702
EPS = 1e-6

LANES = 128
SUBLANES = 8
VMEM_LIMIT = 56 * 1024 * 1024

PROJ_WIDTH = 4096
COL_Q = 0
COL_Z = ATTN_WIDTH
COL_XBC = COL_Z + SSM_WIDTH
COL_K = COL_XBC + CONV_WIDTH
COL_V = COL_K + KV_WIDTH
COL_DT = COL_V + KV_WIDTH
NEG_BIG = -1e30


def _params(*sem):
    return pltpu.CompilerParams(dimension_semantics=sem, vmem_limit_bytes=VMEM_LIMIT)


def _split2(v):
    hi = v.astype(BF16)
    lo = (v - hi.astype(F32)).astype(BF16)
    return hi, lo


def _split3(v):
    hi = v.astype(BF16)
    r = v - hi.astype(F32)
    mid = r.astype(BF16)
    lo = (r - mid.astype(F32)).astype(BF16)
    return hi, mid, lo


def _dot(a, b):
    return jnp.dot(a, b, preferred_element_type=F32)


def _dot_exact_lhs(parts, m):
    acc = _dot(parts[0], m)
    for p in parts[1:]:
        acc = acc + _dot(p, m)
    return acc


def _sigmoid(v):
    return 1.0 / (1.0 + jnp.exp(-v))


def _iota(shape, dim):
    return lax.broadcasted_iota(jnp.int32, shape, dim)


def _in_proj_body(x0_ref, xn_ref, g_ref, w_ref, o_ref, hn_ref):
    i = pl.program_id(0)

    def norm(x):
        ms = jnp.mean(x * x, axis=-1, keepdims=True)
        return (x * lax.rsqrt(ms + EPS) * g_ref[...]).astype(BF16)

    @pl.when(i == 0)
    def _():
        hn_ref[0] = norm(x0_ref[...])

    slot = i % 2
    hn_ref[1 - slot] = norm(xn_ref[...])
    o_ref[...] = _dot(hn_ref[slot], w_ref[...])


def in_proj(x2d, g, w_p):
    t, d = x2d.shape
    n = w_p.shape[1]
    tm = min(512, t)
    ni = t // tm
    once = pl.Buffered(1)
    return pl.pallas_call(
        _in_proj_body,
        grid=(ni,),
        in_specs=[
            pl.BlockSpec((tm, d), lambda i: (0, 0), pipeline_mode=once),
            pl.BlockSpec((tm, d), lambda i: (jnp.minimum(i + 1, ni - 1), 0)),
            pl.BlockSpec((1, d), lambda i: (0, 0), pipeline_mode=once),
            pl.BlockSpec((d, n), lambda i: (0, 0), pipeline_mode=once),
        ],
        out_specs=pl.BlockSpec((tm, n), lambda i: (i, 0)),
        out_shape=jax.ShapeDtypeStruct((t, n), F32),
        scratch_shapes=[pltpu.VMEM((2, tm, d), BF16)],
        compiler_params=_params("arbitrary"),
        name="in_proj",
    )(x2d, x2d, g, w_p)


def _head_pair_norm(v, w, blockdiag):
    ss = _dot_exact_lhs(_split2(v * v), blockdiag)
    return v * lax.rsqrt(ss * (1.0 / HEAD_DIM) + EPS) * w


def _cast_side_streams(refs, n_side):
    for k in range(n_side):
        refs[n_side + 1 + k][...] = refs[k][...].astype(BF16)
    return refs[n_side]


def _side_specs(side, n_steps, step_index):
    specs, shapes = [], []
    for w in side:
        rows, cols = w.shape
        specs.append(pl.BlockSpec((rows // n_steps, cols), step_index))
        shapes.append(jax.ShapeDtypeStruct((rows, cols), BF16))
    return specs, shapes


def _attn_body(n_side, sinks_ref, q_ref, kvc_ref, kvp_ref, qw_ref, kw_ref, onw_ref, *rest):
    o_ref = _cast_side_streams(rest, n_side)
    acc_ref = rest[-1]
    i = pl.program_id(1)
    tq = q_ref.shape[0]
    nsub = tq // WINDOW
    nk = tq + WINDOW
    lo_lanes = _iota((1, LANES), 1) < HEAD_DIM
    blockdiag = jnp.where(
        _iota((LANES, LANES), 0) // HEAD_DIM == _iota((LANES, LANES), 1) // HEAD_DIM, 1.0, 0.0
    ).astype(BF16)

    kv_all = jnp.concatenate([kvp_ref[...], kvc_ref[...]], axis=0)
    k_all = _head_pair_norm(kv_all[:, :KV_WIDTH], kw_ref[...], blockdiag)
    v_all = kv_all[:, KV_WIDTH:]
    kt = k_all.T
    zero_half = jnp.zeros((HEAD_DIM, nk), F32)
    kt_var = [
        [jnp.concatenate([kt[j * HEAD_DIM:(j + 1) * HEAD_DIM], zero_half], axis=0).astype(BF16),
         jnp.concatenate([zero_half, kt[j * HEAD_DIM:(j + 1) * HEAD_DIM]], axis=0).astype(BF16)]
        for j in range(2)
    ]
    v_sw = pltpu.roll(v_all, HEAD_DIM, axis=1)
    v_var = [
        [jnp.where(lo_lanes, v_all, 0.0).astype(BF16), jnp.where(lo_lanes, 0.0, v_sw).astype(BF16)],
        [jnp.where(lo_lanes, v_sw, 0.0).astype(BF16), jnp.where(lo_lanes, 0.0, v_all).astype(BF16)],
    ]

    qn = [
        _head_pair_norm(q_ref[:, c * LANES:(c + 1) * LANES], qw_ref[...], blockdiag).astype(BF16)
        for c in range(ATTN_WIDTH // LANES)
    ]

    rows = 4 * WINDOW
    qi = _iota((rows, 2 * WINDOW), 0) % WINDOW
    kj = _iota((rows, 2 * WINDOW), 1)
    band = (kj > qi) & (kj <= qi + WINDOW)
    first_min = jnp.where(i == 0, WINDOW, 0)
    band_first = band & (kj >= first_min)

    for s in range(nsub):
        mask = band_first if s == 0 else band
        for j in range(2):
            qs = jnp.concatenate([qn[4 * j + c][s * WINDOW:(s + 1) * WINDOW] for c in range(4)], axis=0)
            out = None
            for par in range(2):
                sink = jnp.concatenate(
                    [jnp.full((WINDOW, 1), sinks_ref[8 * j + 2 * c + par], F32) for c in range(4)], axis=0)
                sc = _dot(qs, kt_var[j][par][:, s * WINDOW:s * WINDOW + 2 * WINDOW])
                sc = jnp.where(mask, sc, NEG_BIG)
                m = jnp.maximum(jnp.max(sc, axis=-1, keepdims=True), sink)
                p = jnp.exp(sc - m)
                den = jnp.sum(p, axis=-1, keepdims=True) + jnp.exp(sink - m)
                pv = _dot(p.astype(BF16), v_var[j][par][s * WINDOW:s * WINDOW + 2 * WINDOW])
                pv = pv * (1.0 / den)
                out = pv if out is None else out + pv
            for c in range(4):
                acc_ref[s * WINDOW:(s + 1) * WINDOW, (4 * j + c) * LANES:(4 * j + c + 1) * LANES] = (
                    out[c * WINDOW:(c + 1) * WINDOW])

    a = acc_ref[...]
    ms = jnp.mean(a * a, axis=-1, keepdims=True)
    o_ref[...] = (a * lax.rsqrt(ms + EPS) * onw_ref[...]).astype(BF16)


def attention(proj, sinks, qw2, kw2, onw, batch, seq, side=()):
    t = proj.shape[0]
    tq = min(256, seq)
    nq = seq // tq
    sub = tq // WINDOW
    kv_col = COL_K // (2 * KV_WIDTH)
    side_specs, side_shapes = _side_specs(side, batch * nq, lambda b, i, s: (b * nq + i, 0))
    grid_spec = pltpu.PrefetchScalarGridSpec(
        num_scalar_prefetch=1,
        grid=(batch, nq),
        in_specs=[
            pl.BlockSpec((tq, ATTN_WIDTH), lambda b, i, s: (b * nq + i, COL_Q // ATTN_WIDTH)),
            pl.BlockSpec((tq, 2 * KV_WIDTH), lambda b, i, s: (b * nq + i, kv_col)),
            pl.BlockSpec((WINDOW, 2 * KV_WIDTH),
                         lambda b, i, s: (b * nq * sub + jnp.maximum(i * sub - 1, 0), kv_col)),
            pl.BlockSpec((1, LANES), lambda b, i, s: (0, 0)),
            pl.BlockSpec((1, LANES), lambda b, i, s: (0, 0)),
            pl.BlockSpec((1, ATTN_WIDTH), lambda b, i, s: (0, 0)),
        ] + side_specs,
        out_specs=[pl.BlockSpec((tq, ATTN_WIDTH), lambda b, i, s: (b * nq + i, 0))] + side_specs,
        scratch_shapes=[pltpu.VMEM((tq, ATTN_WIDTH), F32)],
    )
    outs = pl.pallas_call(
        functools.partial(_attn_body, len(side)),
        grid_spec=grid_spec,
        out_shape=[jax.ShapeDtypeStruct((t, ATTN_WIDTH), BF16)] + side_shapes,
        compiler_params=_params("arbitrary", "arbitrary"),
        name="attention",
    )(sinks, proj, proj, proj, qw2, kw2, onw, *side)
    return outs[0], outs[1:]


def _ssd_body(n_side, z_ref, xs_ref, bc_ref, dt_ref, cwx_ref, cwb_ref, cbx_ref, cbb_ref, dtb_ref, alog_ref,
              dsk_ref, nw_ref, *rest):
    o_ref = _cast_side_streams(rest, n_side)
    xpx_ref, xpb_ref, st_ref, y_ref = rest[-4:]
    i = pl.program_id(1)
    ts = xs_ref.shape[0]
    nch = ts // SSM_CHUNK
    gw = SSM_WIDTH // SSM_GROUPS
    pad = SUBLANES

    @pl.when(i == 0)
    def _():
        xpx_ref[0:pad, :] = jnp.zeros((pad, SSM_WIDTH), F32)
        xpb_ref[0:pad, :] = jnp.zeros((pad, 2 * BC_WIDTH), F32)
        st_ref[...] = jnp.zeros(st_ref.shape, F32)

    xpx_ref[pad:pad + ts, :] = xs_ref[...]
    xpb_ref[pad:pad + ts, :] = bc_ref[...]

    def conv_silu(xp_ref, w_ref, b_ref):
        acc = b_ref[...] + w_ref[SSM_CONV - 1:SSM_CONV, :] * xp_ref[pad:pad + ts, :]
        for k in range(SSM_CONV - 1):
            off = pad - (SSM_CONV - 1) + k
            acc = acc + w_ref[k:k + 1, :] * xp_ref[off:off + ts, :]
        return acc * _sigmoid(acc)

    xc = conv_silu(xpx_ref, cwx_ref, cbx_ref)
    bcc = conv_silu(xpb_ref, cwb_ref, cbb_ref)
    xpx_ref[0:pad, :] = xs_ref[ts - pad:ts, :]
    xpb_ref[0:pad, :] = bc_ref[ts - pad:ts, :]

    dtr = dt_ref[...] + dtb_ref[...]
    dt = jnp.maximum(dtr, 0.0) + jnp.log(1.0 + jnp.exp(-jnp.abs(dtr)))
    a = -jnp.exp(alog_ref[...])
    da = dt * a

    ri = _iota((ts, ts), 0)
    ci = _iota((ts, ts), 1)
    same = (ri // SSM_CHUNK) == (ci // SSM_CHUNK)
    tri_blk = jnp.where(same & (ci <= ri), 1.0, 0.0).astype(BF16)
    all_blk = jnp.where(same, 1.0, 0.0).astype(BF16)
    da3 = _split3(da)
    cs = _dot(tri_blk, da3[0]) + _dot(tri_blk, da3[1]) + _dot(tri_blk, da3[2])
    cl = _dot(all_blk, da3[0]) + _dot(all_blk, da3[1]) + _dot(all_blk, da3[2])
    ecs = jnp.exp(cs)
    dte = jnp.exp(cl - cs)

    expand = jnp.where(_iota((LANES, SSM_WIDTH), 1) // HEAD_DIM == _iota((LANES, SSM_WIDTH), 0),
                       1.0, 0.0).astype(BF16)
    dt_e = _dot_exact_lhs(_split2(dt), expand)
    dtd_e = _dot_exact_lhs(_split2(dt * dte), expand)
    ecs_e = _dot_exact_lhs(_split2(ecs), expand)
    xf = xc * dt_e
    xd = (xc * dtd_e).astype(BF16)

    lo_lanes = _iota((1, LANES), 1) < HEAD_DIM
    li = _iota((SSM_CHUNK, SSM_CHUNK), 0)
    lj = _iota((SSM_CHUNK, SSM_CHUNK), 1)
    tril = lj <= li
    upper = jnp.where(li <= lj, 1.0, 0.0).astype(BF16)

    for c in range(nch):
        r0 = c * SSM_CHUNK
        da_t = da[r0:r0 + SSM_CHUNK].T
        cs_t = _dot_exact_lhs(_split3(da_t), upper)
        cs_c = cs[r0:r0 + SSM_CHUNK]
        for g in range(SSM_GROUPS):
            b_c = bcc[r0:r0 + SSM_CHUNK, g * SSM_STATE:(g + 1) * SSM_STATE]
            c_c = bcc[r0:r0 + SSM_CHUNK, BC_WIDTH + g * SSM_STATE:BC_WIDTH + (g + 1) * SSM_STATE].astype(BF16)
            b_t = b_c.T.astype(BF16)
            cb = _dot(c_c, b_t)
            st = st_ref[g]
            y_off = _dot(c_c, st.astype(BF16)) * ecs_e[r0:r0 + SSM_CHUNK, g * gw:(g + 1) * gw]
            s_new = _dot(b_t, xd[r0:r0 + SSM_CHUNK, g * gw:(g + 1) * gw])
            parts = []
            for p in range(4):
                h0 = 8 * g + 2 * p
                ms = []
                for h in (h0, h0 + 1):
                    diff = cs_c[:, h:h + 1] - cs_t[h:h + 1, :]
                    ms.append((cb * jnp.exp(jnp.where(tril, diff, NEG_BIG))).astype(BF16))
                m2 = jnp.concatenate(ms, axis=1)
                x2 = xf[r0:r0 + SSM_CHUNK, h0 * HEAD_DIM:(h0 + 2) * HEAD_DIM]
                xbd = jnp.concatenate([jnp.where(lo_lanes, x2, 0.0), jnp.where(lo_lanes, 0.0, x2)],
                                      axis=0).astype(BF16)
                parts.append(_dot(m2, xbd))
            y_ref[r0:r0 + SSM_CHUNK, g * gw:(g + 1) * gw] = jnp.concatenate(parts, axis=1) + y_off
            dec = ecs_e[r0 + SSM_CHUNK - 1:r0 + SSM_CHUNK, g * gw:(g + 1) * gw]
            st_ref[g] = st * dec + s_new

    y = y_ref[...] + xc * dsk_ref[...]
    z = z_ref[...]
    gated = y * (z * _sigmoid(z))
    outs = []
    for g in range(SSM_GROUPS):
        gg = gated[:, g * gw:(g + 1) * gw]
        ms = jnp.mean(gg * gg, axis=-1, keepdims=True)
        outs.append(gg * lax.rsqrt(ms + EPS) * nw_ref[:, g * gw:(g + 1) * gw])
    o_ref[...] = jnp.concatenate(outs, axis=1).astype(BF16)


def ssd(proj, cwx, cwb, cbx, cbb, dtb, alog, dsk, nw, batch, seq, side=()):
    t = proj.shape[0]
    ts = min(256, seq)
    ns = seq // ts
    row = lambda b, i: b * ns + i
    const = lambda b, i: (0, 0)
    side_specs, side_shapes = _side_specs(side, batch * ns, lambda b, i: (row(b, i), 0))
    outs = pl.pallas_call(
        functools.partial(_ssd_body, len(side)),
        grid=(batch, ns),
        in_specs=[
            pl.BlockSpec((ts, SSM_WIDTH), lambda b, i: (row(b, i), COL_Z // SSM_WIDTH)),
            pl.BlockSpec((ts, SSM_WIDTH), lambda b, i: (row(b, i), COL_XBC // SSM_WIDTH)),
            pl.BlockSpec((ts, 2 * BC_WIDTH), lambda b, i: (row(b, i), (COL_XBC + SSM_WIDTH) // (2 * BC_WIDTH))),
            pl.BlockSpec((ts, LANES), lambda b, i: (row(b, i), COL_DT // LANES)),
            pl.BlockSpec((SSM_CONV, SSM_WIDTH), const),
            pl.BlockSpec((SSM_CONV, 2 * BC_WIDTH), const),
            pl.BlockSpec((1, SSM_WIDTH), const),
            pl.BlockSpec((1, 2 * BC_WIDTH), const),
            pl.BlockSpec((1, LANES), const),
            pl.BlockSpec((1, LANES), const),
            pl.BlockSpec((1, SSM_WIDTH), const),
            pl.BlockSpec((1, SSM_WIDTH), const),
        ] + side_specs,
        out_specs=[pl.BlockSpec((ts, SSM_WIDTH), lambda b, i: (row(b, i), 0))] + side_specs,
        out_shape=[jax.ShapeDtypeStruct((t, SSM_WIDTH), BF16)] + side_shapes,
        scratch_shapes=[
            pltpu.VMEM((ts + SUBLANES, SSM_WIDTH), F32),
            pltpu.VMEM((ts + SUBLANES, 2 * BC_WIDTH), F32),
            pltpu.VMEM((SSM_GROUPS, SSM_STATE, SSM_WIDTH // SSM_GROUPS), F32),
            pltpu.VMEM((ts, SSM_WIDTH), F32),
        ],
        compiler_params=_params("arbitrary", "arbitrary"),
        name="ssd",
    )(proj, proj, proj, proj, cwx, cwb, cbx, cbb, dtb, alog, dsk, nw, *side)
    return outs[0], outs[1:]


def _out_body(x_ref, a_ref, s_ref, wo_ref, fg_ref, wr_ref, br_ref, h_ref, hf_ref, idx_ref, gate_ref):
    tm = x_ref.shape[0]
    h = x_ref[...] + _dot(a_ref[...], wo_ref[0:ATTN_WIDTH, :]) + _dot(s_ref[...], wo_ref[ATTN_WIDTH:, :])
    h_ref[...] = h
    ms = jnp.mean(h * h, axis=-1, keepdims=True)
    hf = h * lax.rsqrt(ms + EPS) * fg_ref[...]
    hi, lo = _split2(hf)
    hi32 = hi.astype(F32)
    hf_ref[...] = (pltpu.bitcast(hi32[:, D_MODEL // 2:], jnp.uint32)
                   | (pltpu.bitcast(hi32[:, :D_MODEL // 2], jnp.uint32) >> 16))

    whi, wlo = _split2(wr_ref[...])
    hw = _dot(hi, jnp.concatenate([whi, wlo], axis=1))
    logits = hw[:, :LANES] + hw[:, LANES:] + _dot(lo, whi) + br_ref[...]
    lane = _iota((tm, LANES), 1)
    vals, idxs = [], []
    for _ in range(TOP_K):
        m = jnp.max(logits, axis=-1, keepdims=True)
        am = jnp.min(jnp.where(logits == m, lane, LANES), axis=-1, keepdims=True)
        vals.append(m)
        idxs.append(am)
        logits = jnp.where(lane == am, NEG_BIG * 2.0, logits)
    es = [jnp.exp(v - vals[0]) for v in vals]
    tot = es[0] + es[1] + es[2] + es[3]
    idx_out = jnp.zeros((tm, LANES), jnp.int32)
    gate_out = jnp.zeros((tm, LANES), F32)
    for k in range(TOP_K):
        idx_out = jnp.where(lane == k, idxs[k], idx_out)
        gate_out = jnp.where(lane == k, es[k] / tot, gate_out)
    idx_ref[...] = idx_out
    gate_ref[...] = gate_out


def out_proj_router(x2d, attn, ssm, wo, fg, wr, br):
    t, d = x2d.shape
    tm = min(512, t)
    const = lambda i: (0, 0)
    once = pl.Buffered(1)
    return pl.pallas_call(
        _out_body,
        grid=(t // tm,),
        in_specs=[
            pl.BlockSpec((tm, d), lambda i: (i, 0)),
            pl.BlockSpec((tm, ATTN_WIDTH), lambda i: (i, 0)),
            pl.BlockSpec((tm, SSM_WIDTH), lambda i: (i, 0)),
            pl.BlockSpec((d, d), const, pipeline_mode=once),
            pl.BlockSpec((1, d), const, pipeline_mode=once),
            pl.BlockSpec((d, LANES), const, pipeline_mode=once),
            pl.BlockSpec((1, LANES), const, pipeline_mode=once),
        ],
        out_specs=[
            pl.BlockSpec((tm, d), lambda i: (i, 0)),
            pl.BlockSpec((tm, d // 2), lambda i: (i, 0)),
            pl.BlockSpec((tm, LANES), lambda i: (i, 0)),
            pl.BlockSpec((tm, LANES), lambda i: (i, 0)),
        ],
        out_shape=[
            jax.ShapeDtypeStruct((t, d), F32),
            jax.ShapeDtypeStruct((t, d // 2), jnp.uint32),
            jax.ShapeDtypeStruct((t, LANES), jnp.int32),
            jax.ShapeDtypeStruct((t, LANES), F32),
        ],
        compiler_params=_params("arbitrary"),
        name="out_proj_router",
    )(x2d, attn, ssm, wo, fg, wr, br)


MOE_ROWS = 512
FF_TILE = 1024
N_FF_TILES = D_FF // FF_TILE


def _expert_body(be_ref, nv_ref, tok_ref, tokn_ref, dst_ref, dstp_ref, hf_ref, wg_ref, wu_ref, wd_ref, bg_ref, bu_ref,
                 bd_ref, yk_ref, xrow_ref, x_ref, h_ref, y_ref, gu_ref, dn_ref, row_sem, out_sem, gu_sem, dn_sem):
    b = pl.program_id(0)
    nv = nv_ref[0]
    tm = h_ref.shape[0]
    half = D_MODEL // 2

    def row_copy(table_ref, r):
        return pltpu.make_async_copy(hf_ref.at[pl.ds(table_ref[0, 0, r], 1)], xrow_ref.at[pl.ds(r, 1)], row_sem)

    def rows_wait():
        pltpu.make_async_copy(hf_ref.at[pl.ds(0, tm)], xrow_ref, row_sem).wait()

    def unpack_rows(slot):
        w = xrow_ref[...]
        x_ref[slot, :, 0:half] = pltpu.bitcast(w << 16, F32).astype(BF16)
        x_ref[slot, :, half:] = pltpu.bitcast(w & jnp.uint32(0xFFFF0000), F32).astype(BF16)

    def out_copy(table_ref, r):
        return pltpu.make_async_copy(y_ref.at[pl.ds(r, 1)], yk_ref.at[pl.ds(table_ref[0, 0, r], 1)], out_sem)

    def out_wait():
        pltpu.make_async_copy(y_ref, yk_ref.at[pl.ds(0, tm)], out_sem).wait()

    def gu_copies(e, j, slot):
        cols = pl.ds(j * FF_TILE, FF_TILE)
        return (pltpu.make_async_copy(wg_ref.at[e, :, cols], gu_ref.at[slot, 0], gu_sem.at[slot]),
                pltpu.make_async_copy(wu_ref.at[e, :, cols], gu_ref.at[slot, 1], gu_sem.at[slot]))

    def dn_copy(e, j, slot):
        return pltpu.make_async_copy(wd_ref.at[e, :, pl.ds(j * FF_TILE, FF_TILE)], dn_ref.at[slot], dn_sem.at[slot])

    per_batch = tm // N_FF_TILES

    @pl.when(b < nv)
    def _():
        e = be_ref[b]
        slot = b % 2

        @pl.when(b == 0)
        def _():
            def issue(r, carry):
                row_copy(tok_ref, r).start()
                return carry
            lax.fori_loop(0, tm, issue, 0, unroll=8)
            for cp in gu_copies(e, 0, 0):
                cp.start()
            y_ref[...] = jnp.zeros(y_ref.shape, F32)
            rows_wait()
            unpack_rows(0)

        x = x_ref[slot]
        for j in range(N_FF_TILES):
            for cp in gu_copies(e, j, j % 2):
                cp.wait()
            if j + 1 < N_FF_TILES:
                for cp in gu_copies(e, j + 1, (j + 1) % 2):
                    cp.start()
            else:
                dn_copy(e, 0, 0).start()
            cols = slice(j * FF_TILE, (j + 1) * FF_TILE)
            g_acc = _dot(x, gu_ref[j % 2, 0])
            u_acc = _dot(x, gu_ref[j % 2, 1])
            for r in range(j * per_batch, (j + 1) * per_batch):
                row_copy(tokn_ref, r).start()
                out_copy(dstp_ref, r).start()
            gl = jnp.minimum(g_acc + bg_ref[0, :, cols], SWIGLU_LIMIT)
            li = jnp.clip(u_acc + bu_ref[0, :, cols], -SWIGLU_LIMIT, SWIGLU_LIMIT)
            h_ref[:, cols] = (gl * _sigmoid(SWIGLU_ALPHA * gl) * (li + 1.0)).astype(BF16)

        out_wait()
        rows_wait()

        h = h_ref[...]
        for j in range(N_FF_TILES):
            dn_copy(e, j, j % 2).wait()
            if j + 1 < N_FF_TILES:
                dn_copy(e, j + 1, (j + 1) % 2).start()
            else:
                @pl.when(b + 1 < nv)
                def _():
                    for cp in gu_copies(be_ref[b + 1], 0, 0):
                        cp.start()
            if j == 0:
                unpack_rows(1 - slot)
            cols = slice(j * FF_TILE, (j + 1) * FF_TILE)
            y_ref[:, cols] = _dot(h, dn_ref[j % 2]) + bd_ref[0, :, cols]

        @pl.when(b + 1 == nv)
        def _():
            def issue(r, carry):
                out_copy(dst_ref, r).start()
                return carry
            lax.fori_loop(0, tm, issue, 0, unroll=8)
            out_wait()


def experts(block_expert, n_valid, slot_tok, slot_dst, hf_packed, wg, bg, wu, bu, wd, bd, n_out_rows):
    n_blocks = block_expert.shape[0]
    tm = MOE_ROWS
    d = D_MODEL

    def blk(b, nv):
        return jnp.minimum(b, nv[0] - 1)

    any_spec = pl.BlockSpec(memory_space=pl.ANY)
    grid_spec = pltpu.PrefetchScalarGridSpec(
        num_scalar_prefetch=2,
        grid=(n_blocks,),
        in_specs=[
            pl.BlockSpec((1, 1, tm), lambda b, be, nv: (blk(b, nv), 0, 0), memory_space=pltpu.SMEM),
            pl.BlockSpec((1, 1, tm), lambda b, be, nv: (blk(b + 1, nv), 0, 0), memory_space=pltpu.SMEM),
            pl.BlockSpec((1, 1, tm), lambda b, be, nv: (blk(b, nv) + 1, 0, 0), memory_space=pltpu.SMEM),
            pl.BlockSpec((1, 1, tm), lambda b, be, nv: (blk(b, nv), 0, 0), memory_space=pltpu.SMEM),
            any_spec, any_spec, any_spec, any_spec,
            pl.BlockSpec((1, 1, D_FF), lambda b, be, nv: (be[blk(b, nv)], 0, 0)),
            pl.BlockSpec((1, 1, D_FF), lambda b, be, nv: (be[blk(b, nv)], 0, 0)),
            pl.BlockSpec((1, 1, d), lambda b, be, nv: (be[blk(b, nv)], 0, 0)),
        ],
        out_specs=any_spec,
        scratch_shapes=[
            pltpu.VMEM((tm, d // 2), jnp.uint32),
            pltpu.VMEM((2, tm, d), BF16),
            pltpu.VMEM((tm, D_FF), BF16),
            pltpu.VMEM((tm, d), F32),
            pltpu.VMEM((2, 2, d, FF_TILE), BF16),
            pltpu.VMEM((2, D_FF, FF_TILE), BF16),
            pltpu.SemaphoreType.DMA,
            pltpu.SemaphoreType.DMA,
            pltpu.SemaphoreType.DMA((2,)),
            pltpu.SemaphoreType.DMA((2,)),
        ],
    )
    return pl.pallas_call(
        _expert_body,
        grid_spec=grid_spec,
        out_shape=jax.ShapeDtypeStruct((n_out_rows, d), F32),
        compiler_params=_params("arbitrary"),
        name="experts",
    )(block_expert, n_valid, slot_tok, slot_tok, slot_dst, slot_dst, hf_packed, wg, wu, wd, bg, bu, bd)


COMBINE_ROWS = 256


def _combine_body(h_ref, gate_ref, yk_ref, o_ref):
    tm = h_ref.shape[0]
    gates = gate_ref[...]
    acc = h_ref[...]
    for k in range(TOP_K):
        acc = acc + gates[:, k:k + 1] * yk_ref[k * tm:(k + 1) * tm, :]
    o_ref[...] = acc


def combine(h, gates, yk):
    t, d = h.shape
    tm = min(COMBINE_ROWS, t)
    return pl.pallas_call(
        _combine_body,
        grid=(t // tm,),
        in_specs=[
            pl.BlockSpec((tm, d), lambda i: (i, 0)),
            pl.BlockSpec((tm, LANES), lambda i: (i, 0)),
            pl.BlockSpec((TOP_K * tm, d), lambda i: (i, 0)),
        ],
        out_specs=pl.BlockSpec((tm, d), lambda i: (i, 0)),
        out_shape=jax.ShapeDtypeStruct((t, d), F32),
        compiler_params=_params("arbitrary"),
        name="combine",
    )(h, gates, yk)


def routing_tables(top_idx, t):
    n_assign = t * TOP_K
    tm_c = min(COMBINE_ROWS, t)
    e_flat = top_idx.reshape(-1).astype(jnp.int32)
    _, order = lax.sort((e_flat, jnp.arange(n_assign, dtype=jnp.int32)), num_keys=1, is_stable=True)
    experts_iota = jnp.arange(N_EXPERTS, dtype=jnp.int32)
    counts = jnp.sum((e_flat[None, :] == experts_iota[:, None]).astype(jnp.int32), axis=1)
    starts = jnp.cumsum(counts) - counts
    padded = (counts + MOE_ROWS - 1) // MOE_ROWS * MOE_ROWS
    pends = jnp.cumsum(padded)
    pstarts = pends - padded
    n_blocks = -(-n_assign // MOE_ROWS) + N_EXPERTS
    block_start = jnp.arange(n_blocks, dtype=jnp.int32) * MOE_ROWS
    block_expert = jnp.minimum(
        jnp.sum((pends[None, :] <= block_start[:, None]).astype(jnp.int32), axis=1), N_EXPERTS - 1)
    n_valid = (pends[-1:] // MOE_ROWS).astype(jnp.int32)

    row = jnp.arange(MOE_ROWS, dtype=jnp.int32)[None, :]
    e_b = block_expert[:, None]
    local = jnp.arange(n_blocks, dtype=jnp.int32)[:, None] * MOE_ROWS + row - pstarts[e_b]
    used = (local >= 0) & (local < counts[e_b])
    asg = order[jnp.clip(starts[e_b] + local, 0, n_assign - 1)]
    tok = asg // TOP_K
    dst = (tok // tm_c) * (TOP_K * tm_c) + (asg % TOP_K) * tm_c + tok % tm_c
    slot_tok = jnp.where(used, tok, 0).reshape(n_blocks, 1, MOE_ROWS)
    slot_dst = jnp.concatenate([n_assign + row, jnp.where(used, dst, n_assign + row)], axis=0)
    return block_expert, n_valid, slot_tok, slot_dst.reshape(n_blocks + 1, 1, MOE_ROWS)


def _pad_lanes(v, width, value=0.0):
    return jnp.pad(v, ((0, 0), (0, width - v.shape[1])), constant_values=value)


def hybrid_layer(x, attn_norm_w, w_in, q_norm_w, k_norm_w, attn_sinks, conv_w, conv_b, dt_bias, a_log, d_skip,
                 ssm_norm_w, attn_out_norm_w, w_out, ffn_norm_w, w_router, b_router, w_gate, b_gate, w_up, b_up,
                 w_down, b_down):
    batch, seq, d = x.shape
    t = batch * seq
    x2d = x.reshape(t, d)

    q_w, k_w, v_w, z_w, xbc_w, dt_w = jnp.split(
        w_in, [ATTN_WIDTH, ATTN_WIDTH + KV_WIDTH, ATTN_WIDTH + 2 * KV_WIDTH,
               ATTN_WIDTH + 2 * KV_WIDTH + SSM_WIDTH, ATTN_WIDTH + 2 * KV_WIDTH + SSM_WIDTH + CONV_WIDTH], axis=1)
    w_p = jnp.concatenate([q_w, z_w, xbc_w, k_w, v_w, dt_w], axis=1)
    w_p = _pad_lanes(w_p, PROJ_WIDTH).astype(BF16)
    proj = in_proj(x2d, attn_norm_w.reshape(1, d), w_p)

    qw2 = (jnp.tile(q_norm_w, 2) * (HEAD_DIM ** -0.5)).reshape(1, LANES)
    kw2 = jnp.tile(k_norm_w, 2).reshape(1, LANES)
    attn, (wg16, wu16) = attention(
        proj, attn_sinks, qw2, kw2, attn_out_norm_w.reshape(1, ATTN_WIDTH), batch, seq,
        side=(w_gate.reshape(N_EXPERTS * D_MODEL, D_FF), w_up.reshape(N_EXPERTS * D_MODEL, D_FF)))

    ssm, (wd16,) = ssd(
        proj,
        conv_w[:, :SSM_WIDTH], conv_w[:, SSM_WIDTH:],
        conv_b[:SSM_WIDTH].reshape(1, -1), conv_b[SSM_WIDTH:].reshape(1, -1),
        _pad_lanes(dt_bias.reshape(1, -1), LANES), _pad_lanes(a_log.reshape(1, -1), LANES),
        jnp.repeat(d_skip, HEAD_DIM).reshape(1, SSM_WIDTH), ssm_norm_w.reshape(1, SSM_WIDTH),
        batch, seq, side=(w_down.reshape(N_EXPERTS * D_FF, D_MODEL),))

    h, hf_packed, idx_pad, gate_pad = out_proj_router(
        x2d, attn, ssm, w_out.astype(BF16), ffn_norm_w.reshape(1, d),
        _pad_lanes(w_router, LANES), _pad_lanes(b_router.reshape(1, -1), LANES, NEG_BIG))

    block_expert, n_valid, slot_tok, slot_dst = routing_tables(idx_pad[:, :TOP_K], t)
    yk = experts(
        block_expert, n_valid, slot_tok, slot_dst, hf_packed,
        wg16.reshape(N_EXPERTS, D_MODEL, D_FF), b_gate.reshape(N_EXPERTS, 1, D_FF),
        wu16.reshape(N_EXPERTS, D_MODEL, D_FF), b_up.reshape(N_EXPERTS, 1, D_FF),
        wd16.reshape(N_EXPERTS, D_FF, D_MODEL), b_down.reshape(N_EXPERTS, 1, D_MODEL),
        t * TOP_K + MOE_ROWS)

    out = combine(h, gate_pad, yk)
    return out.reshape(batch, seq, d)


def kernel(x, attn_norm_w, w_in, q_norm_w, k_norm_w, attn_sinks, conv_w, conv_b, dt_bias, a_log, d_skip, ssm_norm_w,
           attn_out_norm_w, w_out, ffn_norm_w, w_router, b_router, w_gate, b_gate, w_up, b_up, w_down, b_down):
    for i in range(attn_norm_w.shape[0]):
        x = hybrid_layer(x, attn_norm_w[i], w_in[i], q_norm_w[i], k_norm_w[i], attn_sinks[i], conv_w[i], conv_b[i],
                         dt_bias[i], a_log[i], d_skip[i], ssm_norm_w[i], attn_out_norm_w[i], w_out[i], ffn_norm_w[i],
                         w_router[i], b_router[i], w_gate[i], b_gate[i], w_up[i], b_up[i], w_down[i], b_down[i])
    return x
```

```python
import functools

import jax
import jax.numpy as jnp
from jax import lax
from jax.experimental import pallas as pl
from jax.experimental.pallas import tpu as pltpu

F32 = jnp.float32
BF16 = jnp.bfloat16

D_MODEL = 2048
HEAD_DIM = 64
ATTN_WIDTH = 1024
KV_WIDTH = 128
WINDOW = 128
SSM_WIDTH = 1024
SSM_HEADS = 16
SSM_GROUPS = 2
SSM_STATE = 128
SSM_CONV = 4
SSM_CHUNK = 128
BC_WIDTH = SSM_GROUPS * SSM_STATE
CONV_WIDTH = SSM_WIDTH + 2 * BC_WIDTH
N_EXPERTS = 32
TOP_K = 4
D_FF = 2048
SWIGLU_LIMIT = 7.0
SWIGLU_ALPHA = 1.702
EPS = 1e-6

LANES = 128
SUBLANES = 8
VMEM_LIMIT = 56 * 1024 * 1024

PROJ_WIDTH = 4096
COL_Q = 0
COL_Z = ATTN_WIDTH
COL_XBC = COL_Z + SSM_WIDTH
COL_K = COL_XBC + CONV_WIDTH
COL_V = COL_K + KV_WIDTH
COL_DT = COL_V + KV_WIDTH
NEG_BIG = -1e30


def _params(*sem):
    return pltpu.CompilerParams(dimension_semantics=sem, vmem_limit_bytes=VMEM_LIMIT)


def _split2(v):
    hi = v.astype(BF16)
    lo = (v - hi.astype(F32)).astype(BF16)
    return hi, lo


def _split3(v):
    hi = v.astype(BF16)
    r = v - hi.astype(F32)
    mid = r.astype(BF16)
    lo = (r - mid.astype(F32)).astype(BF16)
    return hi, mid, lo


def _dot(a, b):
    return jnp.dot(a, b, preferred_element_type=F32)


def _dot_exact_lhs(parts, m):
    acc = _dot(parts[0], m)
    for p in parts[1:]:
        acc = acc + _dot(p, m)
    return acc


def _sigmoid(v):
    return 1.0 / (1.0 + jnp.exp(-v))


def _iota(shape, dim):
    return lax.broadcasted_iota(jnp.int32, shape, dim)


def _in_proj_body(x0_ref, xn_ref, g_ref, w_ref, o_ref, hn_ref):
    i = pl.program_id(0)

    def norm(x):
        ms = jnp.mean(x * x, axis=-1, keepdims=True)
        return (x * lax.rsqrt(ms + EPS) * g_ref[...]).astype(BF16)

    @pl.when(i == 0)
    def _():
        hn_ref[0] = norm(x0_ref[...])

    slot = i % 2
    hn_ref[1 - slot] = norm(xn_ref[...])
    o_ref[...] = _dot(hn_ref[slot], w_ref[...])


def in_proj(x2d, g, w_p):
    t, d = x2d.shape
    n = w_p.shape[1]
    tm = min(512, t)
    ni = t // tm
    once = pl.Buffered(1)
    return pl.pallas_call(
        _in_proj_body,
        grid=(ni,),
        in_specs=[
            pl.BlockSpec((tm, d), lambda i: (0, 0), pipeline_mode=once),
            pl.BlockSpec((tm, d), lambda i: (jnp.minimum(i + 1, ni - 1), 0)),
            pl.BlockSpec((1, d), lambda i: (0, 0), pipeline_mode=once),
            pl.BlockSpec((d, n), lambda i: (0, 0), pipeline_mode=once),
        ],
        out_specs=pl.BlockSpec((tm, n), lambda i: (i, 0)),
        out_shape=jax.ShapeDtypeStruct((t, n), F32),
        scratch_shapes=[pltpu.VMEM((2, tm, d), BF16)],
        compiler_params=_params("arbitrary"),
        name="in_proj",
    )(x2d, x2d, g, w_p)


def _head_pair_norm(v, w, blockdiag):
    ss = _dot_exact_lhs(_split2(v * v), blockdiag)
    return v * lax.rsqrt(ss * (1.0 / HEAD_DIM) + EPS) * w


def _cast_side_streams(refs, n_side):
    for k in range(n_side):
        refs[n_side + 1 + k][...] = refs[k][...].astype(BF16)
    return refs[n_side]


def _side_specs(side, n_steps, step_index):
    specs, shapes = [], []
    for w in side:
        rows, cols = w.shape
        specs.append(pl.BlockSpec((rows // n_steps, cols), step_index))
        shapes.append(jax.ShapeDtypeStruct((rows, cols), BF16))
    return specs, shapes


def _attn_body(n_side, sinks_ref, q_ref, kvc_ref, kvp_ref, qw_ref, kw_ref, onw_ref, *rest):
    o_ref = _cast_side_streams(rest, n_side)
    acc_ref = rest[-1]
    i = pl.program_id(1)
    tq = q_ref.shape[0]
    nsub = tq // WINDOW
    nk = tq + WINDOW
    lo_lanes = _iota((1, LANES), 1) < HEAD_DIM
    blockdiag = jnp.where(
        _iota((LANES, LANES), 0) // HEAD_DIM == _iota((LANES, LANES), 1) // HEAD_DIM, 1.0, 0.0
    ).astype(BF16)

    kv_all = jnp.concatenate([kvp_ref[...], kvc_ref[...]], axis=0)
    k_all = _head_pair_norm(kv_all[:, :KV_WIDTH], kw_ref[...], blockdiag)
    v_all = kv_all[:, KV_WIDTH:]
    kt = k_all.T
    zero_half = jnp.zeros((HEAD_DIM, nk), F32)
    kt_var = [
        [jnp.concatenate([kt[j * HEAD_DIM:(j + 1) * HEAD_DIM], zero_half], axis=0).astype(BF16),
         jnp.concatenate([zero_half, kt[j * HEAD_DIM:(j + 1) * HEAD_DIM]], axis=0).astype(BF16)]
        for j in range(2)
    ]
    v_sw = pltpu.roll(v_all, HEAD_DIM, axis=1)
    v_var = [
        [jnp.where(lo_lanes, v_all, 0.0).astype(BF16), jnp.where(lo_lanes, 0.0, v_sw).astype(BF16)],
        [jnp.where(lo_lanes, v_sw, 0.0).astype(BF16), jnp.where(lo_lanes, 0.0, v_all).astype(BF16)],
    ]

    qn = [
        _head_pair_norm(q_ref[:, c * LANES:(c + 1) * LANES], qw_ref[...], blockdiag).astype(BF16)
        for c in range(ATTN_WIDTH // LANES)
    ]

    rows = 4 * WINDOW
    qi = _iota((rows, 2 * WINDOW), 0) % WINDOW
    kj = _iota((rows, 2 * WINDOW), 1)
    band = (kj > qi) & (kj <= qi + WINDOW)
    first_min = jnp.where(i == 0, WINDOW, 0)
    band_first = band & (kj >= first_min)

    for s in range(nsub):
        mask = band_first if s == 0 else band
        for j in range(2):
            qs = jnp.concatenate([qn[4 * j + c][s * WINDOW:(s + 1) * WINDOW] for c in range(4)], axis=0)
            out = None
            for par in range(2):
                sink = jnp.concatenate(
                    [jnp.full((WINDOW, 1), sinks_ref[8 * j + 2 * c + par], F32) for c in range(4)], axis=0)
                sc = _dot(qs, kt_var[j][par][:, s * WINDOW:s * WINDOW + 2 * WINDOW])
                sc = jnp.where(mask, sc, NEG_BIG)
                m = jnp.maximum(jnp.max(sc, axis=-1, keepdims=True), sink)
                p = jnp.exp(sc - m)
                den = jnp.sum(p, axis=-1, keepdims=True) + jnp.exp(sink - m)
                pv = _dot(p.astype(BF16), v_var[j][par][s * WINDOW:s * WINDOW + 2 * WINDOW])
                pv = pv * (1.0 / den)
                out = pv if out is None else out + pv
            for c in range(4):
                acc_ref[s * WINDOW:(s + 1) * WINDOW, (4 * j + c) * LANES:(4 * j + c + 1) * LANES] = (
                    out[c * WINDOW:(c + 1) * WINDOW])

    a = acc_ref[...]
    ms = jnp.mean(a * a, axis=-1, keepdims=True)
    o_ref[...] = (a * lax.rsqrt(ms + EPS) * onw_ref[...]).astype(BF16)


def attention(proj, sinks, qw2, kw2, onw, batch, seq, side=()):
    t = proj.shape[0]
    tq = min(256, seq)
    nq = seq // tq
    sub = tq // WINDOW
    kv_col = COL_K // (2 * KV_WIDTH)
    side_specs, side_shapes = _side_specs(side, batch * nq, lambda b, i, s: (b * nq + i, 0))
    grid_spec = pltpu.PrefetchScalarGridSpec(
        num_scalar_prefetch=1,
        grid=(batch, nq),
        in_specs=[
            pl.BlockSpec((tq, ATTN_WIDTH), lambda b, i, s: (b * nq + i, COL_Q // ATTN_WIDTH)),
            pl.BlockSpec((tq, 2 * KV_WIDTH), lambda b, i, s: (b * nq + i, kv_col)),
            pl.BlockSpec((WINDOW, 2 * KV_WIDTH),
                         lambda b, i, s: (b * nq * sub + jnp.maximum(i * sub - 1, 0), kv_col)),
            pl.BlockSpec((1, LANES), lambda b, i, s: (0, 0)),
            pl.BlockSpec((1, LANES), lambda b, i, s: (0, 0)),
            pl.BlockSpec((1, ATTN_WIDTH), lambda b, i, s: (0, 0)),
        ] + side_specs,
        out_specs=[pl.BlockSpec((tq, ATTN_WIDTH), lambda b, i, s: (b * nq + i, 0))] + side_specs,
        scratch_shapes=[pltpu.VMEM((tq, ATTN_WIDTH), F32)],
    )
    outs = pl.pallas_call(
        functools.partial(_attn_body, len(side)),
        grid_spec=grid_spec,
        out_shape=[jax.ShapeDtypeStruct((t, ATTN_WIDTH), BF16)] + side_shapes,
        compiler_params=_params("arbitrary", "arbitrary"),
        name="attention",
    )(sinks, proj, proj, proj, qw2, kw2, onw, *side)
    return outs[0], outs[1:]


def _ssd_body(n_side, z_ref, xs_ref, bc_ref, dt_ref, cwx_ref, cwb_ref, cbx_ref, cbb_ref, dtb_ref, alog_ref,
              dsk_ref, nw_ref, *rest):
    o_ref = _cast_side_streams(rest, n_side)
    xpx_ref, xpb_ref, st_ref, y_ref = rest[-4:]
    i = pl.program_id(1)
    ts = xs_ref.shape[0]
    nch = ts // SSM_CHUNK
    gw = SSM_WIDTH // SSM_GROUPS
    pad = SUBLANES

    @pl.when(i == 0)
    def _():
        xpx_ref[0:pad, :] = jnp.zeros((pad, SSM_WIDTH), F32)
        xpb_ref[0:pad, :] = jnp.zeros((pad, 2 * BC_WIDTH), F32)
        st_ref[...] = jnp.zeros(st_ref.shape, F32)

    xpx_ref[pad:pad + ts, :] = xs_ref[...]
    xpb_ref[pad:pad + ts, :] = bc_ref[...]

    def conv_silu(xp_ref, w_ref, b_ref):
        acc = b_ref[...] + w_ref[SSM_CONV - 1:SSM_CONV, :] * xp_ref[pad:pad + ts, :]
        for k in range(SSM_CONV - 1):
            off = pad - (SSM_CONV - 1) + k
            acc = acc + w_ref[k:k + 1, :] * xp_ref[off:off + ts, :]
        return acc * _sigmoid(acc)

    xc = conv_silu(xpx_ref, cwx_ref, cbx_ref)
    bcc = conv_silu(xpb_ref, cwb_ref, cbb_ref)
    xpx_ref[0:pad, :] = xs_ref[ts - pad:ts, :]
    xpb_ref[0:pad, :] = bc_ref[ts - pad:ts, :]

    dtr = dt_ref[...] + dtb_ref[...]
    dt = jnp.maximum(dtr, 0.0) + jnp.log(1.0 + jnp.exp(-jnp.abs(dtr)))
    a = -jnp.exp(alog_ref[...])
    da = dt * a

    ri = _iota((ts, ts), 0)
    ci = _iota((ts, ts), 1)
    same = (ri // SSM_CHUNK) == (ci // SSM_CHUNK)
    tri_blk = jnp.where(same & (ci <= ri), 1.0, 0.0).astype(BF16)
    all_blk = jnp.where(same, 1.0, 0.0).astype(BF16)
    da3 = _split3(da)
    cs = _dot(tri_blk, da3[0]) + _dot(tri_blk, da3[1]) + _dot(tri_blk, da3[2])
    cl = _dot(all_blk, da3[0]) + _dot(all_blk, da3[1]) + _dot(all_blk, da3[2])
    ecs = jnp.exp(cs)
    dte = jnp.exp(cl - cs)

    expand = jnp.where(_iota((LANES, SSM_WIDTH), 1) // HEAD_DIM == _iota((LANES, SSM_WIDTH), 0),
                       1.0, 0.0).astype(BF16)
    dt_e = _dot_exact_lhs(_split2(dt), expand)
    dtd_e = _dot_exact_lhs(_split2(dt * dte), expand)
    ecs_e = _dot_exact_lhs(_split2(ecs), expand)
    xf = xc * dt_e
    xd = (xc * dtd_e).astype(BF16)

    lo_lanes = _iota((1, LANES), 1) < HEAD_DIM
    li = _iota((SSM_CHUNK, SSM_CHUNK), 0)
    lj = _iota((SSM_CHUNK, SSM_CHUNK), 1)
    tril = lj <= li
    upper = jnp.where(li <= lj, 1.0, 0.0).astype(BF16)

    for c in range(nch):
        r0 = c * SSM_CHUNK
        da_t = da[r0:r0 + SSM_CHUNK].T
        cs_t = _dot_exact_lhs(_split3(da_t), upper)
        cs_c = cs[r0:r0 + SSM_CHUNK]
        for g in range(SSM_GROUPS):
            b_c = bcc[r0:r0 + SSM_CHUNK, g * SSM_STATE:(g + 1) * SSM_STATE]
            c_c = bcc[r0:r0 + SSM_CHUNK, BC_WIDTH + g * SSM_STATE:BC_WIDTH + (g + 1) * SSM_STATE].astype(BF16)
            b_t = b_c.T.astype(BF16)
            cb = _dot(c_c, b_t)
            st = st_ref[g]
            y_off = _dot(c_c, st.astype(BF16)) * ecs_e[r0:r0 + SSM_CHUNK, g * gw:(g + 1) * gw]
            s_new = _dot(b_t, xd[r0:r0 + SSM_CHUNK, g * gw:(g + 1) * gw])
            parts = []
            for p in range(4):
                h0 = 8 * g + 2 * p
                ms = []
                for h in (h0, h0 + 1):
                    diff = cs_c[:, h:h + 1] - cs_t[h:h + 1, :]
                    ms.append((cb * jnp.exp(jnp.where(tril, diff, NEG_BIG))).astype(BF16))
                m2 = jnp.concatenate(ms, axis=1)
                x2 = xf[r0:r0 + SSM_CHUNK, h0 * HEAD_DIM:(h0 + 2) * HEAD_DIM]
                xbd = jnp.concatenate([jnp.where(lo_lanes, x2, 0.0), jnp.where(lo_lanes, 0.0, x2)],
                                      axis=0).astype(BF16)
                parts.append(_dot(m2, xbd))
            y_ref[r0:r0 + SSM_CHUNK, g * gw:(g + 1) * gw] = jnp.concatenate(parts, axis=1) + y_off
            dec = ecs_e[r0 + SSM_CHUNK - 1:r0 + SSM_CHUNK, g * gw:(g + 1) * gw]
            st_ref[g] = st * dec + s_new

    y = y_ref[...] + xc * dsk_ref[...]
    z = z_ref[...]
    gated = y * (z * _sigmoid(z))
    outs = []
    for g in range(SSM_GROUPS):
        gg = gated[:, g * gw:(g + 1) * gw]
        ms = jnp.mean(gg * gg, axis=-1, keepdims=True)
        outs.append(gg * lax.rsqrt(ms + EPS) * nw_ref[:, g * gw:(g + 1) * gw])
    o_ref[...] = jnp.concatenate(outs, axis=1).astype(BF16)


def ssd(proj, cwx, cwb, cbx, cbb, dtb, alog, dsk, nw, batch, seq, side=()):
    t = proj.shape[0]
    ts = min(256, seq)
    ns = seq // ts
    row = lambda b, i: b * ns + i
    const = lambda b, i: (0, 0)
    side_specs, side_shapes = _side_specs(side, batch * ns, lambda b, i: (row(b, i), 0))
    outs = pl.pallas_call(
        functools.partial(_ssd_body, len(side)),
        grid=(batch, ns),
        in_specs=[
            pl.BlockSpec((ts, SSM_WIDTH), lambda b, i: (row(b, i), COL_Z // SSM_WIDTH)),
            pl.BlockSpec((ts, SSM_WIDTH), lambda b, i: (row(b, i), COL_XBC // SSM_WIDTH)),
            pl.BlockSpec((ts, 2 * BC_WIDTH), lambda b, i: (row(b, i), (COL_XBC + SSM_WIDTH) // (2 * BC_WIDTH))),
            pl.BlockSpec((ts, LANES), lambda b, i: (row(b, i), COL_DT // LANES)),
            pl.BlockSpec((SSM_CONV, SSM_WIDTH), const),
            pl.BlockSpec((SSM_CONV, 2 * BC_WIDTH), const),
            pl.BlockSpec((1, SSM_WIDTH), const),
            pl.BlockSpec((1, 2 * BC_WIDTH), const),
            pl.BlockSpec((1, LANES), const),
            pl.BlockSpec((1, LANES), const),
            pl.BlockSpec((1, SSM_WIDTH), const),
            pl.BlockSpec((1, SSM_WIDTH), const),
        ] + side_specs,
        out_specs=[pl.BlockSpec((ts, SSM_WIDTH), lambda b, i: (row(b, i), 0))] + side_specs,
        out_shape=[jax.ShapeDtypeStruct((t, SSM_WIDTH), BF16)] + side_shapes,
        scratch_shapes=[
            pltpu.VMEM((ts + SUBLANES, SSM_WIDTH), F32),
            pltpu.VMEM((ts + SUBLANES, 2 * BC_WIDTH), F32),
            pltpu.VMEM((SSM_GROUPS, SSM_STATE, SSM_WIDTH // SSM_GROUPS), F32),
            pltpu.VMEM((ts, SSM_WIDTH), F32),
        ],
        compiler_params=_params("arbitrary", "arbitrary"),
        name="ssd",
    )(proj, proj, proj, proj, cwx, cwb, cbx, cbb, dtb, alog, dsk, nw, *side)
    return outs[0], outs[1:]


def _out_body(x_ref, a_ref, s_ref, wo_ref, fg_ref, wr_ref, br_ref, h_ref, hf_ref, idx_ref, gate_ref):
    tm = x_ref.shape[0]
    h = x_ref[...] + _dot(a_ref[...], wo_ref[0:ATTN_WIDTH, :]) + _dot(s_ref[...], wo_ref[ATTN_WIDTH:, :])
    h_ref[...] = h
    ms = jnp.mean(h * h, axis=-1, keepdims=True)
    hf = h * lax.rsqrt(ms + EPS) * fg_ref[...]
    hi, lo = _split2(hf)
    hi32 = hi.astype(F32)
    hf_ref[...] = (pltpu.bitcast(hi32[:, D_MODEL // 2:], jnp.uint32)
                   | (pltpu.bitcast(hi32[:, :D_MODEL // 2], jnp.uint32) >> 16))

    whi, wlo = _split2(wr_ref[...])
    hw = _dot(hi, jnp.concatenate([whi, wlo], axis=1))
    logits = hw[:, :LANES] + hw[:, LANES:] + _dot(lo, whi) + br_ref[...]
    lane = _iota((tm, LANES), 1)
    vals, idxs = [], []
    for _ in range(TOP_K):
        m = jnp.max(logits, axis=-1, keepdims=True)
        am = jnp.min(jnp.where(logits == m, lane, LANES), axis=-1, keepdims=True)
        vals.append(m)
        idxs.append(am)
        logits = jnp.where(lane == am, NEG_BIG * 2.0, logits)
    es = [jnp.exp(v - vals[0]) for v in vals]
    tot = es[0] + es[1] + es[2] + es[3]
    idx_out = jnp.zeros((tm, LANES), jnp.int32)
    gate_out = jnp.zeros((tm, LANES), F32)
    for k in range(TOP_K):
        idx_out = jnp.where(lane == k, idxs[k], idx_out)
        gate_out = jnp.where(lane == k, es[k] / tot, gate_out)
    idx_ref[...] = idx_out
    gate_ref[...] = gate_out


def out_proj_router(x2d, attn, ssm, wo, fg, wr, br):
    t, d = x2d.shape
    tm = min(512, t)
    const = lambda i: (0, 0)
    once = pl.Buffered(1)
    return pl.pallas_call(
        _out_body,
        grid=(t // tm,),
        in_specs=[
            pl.BlockSpec((tm, d), lambda i: (i, 0)),
            pl.BlockSpec((tm, ATTN_WIDTH), lambda i: (i, 0)),
            pl.BlockSpec((tm, SSM_WIDTH), lambda i: (i, 0)),
            pl.BlockSpec((d, d), const, pipeline_mode=once),
            pl.BlockSpec((1, d), const, pipeline_mode=once),
            pl.BlockSpec((d, LANES), const, pipeline_mode=once),
            pl.BlockSpec((1, LANES), const, pipeline_mode=once),
        ],
        out_specs=[
            pl.BlockSpec((tm, d), lambda i: (i, 0)),
            pl.BlockSpec((tm, d // 2), lambda i: (i, 0)),
            pl.BlockSpec((tm, LANES), lambda i: (i, 0)),
            pl.BlockSpec((tm, LANES), lambda i: (i, 0)),
        ],
        out_shape=[
            jax.ShapeDtypeStruct((t, d), F32),
            jax.ShapeDtypeStruct((t, d // 2), jnp.uint32),
            jax.ShapeDtypeStruct((t, LANES), jnp.int32),
            jax.ShapeDtypeStruct((t, LANES), F32),
        ],
        compiler_params=_params("arbitrary"),
        name="out_proj_router",
    )(x2d, attn, ssm, wo, fg, wr, br)


MOE_ROWS = 512
FF_TILE = 1024
N_FF_TILES = D_FF // FF_TILE


def _expert_body(be_ref, nv_ref, tok_ref, tokn_ref, dst_ref, dstp_ref, hf_ref, wg_ref, wu_ref, wd_ref, bg_ref, bu_ref,
                 bd_ref, yk_ref, xrow_ref, x_ref, h_ref, y_ref, gu_ref, dn_ref, row_sem, out_sem, gu_sem, dn_sem):
    b = pl.program_id(0)
    nv = nv_ref[0]
    tm = h_ref.shape[0]
    half = D_MODEL // 2

    def row_copy(table_ref, r):
        return pltpu.make_async_copy(hf_ref.at[pl.ds(table_ref[0, 0, r], 1)], xrow_ref.at[pl.ds(r, 1)], row_sem)

    def rows_wait():
        pltpu.make_async_copy(hf_ref.at[pl.ds(0, tm)], xrow_ref, row_sem).wait()

    def unpack_rows(slot):
        w = xrow_ref[...]
        x_ref[slot, :, 0:half] = pltpu.bitcast(w << 16, F32).astype(BF16)
        x_ref[slot, :, half:] = pltpu.bitcast(w & jnp.uint32(0xFFFF0000), F32).astype(BF16)

    def out_copy(table_ref, r):
        return pltpu.make_async_copy(y_ref.at[pl.ds(r, 1)], yk_ref.at[pl.ds(table_ref[0, 0, r], 1)], out_sem)

    def out_wait():
        pltpu.make_async_copy(y_ref, yk_ref.at[pl.ds(0, tm)], out_sem).wait()

    n_chunks = 2 * N_FF_TILES

    def chunk_copies(e, c):
        j = c % N_FF_TILES
        cols = pl.ds(j * FF_TILE, FF_TILE)
        if c < N_FF_TILES:
            return (pltpu.make_async_copy(wg_ref.at[e, :, cols], gu_ref.at[j, 0], gu_sem.at[j]),
                    pltpu.make_async_copy(wu_ref.at[e, :, cols], gu_ref.at[j, 1], gu_sem.at[j]))
        return (pltpu.make_async_copy(wd_ref.at[e, :, cols], dn_ref.at[j], dn_sem.at[j]),)

    def start_chunk(e, c):
        for cp in chunk_copies(e, c):
            cp.start()

    def next_chunk(e, c):
        for cp in chunk_copies(e, c):
            cp.wait()
        if c == 0:
            start_chunk(e, n_chunks - 1)
        else:
            @pl.when(b + 1 < nv)
            def _():
                start_chunk(be_ref[b + 1], c - 1)

    per_batch = tm // N_FF_TILES

    @pl.when(b < nv)
    def _():
        e = be_ref[b]
        slot = b % 2

        @pl.when(b == 0)
        def _():
            def issue(r, carry):
                row_copy(tok_ref, r).start()
                return carry
            lax.fori_loop(0, tm, issue, 0, unroll=8)
            for c in range(n_chunks - 1):
                start_chunk(e, c)
            y_ref[...] = jnp.zeros(y_ref.shape, F32)
            rows_wait()
            unpack_rows(0)

        x = x_ref[slot]
        for j in range(N_FF_TILES):
            next_chunk(e, j)
            cols = slice(j * FF_TILE, (j + 1) * FF_TILE)
            g_acc = _dot(x, gu_ref[j, 0])
            u_acc = _dot(x, gu_ref[j, 1])
            for r in range(j * per_batch, (j + 1) * per_batch):
                row_copy(tokn_ref, r).start()
                out_copy(dstp_ref, r).start()
            gl = jnp.minimum(g_acc + bg_ref[0, :, cols], SWIGLU_LIMIT)
            li = jnp.clip(u_acc + bu_ref[0, :, cols], -SWIGLU_LIMIT, SWIGLU_LIMIT)
            h_ref[:, cols] = (gl * _sigmoid(SWIGLU_ALPHA * gl) * (li + 1.0)).astype(BF16)

        out_wait()
        rows_wait()

        h = h_ref[...]
        for j in range(N_FF_TILES):
            next_chunk(e, N_FF_TILES + j)
            if j == 0:
                unpack_rows(1 - slot)
            cols = slice(j * FF_TILE, (j + 1) * FF_TILE)
            y_ref[:, cols] = _dot(h, dn_ref[j]) + bd_ref[0, :, cols]

        @pl.when(b + 1 == nv)
        def _():
            def issue(r, carry):
                out_copy(dst_ref, r).start()
                return carry
            lax.fori_loop(0, tm, issue, 0, unroll=8)
            out_wait()


def experts(block_expert, n_valid, slot_tok, slot_dst, hf_packed, wg, bg, wu, bu, wd, bd, n_out_rows):
    n_blocks = block_expert.shape[0]
    tm = MOE_ROWS
    d = D_MODEL

    def blk(b, nv):
        return jnp.minimum(b, nv[0] - 1)

    any_spec = pl.BlockSpec(memory_space=pl.ANY)
    grid_spec = pltpu.PrefetchScalarGridSpec(
        num_scalar_prefetch=2,
        grid=(n_blocks,),
        in_specs=[
            pl.BlockSpec((1, 1, tm), lambda b, be, nv: (blk(b, nv), 0, 0), memory_space=pltpu.SMEM),
            pl.BlockSpec((1, 1, tm), lambda b, be, nv: (blk(b + 1, nv), 0, 0), memory_space=pltpu.SMEM),
            pl.BlockSpec((1, 1, tm), lambda b, be, nv: (blk(b, nv) + 1, 0, 0), memory_space=pltpu.SMEM),
            pl.BlockSpec((1, 1, tm), lambda b, be, nv: (blk(b, nv), 0, 0), memory_space=pltpu.SMEM),
            any_spec, any_spec, any_spec, any_spec,
            pl.BlockSpec((1, 1, D_FF), lambda b, be, nv: (be[blk(b, nv)], 0, 0)),
            pl.BlockSpec((1, 1, D_FF), lambda b, be, nv: (be[blk(b, nv)], 0, 0)),
            pl.BlockSpec((1, 1, d), lambda b, be, nv: (be[blk(b, nv)], 0, 0)),
        ],
        out_specs=any_spec,
        scratch_shapes=[
            pltpu.VMEM((tm, d // 2), jnp.uint32),
            pltpu.VMEM((2, tm, d), BF16),
            pltpu.VMEM((tm, D_FF), BF16),
            pltpu.VMEM((tm, d), F32),
            pltpu.VMEM((2, 2, d, FF_TILE), BF16),
            pltpu.VMEM((2, D_FF, FF_TILE), BF16),
            pltpu.SemaphoreType.DMA,
            pltpu.SemaphoreType.DMA,
            pltpu.SemaphoreType.DMA((2,)),
            pltpu.SemaphoreType.DMA((2,)),
        ],
    )
    return pl.pallas_call(
        _expert_body,
        grid_spec=grid_spec,
        out_shape=jax.ShapeDtypeStruct((n_out_rows, d), F32),
        compiler_params=_params("arbitrary"),
        name="experts",
    )(block_expert, n_valid, slot_tok, slot_tok, slot_dst, slot_dst, hf_packed, wg, wu, wd, bg, bu, bd)


COMBINE_ROWS = 256


def _combine_body(h_ref, gate_ref, yk_ref, o_ref):
    tm = h_ref.shape[0]
    gates = gate_ref[...]
    acc = h_ref[...]
    for k in range(TOP_K):
        acc = acc + gates[:, k:k + 1] * yk_ref[k * tm:(k + 1) * tm, :]
    o_ref[...] = acc


def combine(h, gates, yk):
    t, d = h.shape
    tm = min(COMBINE_ROWS, t)
    return pl.pallas_call(
        _combine_body,
        grid=(t // tm,),
        in_specs=[
            pl.BlockSpec((tm, d), lambda i: (i, 0)),
            pl.BlockSpec((tm, LANES), lambda i: (i, 0)),
            pl.BlockSpec((TOP_K * tm, d), lambda i: (i, 0)),
        ],
        out_specs=pl.BlockSpec((tm, d), lambda i: (i, 0)),
        out_shape=jax.ShapeDtypeStruct((t, d), F32),
        compiler_params=_params("arbitrary"),
        name="combine",
    )(h, gates, yk)


def routing_tables(top_idx, t):
    n_assign = t * TOP_K
    tm_c = min(COMBINE_ROWS, t)
    e_flat = top_idx.reshape(-1).astype(jnp.int32)
    _, order = lax.sort((e_flat, jnp.arange(n_assign, dtype=jnp.int32)), num_keys=1, is_stable=True)
    experts_iota = jnp.arange(N_EXPERTS, dtype=jnp.int32)
    counts = jnp.sum((e_flat[None, :] == experts_iota[:, None]).astype(jnp.int32), axis=1)
    starts = jnp.cumsum(counts) - counts
    padded = (counts + MOE_ROWS - 1) // MOE_ROWS * MOE_ROWS
    pends = jnp.cumsum(padded)
    pstarts = pends - padded
    n_blocks = -(-n_assign // MOE_ROWS) + N_EXPERTS
    block_start = jnp.arange(n_blocks, dtype=jnp.int32) * MOE_ROWS
    block_expert = jnp.minimum(
        jnp.sum((pends[None, :] <= block_start[:, None]).astype(jnp.int32), axis=1), N_EXPERTS - 1)
    n_valid = (pends[-1:] // MOE_ROWS).astype(jnp.int32)

    row = jnp.arange(MOE_ROWS, dtype=jnp.int32)[None, :]
    e_b = block_expert[:, None]
    local = jnp.arange(n_blocks, dtype=jnp.int32)[:, None] * MOE_ROWS + row - pstarts[e_b]
    used = (local >= 0) & (local < counts[e_b])
    asg = order[jnp.clip(starts[e_b] + local, 0, n_assign - 1)]
    tok = asg // TOP_K
    dst = (tok // tm_c) * (TOP_K * tm_c) + (asg % TOP_K) * tm_c + tok % tm_c
    slot_tok = jnp.where(used, tok, 0).reshape(n_blocks, 1, MOE_ROWS)
    slot_dst = jnp.concatenate([n_assign + row, jnp.where(used, dst, n_assign + row)], axis=0)
    return block_expert, n_valid, slot_tok, slot_dst.reshape(n_blocks + 1, 1, MOE_ROWS)


def _pad_lanes(v, width, value=0.0):
    return jnp.pad(v, ((0, 0), (0, width - v.shape[1])), constant_values=value)


def hybrid_layer(x, attn_norm_w, w_in, q_norm_w, k_norm_w, attn_sinks, conv_w, conv_b, dt_bias, a_log, d_skip,
                 ssm_norm_w, attn_out_norm_w, w_out, ffn_norm_w, w_router, b_router, w_gate, b_gate, w_up, b_up,
                 w_down, b_down):
    batch, seq, d = x.shape
    t = batch * seq
    x2d = x.reshape(t, d)

    q_w, k_w, v_w, z_w, xbc_w, dt_w = jnp.split(
        w_in, [ATTN_WIDTH, ATTN_WIDTH + KV_WIDTH, ATTN_WIDTH + 2 * KV_WIDTH,
               ATTN_WIDTH + 2 * KV_WIDTH + SSM_WIDTH, ATTN_WIDTH + 2 * KV_WIDTH + SSM_WIDTH + CONV_WIDTH], axis=1)
    w_p = jnp.concatenate([q_w, z_w, xbc_w, k_w, v_w, dt_w], axis=1)
    w_p = _pad_lanes(w_p, PROJ_WIDTH).astype(BF16)
    proj = in_proj(x2d, attn_norm_w.reshape(1, d), w_p)

    qw2 = (jnp.tile(q_norm_w, 2) * (HEAD_DIM ** -0.5)).reshape(1, LANES)
    kw2 = jnp.tile(k_norm_w, 2).reshape(1, LANES)
    attn, (wg16, wu16) = attention(
        proj, attn_sinks, qw2, kw2, attn_out_norm_w.reshape(1, ATTN_WIDTH), batch, seq,
        side=(w_gate.reshape(N_EXPERTS * D_MODEL, D_FF), w_up.reshape(N_EXPERTS * D_MODEL, D_FF)))

    ssm, (wd16,) = ssd(
        proj,
        conv_w[:, :SSM_WIDTH], conv_w[:, SSM_WIDTH:],
        conv_b[:SSM_WIDTH].reshape(1, -1), conv_b[SSM_WIDTH:].reshape(1, -1),
        _pad_lanes(dt_bias.reshape(1, -1), LANES), _pad_lanes(a_log.reshape(1, -1), LANES),
        jnp.repeat(d_skip, HEAD_DIM).reshape(1, SSM_WIDTH), ssm_norm_w.reshape(1, SSM_WIDTH),
        batch, seq, side=(w_down.reshape(N_EXPERTS * D_FF, D_MODEL),))

    h, hf_packed, idx_pad, gate_pad = out_proj_router(
        x2d, attn, ssm, w_out.astype(BF16), ffn_norm_w.reshape(1, d),
        _pad_lanes(w_router, LANES), _pad_lanes(b_router.reshape(1, -1), LANES, NEG_BIG))

    block_expert, n_valid, slot_tok, slot_dst = routing_tables(idx_pad[:, :TOP_K], t)
    yk = experts(
        block_expert, n_valid, slot_tok, slot_dst, hf_packed,
        wg16.reshape(N_EXPERTS, D_MODEL, D_FF), b_gate.reshape(N_EXPERTS, 1, D_FF),
        wu16.reshape(N_EXPERTS, D_MODEL, D_FF), b_up.reshape(N_EXPERTS, 1, D_FF),
        wd16.reshape(N_EXPERTS, D_FF, D_MODEL), b_down.reshape(N_EXPERTS, 1, D_MODEL),
        t * TOP_K + MOE_ROWS)

    out = combine(h, gate_pad, yk)
    return out.reshape(batch, seq, d)


def kernel(x, attn_norm_w, w_in, q_norm_w, k_norm_w, attn_sinks, conv_w, conv_b, dt_bias, a_log, d_skip, ssm_norm_w,
           attn_out_norm_w, w_out, ffn_norm_w, w_router, b_router, w_gate, b_gate, w_up, b_up, w_down, b_down):
    for i in range(attn_norm_w.shape[0]):
        x = hybrid_layer(x, attn_norm_w[i], w_in[i], q_norm_w[i], k_norm_w[i], attn_sinks[i], conv_w[i], conv_b[i],
                         dt_bias[i], a_log[i], d_skip[i], ssm_norm_w[i], attn_out_norm_w[i], w_out[i], ffn_norm_w[i],
                         w_router[i], b_router[i], w_gate[i], b_gate[i], w_up[i], b_up[i], w_down[i], b_down[i])
    return x
```

```python
import functools

import jax
import jax.numpy as jnp
from jax import lax
from jax.experimental import pallas as pl
from jax.experimental.pallas import tpu as pltpu

F32 = jnp.float32
BF16 = jnp.bfloat16

D_MODEL = 2048
HEAD_DIM = 64
ATTN_WIDTH = 1024
KV_WIDTH = 128
WINDOW = 128
SSM_WIDTH = 1024
SSM_HEADS = 16
SSM_GROUPS = 2
SSM_STATE = 128
SSM_CONV = 4
SSM_CHUNK = 128
BC_WIDTH = SSM_GROUPS * SSM_STATE
CONV_WIDTH = SSM_WIDTH + 2 * BC_WIDTH
N_EXPERTS = 32
TOP_K = 4
D_FF = 2048
SWIGLU_LIMIT = 7.0
SWIGLU_ALPHA = 1.702
EPS = 1e-6

LANES = 128
SUBLANES = 8
VMEM_LIMIT = 56 * 1024 * 1024

PROJ_WIDTH = 4096
COL_Q = 0
COL_Z = ATTN_WIDTH
COL_XBC = COL_Z + SSM_WIDTH
COL_K = COL_XBC + CONV_WIDTH
COL_V = COL_K + KV_WIDTH
COL_DT = COL_V + KV_WIDTH
NEG_BIG = -1e30


def _params(*sem):
    return pltpu.CompilerParams(dimension_semantics=sem, vmem_limit_bytes=VMEM_LIMIT)


def _split2(v):
    hi = v.astype(BF16)
    lo = (v - hi.astype(F32)).astype(BF16)
    return hi, lo


def _split3(v):
    hi = v.astype(BF16)
    r = v - hi.astype(F32)
    mid = r.astype(BF16)
    lo = (r - mid.astype(F32)).astype(BF16)
    return hi, mid, lo


def _dot(a, b):
    return jnp.dot(a, b, preferred_element_type=F32)


def _dot_exact_lhs(parts, m):
    acc = _dot(parts[0], m)
    for p in parts[1:]:
        acc = acc + _dot(p, m)
    return acc


def _sigmoid(v):
    return 1.0 / (1.0 + jnp.exp(-v))


def _iota(shape, dim):
    return lax.broadcasted_iota(jnp.int32, shape, dim)


def _in_proj_body(x0_ref, xn_ref, g_ref, w_ref, o_ref, hn_ref):
    i = pl.program_id(0)

    def norm(x):
        ms = jnp.mean(x * x, axis=-1, keepdims=True)
        return (x * lax.rsqrt(ms + EPS) * g_ref[...]).astype(BF16)

    @pl.when(i == 0)
    def _():
        hn_ref[0] = norm(x0_ref[...])

    slot = i % 2
    hn_ref[1 - slot] = norm(xn_ref[...])
    o_ref[...] = _dot(hn_ref[slot], w_ref[...])


def in_proj(x2d, g, w_p):
    t, d = x2d.shape
    n = w_p.shape[1]
    tm = min(512, t)
    ni = t // tm
    once = pl.Buffered(1)
    return pl.pallas_call(
        _in_proj_body,
        grid=(ni,),
        in_specs=[
            pl.BlockSpec((tm, d), lambda i: (0, 0), pipeline_mode=once),
            pl.BlockSpec((tm, d), lambda i: (jnp.minimum(i + 1, ni - 1), 0)),
            pl.BlockSpec((1, d), lambda i: (0, 0), pipeline_mode=once),
            pl.BlockSpec((d, n), lambda i: (0, 0), pipeline_mode=once),
        ],
        out_specs=pl.BlockSpec((tm, n), lambda i: (i, 0)),
        out_shape=jax.ShapeDtypeStruct((t, n), F32),
        scratch_shapes=[pltpu.VMEM((2, tm, d), BF16)],
        compiler_params=_params("arbitrary"),
        name="in_proj",
    )(x2d, x2d, g, w_p)


def _head_pair_norm(v, w, blockdiag):
    ss = _dot_exact_lhs(_split2(v * v), blockdiag)
    return v * lax.rsqrt(ss * (1.0 / HEAD_DIM) + EPS) * w


def _cast_side_streams(refs, n_side):
    for k in range(n_side):
        refs[n_side + 1 + k][...] = refs[k][...].astype(BF16)
    return refs[n_side]


def _side_specs(side, n_steps, step_index):
    specs, shapes = [], []
    for w in side:
        rows, cols = w.shape
        specs.append(pl.BlockSpec((rows // n_steps, cols), step_index))
        shapes.append(jax.ShapeDtypeStruct((rows, cols), BF16))
    return specs, shapes


def _attn_body(n_side, sinks_ref, q_ref, kvc_ref, kvp_ref, qw_ref, kw_ref, onw_ref, *rest):
    o_ref = _cast_side_streams(rest, n_side)
    acc_ref = rest[-1]
    i = pl.program_id(1)
    tq = q_ref.shape[0]
    nsub = tq // WINDOW
    nk = tq + WINDOW
    lo_lanes = _iota((1, LANES), 1) < HEAD_DIM
    blockdiag = jnp.where(
        _iota((LANES, LANES), 0) // HEAD_DIM == _iota((LANES, LANES), 1) // HEAD_DIM, 1.0, 0.0
    ).astype(BF16)

    kv_all = jnp.concatenate([kvp_ref[...], kvc_ref[...]], axis=0)
    k_all = _head_pair_norm(kv_all[:, :KV_WIDTH], kw_ref[...], blockdiag)
    v_all = kv_all[:, KV_WIDTH:]
    kt = k_all.T
    zero_half = jnp.zeros((HEAD_DIM, nk), F32)
    kt_var = [
        [jnp.concatenate([kt[j * HEAD_DIM:(j + 1) * HEAD_DIM], zero_half], axis=0).astype(BF16),
         jnp.concatenate([zero_half, kt[j * HEAD_DIM:(j + 1) * HEAD_DIM]], axis=0).astype(BF16)]
        for j in range(2)
    ]
    v_sw = pltpu.roll(v_all, HEAD_DIM, axis=1)
    v_var = [
        [jnp.where(lo_lanes, v_all, 0.0), jnp.where(lo_lanes, 0.0, v_sw)],
        [jnp.where(lo_lanes, v_sw, 0.0), jnp.where(lo_lanes, 0.0, v_all)],
    ]

    qn = [
        _head_pair_norm(q_ref[:, c * LANES:(c + 1) * LANES], qw_ref[...], blockdiag).astype(BF16)
        for c in range(ATTN_WIDTH // LANES)
    ]

    rows = 4 * WINDOW
    qi = _iota((rows, 2 * WINDOW), 0) % WINDOW
    kj = _iota((rows, 2 * WINDOW), 1)
    band = (kj > qi) & (kj <= qi + WINDOW)
    first_min = jnp.where(i == 0, WINDOW, 0)
    band_first = band & (kj >= first_min)
    sink_col = _iota((WINDOW, 2 * WINDOW), 1) == 0
    fill = [[jnp.concatenate([jnp.where(sink_col, sinks_ref[8 * j + 2 * c + par], NEG_BIG) for c in range(4)], axis=0)
             for par in range(2)] for j in range(2)]
    not_first_key = _iota((2 * WINDOW, LANES), 0) > 0
    lo_band = _iota((2 * WINDOW, LANES), 1) < HEAD_DIM
    ones_half = [jnp.where(lo_band, 1.0, 0.0).astype(BF16), jnp.where(lo_band, 0.0, 1.0).astype(BF16)]

    for s in range(nsub):
        mask = band_first if s == 0 else band
        keys = slice(s * WINDOW, s * WINDOW + 2 * WINDOW)
        for j in range(2):
            qs = jnp.concatenate([qn[4 * j + c][s * WINDOW:(s + 1) * WINDOW] for c in range(4)], axis=0)
            num = None
            den = None
            for par in range(2):
                sc = jnp.where(mask, _dot(qs, kt_var[j][par][:, keys]), fill[j][par])
                m = jnp.max(sc, axis=-1, keepdims=True)
                p = jnp.exp(sc - m).astype(BF16)
                v_band = jnp.where(not_first_key, v_var[j][par][keys], 0.0).astype(BF16)
                pv = _dot(p, v_band)
                ps = _dot(p, ones_half[par])
                num = pv if num is None else num + pv
                den = ps if den is None else den + ps
            out = num / den
            for c in range(4):
                acc_ref[s * WINDOW:(s + 1) * WINDOW, (4 * j + c) * LANES:(4 * j + c + 1) * LANES] = (
                    out[c * WINDOW:(c + 1) * WINDOW])

    a = acc_ref[...]
    ms = jnp.mean(a * a, axis=-1, keepdims=True)
    o_ref[...] = (a * lax.rsqrt(ms + EPS) * onw_ref[...]).astype(BF16)


def attention(proj, sinks, qw2, kw2, onw, batch, seq, side=()):
    t = proj.shape[0]
    tq = min(256, seq)
    nq = seq // tq
    sub = tq // WINDOW
    kv_col = COL_K // (2 * KV_WIDTH)
    side_specs, side_shapes = _side_specs(side, batch * nq, lambda b, i, s: (b * nq + i, 0))
    grid_spec = pltpu.PrefetchScalarGridSpec(
        num_scalar_prefetch=1,
        grid=(batch, nq),
        in_specs=[
            pl.BlockSpec((tq, ATTN_WIDTH), lambda b, i, s: (b * nq + i, COL_Q // ATTN_WIDTH)),
            pl.BlockSpec((tq, 2 * KV_WIDTH), lambda b, i, s: (b * nq + i, kv_col)),
            pl.BlockSpec((WINDOW, 2 * KV_WIDTH),
                         lambda b, i, s: (b * nq * sub + jnp.maximum(i * sub - 1, 0), kv_col)),
            pl.BlockSpec((1, LANES), lambda b, i, s: (0, 0)),
            pl.BlockSpec((1, LANES), lambda b, i, s: (0, 0)),
            pl.BlockSpec((1, ATTN_WIDTH), lambda b, i, s: (0, 0)),
        ] + side_specs,
        out_specs=[pl.BlockSpec((tq, ATTN_WIDTH), lambda b, i, s: (b * nq + i, 0))] + side_specs,
        scratch_shapes=[pltpu.VMEM((tq, ATTN_WIDTH), F32)],
    )
    outs = pl.pallas_call(
        functools.partial(_attn_body, len(side)),
        grid_spec=grid_spec,
        out_shape=[jax.ShapeDtypeStruct((t, ATTN_WIDTH), BF16)] + side_shapes,
        compiler_params=_params("arbitrary", "arbitrary"),
        name="attention",
    )(sinks, proj, proj, proj, qw2, kw2, onw, *side)
    return outs[0], outs[1:]


def _ssd_body(n_side, z_ref, xs_ref, bc_ref, dt_ref, cwx_ref, cwb_ref, cbx_ref, cbb_ref, dtb_ref, alog_ref,
              dsk_ref, nw_ref, *rest):
    o_ref = _cast_side_streams(rest, n_side)
    xpx_ref, xpb_ref, st_ref, y_ref = rest[-4:]
    i = pl.program_id(1)
    ts = xs_ref.shape[0]
    nch = ts // SSM_CHUNK
    gw = SSM_WIDTH // SSM_GROUPS
    pad = SUBLANES

    @pl.when(i == 0)
    def _():
        xpx_ref[0:pad, :] = jnp.zeros((pad, SSM_WIDTH), F32)
        xpb_ref[0:pad, :] = jnp.zeros((pad, 2 * BC_WIDTH), F32)
        st_ref[...] = jnp.zeros(st_ref.shape, F32)

    xpx_ref[pad:pad + ts, :] = xs_ref[...]
    xpb_ref[pad:pad + ts, :] = bc_ref[...]

    def conv_silu(xp_ref, w_ref, b_ref):
        acc = b_ref[...] + w_ref[SSM_CONV - 1:SSM_CONV, :] * xp_ref[pad:pad + ts, :]
        for k in range(SSM_CONV - 1):
            off = pad - (SSM_CONV - 1) + k
            acc = acc + w_ref[k:k + 1, :] * xp_ref[off:off + ts, :]
        return acc * _sigmoid(acc)

    xc = conv_silu(xpx_ref, cwx_ref, cbx_ref)
    bcc = conv_silu(xpb_ref, cwb_ref, cbb_ref)
    xpx_ref[0:pad, :] = xs_ref[ts - pad:ts, :]
    xpb_ref[0:pad, :] = bc_ref[ts - pad:ts, :]

    dtr = dt_ref[...] + dtb_ref[...]
    dt = jnp.maximum(dtr, 0.0) + jnp.log(1.0 + jnp.exp(-jnp.abs(dtr)))
    a = -jnp.exp(alog_ref[...])
    da = dt * a

    ri = _iota((ts, ts), 0)
    ci = _iota((ts, ts), 1)
    same = (ri // SSM_CHUNK) == (ci // SSM_CHUNK)
    tri_blk = jnp.where(same & (ci <= ri), 1.0, 0.0).astype(BF16)
    all_blk = jnp.where(same, 1.0, 0.0).astype(BF16)
    da3 = _split3(da)
    cs = _dot(tri_blk, da3[0]) + _dot(tri_blk, da3[1]) + _dot(tri_blk, da3[2])
    cl = _dot(all_blk, da3[0]) + _dot(all_blk, da3[1]) + _dot(all_blk, da3[2])
    ecs = jnp.exp(cs)
    dte = jnp.exp(cl - cs)

    expand = jnp.where(_iota((LANES, SSM_WIDTH), 1) // HEAD_DIM == _iota((LANES, SSM_WIDTH), 0),
                       1.0, 0.0).astype(BF16)
    dt_e = _dot_exact_lhs(_split2(dt), expand)
    dtd_e = _dot_exact_lhs(_split2(dt * dte), expand)
    ecs_e = _dot_exact_lhs(_split2(ecs), expand)
    xf = xc * dt_e
    xd = (xc * dtd_e).astype(BF16)

    lo_lanes = _iota((1, LANES), 1) < HEAD_DIM
    li = _iota((SSM_CHUNK, SSM_CHUNK), 0)
    lj = _iota((SSM_CHUNK, SSM_CHUNK), 1)
    tril = lj <= li
    upper = jnp.where(li <= lj, 1.0, 0.0).astype(BF16)

    for c in range(nch):
        r0 = c * SSM_CHUNK
        da_t = da[r0:r0 + SSM_CHUNK].T
        cs_t = _dot_exact_lhs(_split3(da_t), upper)
        cs_c = cs[r0:r0 + SSM_CHUNK]
        for g in range(SSM_GROUPS):
            b_c = bcc[r0:r0 + SSM_CHUNK, g * SSM_STATE:(g + 1) * SSM_STATE]
            c_c = bcc[r0:r0 + SSM_CHUNK, BC_WIDTH + g * SSM_STATE:BC_WIDTH + (g + 1) * SSM_STATE].astype(BF16)
            b_t = b_c.T.astype(BF16)
            cb = _dot(c_c, b_t)
            st = st_ref[g]
            y_off = _dot(c_c, st.astype(BF16)) * ecs_e[r0:r0 + SSM_CHUNK, g * gw:(g + 1) * gw]
            s_new = _dot(b_t, xd[r0:r0 + SSM_CHUNK, g * gw:(g + 1) * gw])
            parts = []
            for p in range(4):
                h0 = 8 * g + 2 * p
                ms = []
                for h in (h0, h0 + 1):
                    diff = cs_c[:, h:h + 1] - cs_t[h:h + 1, :]
                    ms.append((cb * jnp.exp(jnp.where(tril, diff, NEG_BIG))).astype(BF16))
                m2 = jnp.concatenate(ms, axis=1)
                x2 = xf[r0:r0 + SSM_CHUNK, h0 * HEAD_DIM:(h0 + 2) * HEAD_DIM]
                xbd = jnp.concatenate([jnp.where(lo_lanes, x2, 0.0), jnp.where(lo_lanes, 0.0, x2)],
                                      axis=0).astype(BF16)
                parts.append(_dot(m2, xbd))
            y_ref[r0:r0 + SSM_CHUNK, g * gw:(g + 1) * gw] = jnp.concatenate(parts, axis=1) + y_off
            dec = ecs_e[r0 + SSM_CHUNK - 1:r0 + SSM_CHUNK, g * gw:(g + 1) * gw]
            st_ref[g] = st * dec + s_new

    y = y_ref[...] + xc * dsk_ref[...]
    z = z_ref[...]
    gated = y * (z * _sigmoid(z))
    outs = []
    for g in range(SSM_GROUPS):
        gg = gated[:, g * gw:(g + 1) * gw]
        ms = jnp.mean(gg * gg, axis=-1, keepdims=True)
        outs.append(gg * lax.rsqrt(ms + EPS) * nw_ref[:, g * gw:(g + 1) * gw])
    o_ref[...] = jnp.concatenate(outs, axis=1).astype(BF16)


def ssd(proj, cwx, cwb, cbx, cbb, dtb, alog, dsk, nw, batch, seq, side=()):
    t = proj.shape[0]
    ts = min(256, seq)
    ns = seq // ts
    row = lambda b, i: b * ns + i
    const = lambda b, i: (0, 0)
    side_specs, side_shapes = _side_specs(side, batch * ns, lambda b, i: (row(b, i), 0))
    outs = pl.pallas_call(
        functools.partial(_ssd_body, len(side)),
        grid=(batch, ns),
        in_specs=[
            pl.BlockSpec((ts, SSM_WIDTH), lambda b, i: (row(b, i), COL_Z // SSM_WIDTH)),
            pl.BlockSpec((ts, SSM_WIDTH), lambda b, i: (row(b, i), COL_XBC // SSM_WIDTH)),
            pl.BlockSpec((ts, 2 * BC_WIDTH), lambda b, i: (row(b, i), (COL_XBC + SSM_WIDTH) // (2 * BC_WIDTH))),
            pl.BlockSpec((ts, LANES), lambda b, i: (row(b, i), COL_DT // LANES)),
            pl.BlockSpec((SSM_CONV, SSM_WIDTH), const),
            pl.BlockSpec((SSM_CONV, 2 * BC_WIDTH), const),
            pl.BlockSpec((1, SSM_WIDTH), const),
            pl.BlockSpec((1, 2 * BC_WIDTH), const),
            pl.BlockSpec((1, LANES), const),
            pl.BlockSpec((1, LANES), const),
            pl.BlockSpec((1, SSM_WIDTH), const),
            pl.BlockSpec((1, SSM_WIDTH), const),
        ] + side_specs,
        out_specs=[pl.BlockSpec((ts, SSM_WIDTH), lambda b, i: (row(b, i), 0))] + side_specs,
        out_shape=[jax.ShapeDtypeStruct((t, SSM_WIDTH), BF16)] + side_shapes,
        scratch_shapes=[
            pltpu.VMEM((ts + SUBLANES, SSM_WIDTH), F32),
            pltpu.VMEM((ts + SUBLANES, 2 * BC_WIDTH), F32),
            pltpu.VMEM((SSM_GROUPS, SSM_STATE, SSM_WIDTH // SSM_GROUPS), F32),
            pltpu.VMEM((ts, SSM_WIDTH), F32),
        ],
        compiler_params=_params("arbitrary", "arbitrary"),
        name="ssd",
    )(proj, proj, proj, proj, cwx, cwb, cbx, cbb, dtb, alog, dsk, nw, *side)
    return outs[0], outs[1:]


def _out_body(x_ref, a_ref, s_ref, wo_ref, fg_ref, wr_ref, br_ref, h_ref, hf_ref, idx_ref, gate_ref):
    tm = x_ref.shape[0]
    h = x_ref[...] + _dot(a_ref[...], wo_ref[0:ATTN_WIDTH, :]) + _dot(s_ref[...], wo_ref[ATTN_WIDTH:, :])
    h_ref[...] = h
    ms = jnp.mean(h * h, axis=-1, keepdims=True)
    hf = h * lax.rsqrt(ms + EPS) * fg_ref[...]
    hi, lo = _split2(hf)
    hi32 = hi.astype(F32)
    hf_ref[...] = (pltpu.bitcast(hi32[:, D_MODEL // 2:], jnp.uint32)
                   | (pltpu.bitcast(hi32[:, :D_MODEL // 2], jnp.uint32) >> 16))

    whi, wlo = _split2(wr_ref[...])
    hw = _dot(hi, jnp.concatenate([whi, wlo], axis=1))
    logits = hw[:, :LANES] + hw[:, LANES:] + _dot(lo, whi) + br_ref[...]
    lane = _iota((tm, LANES), 1)
    vals, idxs = [], []
    for _ in range(TOP_K):
        m = jnp.max(logits, axis=-1, keepdims=True)
        am = jnp.min(jnp.where(logits == m, lane, LANES), axis=-1, keepdims=True)
        vals.append(m)
        idxs.append(am)
        logits = jnp.where(lane == am, NEG_BIG * 2.0, logits)
    es = [jnp.exp(v - vals[0]) for v in vals]
    tot = es[0] + es[1] + es[2] + es[3]
    idx_out = jnp.zeros((tm, LANES), jnp.int32)
    gate_out = jnp.zeros((tm, LANES), F32)
    for k in range(TOP_K):
        idx_out = jnp.where(lane == k, idxs[k], idx_out)
        gate_out = jnp.where(lane == k, es[k] / tot, gate_out)
    idx_ref[...] = idx_out
    gate_ref[...] = gate_out


def out_proj_router(x2d, attn, ssm, wo, fg, wr, br):
    t, d = x2d.shape
    tm = min(512, t)
    const = lambda i: (0, 0)
    once = pl.Buffered(1)
    return pl.pallas_call(
        _out_body,
        grid=(t // tm,),
        in_specs=[
            pl.BlockSpec((tm, d), lambda i: (i, 0)),
            pl.BlockSpec((tm, ATTN_WIDTH), lambda i: (i, 0)),
            pl.BlockSpec((tm, SSM_WIDTH), lambda i: (i, 0)),
            pl.BlockSpec((d, d), const, pipeline_mode=once),
            pl.BlockSpec((1, d), const, pipeline_mode=once),
            pl.BlockSpec((d, LANES), const, pipeline_mode=once),
            pl.BlockSpec((1, LANES), const, pipeline_mode=once),
        ],
        out_specs=[
            pl.BlockSpec((tm, d), lambda i: (i, 0)),
            pl.BlockSpec((tm, d // 2), lambda i: (i, 0)),
            pl.BlockSpec((tm, LANES), lambda i: (i, 0)),
            pl.BlockSpec((tm, LANES), lambda i: (i, 0)),
        ],
        out_shape=[
            jax.ShapeDtypeStruct((t, d), F32),
            jax.ShapeDtypeStruct((t, d // 2), jnp.uint32),
            jax.ShapeDtypeStruct((t, LANES), jnp.int32),
            jax.ShapeDtypeStruct((t, LANES), F32),
        ],
        compiler_params=_params("arbitrary"),
        name="out_proj_router",
    )(x2d, attn, ssm, wo, fg, wr, br)


MOE_ROWS = 512
FF_TILE = 1024
N_FF_TILES = D_FF // FF_TILE


def _expert_body(be_ref, nv_ref, tok_ref, tokn_ref, dst_ref, dstp_ref, hf_ref, wg_ref, wu_ref, wd_ref, bg_ref, bu_ref,
                 bd_ref, yk_ref, xrow_ref, x_ref, h_ref, y_ref, gu_ref, dn_ref, row_sem, out_sem, gu_sem, dn_sem):
    b = pl.program_id(0)
    nv = nv_ref[0]
    tm = h_ref.shape[0]
    half = D_MODEL // 2

    def row_copy(table_ref, r):
        return pltpu.make_async_copy(hf_ref.at[pl.ds(table_ref[0, 0, r], 1)], xrow_ref.at[pl.ds(r, 1)], row_sem)

    def rows_wait():
        pltpu.make_async_copy(hf_ref.at[pl.ds(0, tm)], xrow_ref, row_sem).wait()

    def unpack_rows(slot):
        w = xrow_ref[...]
        x_ref[slot, :, 0:half] = pltpu.bitcast(w << 16, F32).astype(BF16)
        x_ref[slot, :, half:] = pltpu.bitcast(w & jnp.uint32(0xFFFF0000), F32).astype(BF16)

    def out_copy(table_ref, r):
        return pltpu.make_async_copy(y_ref.at[pl.ds(r, 1)], yk_ref.at[pl.ds(table_ref[0, 0, r], 1)], out_sem)

    def out_wait():
        pltpu.make_async_copy(y_ref, yk_ref.at[pl.ds(0, tm)], out_sem).wait()

    n_chunks = 2 * N_FF_TILES

    def chunk_copies(e, c):
        j = c % N_FF_TILES
        cols = pl.ds(j * FF_TILE, FF_TILE)
        if c < N_FF_TILES:
            return (pltpu.make_async_copy(wg_ref.at[e, :, cols], gu_ref.at[j, 0], gu_sem.at[j]),
                    pltpu.make_async_copy(wu_ref.at[e, :, cols], gu_ref.at[j, 1], gu_sem.at[j]))
        return (pltpu.make_async_copy(wd_ref.at[e, :, cols], dn_ref.at[j], dn_sem.at[j]),)

    def start_chunk(e, c):
        for cp in chunk_copies(e, c):
            cp.start()

    def next_chunk(e, c):
        for cp in chunk_copies(e, c):
            cp.wait()
        if c == 0:
            start_chunk(e, n_chunks - 1)
        else:
            @pl.when(b + 1 < nv)
            def _():
                start_chunk(be_ref[b + 1], c - 1)

    per_batch = tm // N_FF_TILES

    @pl.when(b < nv)
    def _():
        e = be_ref[b]
        slot = b % 2

        @pl.when(b == 0)
        def _():
            def issue(r, carry):
                row_copy(tok_ref, r).start()
                return carry
            lax.fori_loop(0, tm, issue, 0, unroll=8)
            for c in range(n_chunks - 1):
                start_chunk(e, c)
            y_ref[...] = jnp.zeros(y_ref.shape, F32)
            rows_wait()
            unpack_rows(0)

        x = x_ref[slot]
        for j in range(N_FF_TILES):
            next_chunk(e, j)
            cols = slice(j * FF_TILE, (j + 1) * FF_TILE)
            g_acc = _dot(x, gu_ref[j, 0])
            u_acc = _dot(x, gu_ref[j, 1])
            for r in range(j * per_batch, (j + 1) * per_batch):
                row_copy(tokn_ref, r).start()
                out_copy(dstp_ref, r).start()
            gl = jnp.minimum(g_acc + bg_ref[0, :, cols], SWIGLU_LIMIT)
            li = jnp.clip(u_acc + bu_ref[0, :, cols], -SWIGLU_LIMIT, SWIGLU_LIMIT)
            h_ref[:, cols] = (gl * _sigmoid(SWIGLU_ALPHA * gl) * (li + 1.0)).astype(BF16)

        out_wait()
        rows_wait()

        h = h_ref[...]
        for j in range(N_FF_TILES):
            next_chunk(e, N_FF_TILES + j)
            if j == 0:
                unpack_rows(1 - slot)
            cols = slice(j * FF_TILE, (j + 1) * FF_TILE)
            y_ref[:, cols] = _dot(h, dn_ref[j]) + bd_ref[0, :, cols]

        @pl.when(b + 1 == nv)
        def _():
            def issue(r, carry):
                out_copy(dst_ref, r).start()
                return carry
            lax.fori_loop(0, tm, issue, 0, unroll=8)
            out_wait()


def experts(block_expert, n_valid, slot_tok, slot_dst, hf_packed, wg, bg, wu, bu, wd, bd, n_out_rows):
    n_blocks = block_expert.shape[0]
    tm = MOE_ROWS
    d = D_MODEL

    def blk(b, nv):
        return jnp.minimum(b, nv[0] - 1)

    any_spec = pl.BlockSpec(memory_space=pl.ANY)
    grid_spec = pltpu.PrefetchScalarGridSpec(
        num_scalar_prefetch=2,
        grid=(n_blocks,),
        in_specs=[
            pl.BlockSpec((1, 1, tm), lambda b, be, nv: (blk(b, nv), 0, 0), memory_space=pltpu.SMEM),
            pl.BlockSpec((1, 1, tm), lambda b, be, nv: (blk(b + 1, nv), 0, 0), memory_space=pltpu.SMEM),
            pl.BlockSpec((1, 1, tm), lambda b, be, nv: (blk(b, nv) + 1, 0, 0), memory_space=pltpu.SMEM),
            pl.BlockSpec((1, 1, tm), lambda b, be, nv: (blk(b, nv), 0, 0), memory_space=pltpu.SMEM),
            any_spec, any_spec, any_spec, any_spec,
            pl.BlockSpec((1, 1, D_FF), lambda b, be, nv: (be[blk(b, nv)], 0, 0)),
            pl.BlockSpec((1, 1, D_FF), lambda b, be, nv: (be[blk(b, nv)], 0, 0)),
            pl.BlockSpec((1, 1, d), lambda b, be, nv: (be[blk(b, nv)], 0, 0)),
        ],
        out_specs=any_spec,
        scratch_shapes=[
            pltpu.VMEM((tm, d // 2), jnp.uint32),
            pltpu.VMEM((2, tm, d), BF16),
            pltpu.VMEM((tm, D_FF), BF16),
            pltpu.VMEM((tm, d), F32),
            pltpu.VMEM((2, 2, d, FF_TILE), BF16),
            pltpu.VMEM((2, D_FF, FF_TILE), BF16),
            pltpu.SemaphoreType.DMA,
            pltpu.SemaphoreType.DMA,
            pltpu.SemaphoreType.DMA((2,)),
            pltpu.SemaphoreType.DMA((2,)),
        ],
    )
    return pl.pallas_call(
        _expert_body,
        grid_spec=grid_spec,
        out_shape=jax.ShapeDtypeStruct((n_out_rows, d), F32),
        compiler_params=_params("arbitrary"),
        name="experts",
    )(block_expert, n_valid, slot_tok, slot_tok, slot_dst, slot_dst, hf_packed, wg, wu, wd, bg, bu, bd)


COMBINE_ROWS = 256


def _combine_body(h_ref, gate_ref, yk_ref, o_ref):
    tm = h_ref.shape[0]
    gates = gate_ref[...]
    acc = h_ref[...]
    for k in range(TOP_K):
        acc = acc + gates[:, k:k + 1] * yk_ref[k * tm:(k + 1) * tm, :]
    o_ref[...] = acc


def combine(h, gates, yk):
    t, d = h.shape
    tm = min(COMBINE_ROWS, t)
    return pl.pallas_call(
        _combine_body,
        grid=(t // tm,),
        in_specs=[
            pl.BlockSpec((tm, d), lambda i: (i, 0)),
            pl.BlockSpec((tm, LANES), lambda i: (i, 0)),
            pl.BlockSpec((TOP_K * tm, d), lambda i: (i, 0)),
        ],
        out_specs=pl.BlockSpec((tm, d), lambda i: (i, 0)),
        out_shape=jax.ShapeDtypeStruct((t, d), F32),
        compiler_params=_params("arbitrary"),
        name="combine",
    )(h, gates, yk)


def routing_tables(top_idx, t):
    n_assign = t * TOP_K
    tm_c = min(COMBINE_ROWS, t)
    e_flat = top_idx.reshape(-1).astype(jnp.int32)
    _, order = lax.sort((e_flat, jnp.arange(n_assign, dtype=jnp.int32)), num_keys=1, is_stable=True)
    experts_iota = jnp.arange(N_EXPERTS, dtype=jnp.int32)
    counts = jnp.sum((e_flat[None, :] == experts_iota[:, None]).astype(jnp.int32), axis=1)
    starts = jnp.cumsum(counts) - counts
    padded = (counts + MOE_ROWS - 1) // MOE_ROWS * MOE_ROWS
    pends = jnp.cumsum(padded)
    pstarts = pends - padded
    n_blocks = -(-n_assign // MOE_ROWS) + N_EXPERTS
    block_start = jnp.arange(n_blocks, dtype=jnp.int32) * MOE_ROWS
    block_expert = jnp.minimum(
        jnp.sum((pends[None, :] <= block_start[:, None]).astype(jnp.int32), axis=1), N_EXPERTS - 1)
    n_valid = (pends[-1:] // MOE_ROWS).astype(jnp.int32)

    row = jnp.arange(MOE_ROWS, dtype=jnp.int32)[None, :]
    e_b = block_expert[:, None]
    local = jnp.arange(n_blocks, dtype=jnp.int32)[:, None] * MOE_ROWS + row - pstarts[e_b]
    used = (local >= 0) & (local < counts[e_b])
    asg = order[jnp.clip(starts[e_b] + local, 0, n_assign - 1)]
    tok = asg // TOP_K
    dst = (tok // tm_c) * (TOP_K * tm_c) + (asg % TOP_K) * tm_c + tok % tm_c
    slot_tok = jnp.where(used, tok, 0).reshape(n_blocks, 1, MOE_ROWS)
    slot_dst = jnp.concatenate([n_assign + row, jnp.where(used, dst, n_assign + row)], axis=0)
    return block_expert, n_valid, slot_tok, slot_dst.reshape(n_blocks + 1, 1, MOE_ROWS)


def _pad_lanes(v, width, value=0.0):
    return jnp.pad(v, ((0, 0), (0, width - v.shape[1])), constant_values=value)


def hybrid_layer(x, attn_norm_w, w_in, q_norm_w, k_norm_w, attn_sinks, conv_w, conv_b, dt_bias, a_log, d_skip,
                 ssm_norm_w, attn_out_norm_w, w_out, ffn_norm_w, w_router, b_router, w_gate, b_gate, w_up, b_up,
                 w_down, b_down):
    batch, seq, d = x.shape
    t = batch * seq
    x2d = x.reshape(t, d)

    q_w, k_w, v_w, z_w, xbc_w, dt_w = jnp.split(
        w_in, [ATTN_WIDTH, ATTN_WIDTH + KV_WIDTH, ATTN_WIDTH + 2 * KV_WIDTH,
               ATTN_WIDTH + 2 * KV_WIDTH + SSM_WIDTH, ATTN_WIDTH + 2 * KV_WIDTH + SSM_WIDTH + CONV_WIDTH], axis=1)
    w_p = jnp.concatenate([q_w, z_w, xbc_w, k_w, v_w, dt_w], axis=1)
    w_p = _pad_lanes(w_p, PROJ_WIDTH).astype(BF16)
    proj = in_proj(x2d, attn_norm_w.reshape(1, d), w_p)

    qw2 = (jnp.tile(q_norm_w, 2) * (HEAD_DIM ** -0.5)).reshape(1, LANES)
    kw2 = jnp.tile(k_norm_w, 2).reshape(1, LANES)
    attn, (wg16, wu16) = attention(
        proj, attn_sinks, qw2, kw2, attn_out_norm_w.reshape(1, ATTN_WIDTH), batch, seq,
        side=(w_gate.reshape(N_EXPERTS * D_MODEL, D_FF), w_up.reshape(N_EXPERTS * D_MODEL, D_FF)))

    ssm, (wd16,) = ssd(
        proj,
        conv_w[:, :SSM_WIDTH], conv_w[:, SSM_WIDTH:],
        conv_b[:SSM_WIDTH].reshape(1, -1), conv_b[SSM_WIDTH:].reshape(1, -1),
        _pad_lanes(dt_bias.reshape(1, -1), LANES), _pad_lanes(a_log.reshape(1, -1), LANES),
        jnp.repeat(d_skip, HEAD_DIM).reshape(1, SSM_WIDTH), ssm_norm_w.reshape(1, SSM_WIDTH),
        batch, seq, side=(w_down.reshape(N_EXPERTS * D_FF, D_MODEL),))

    h, hf_packed, idx_pad, gate_pad = out_proj_router(
        x2d, attn, ssm, w_out.astype(BF16), ffn_norm_w.reshape(1, d),
        _pad_lanes(w_router, LANES), _pad_lanes(b_router.reshape(1, -1), LANES, NEG_BIG))

    block_expert, n_valid, slot_tok, slot_dst = routing_tables(idx_pad[:, :TOP_K], t)
    yk = experts(
        block_expert, n_valid, slot_tok, slot_dst, hf_packed,
        wg16.reshape(N_EXPERTS, D_MODEL, D_FF), b_gate.reshape(N_EXPERTS, 1, D_FF),
        wu16.reshape(N_EXPERTS, D_MODEL, D_FF), b_up.reshape(N_EXPERTS, 1, D_FF),
        wd16.reshape(N_EXPERTS, D_FF, D_MODEL), b_down.reshape(N_EXPERTS, 1, D_MODEL),
        t * TOP_K + MOE_ROWS)

    out = combine(h, gate_pad, yk)
    return out.reshape(batch, seq, d)


def kernel(x, attn_norm_w, w_in, q_norm_w, k_norm_w, attn_sinks, conv_w, conv_b, dt_bias, a_log, d_skip, ssm_norm_w,
           attn_out_norm_w, w_out, ffn_norm_w, w_router, b_router, w_gate, b_gate, w_up, b_up, w_down, b_down):
    for i in range(attn_norm_w.shape[0]):
        x = hybrid_layer(x, attn_norm_w[i], w_in[i], q_norm_w[i], k_norm_w[i], attn_sinks[i], conv_w[i], conv_b[i],
                         dt_bias[i], a_log[i], d_skip[i], ssm_norm_w[i], attn_out_norm_w[i], w_out[i], ffn_norm_w[i],
                         w_router[i], b_router[i], w_gate[i], b_gate[i], w_up[i], b_up[i], w_down[i], b_down[i])
    return x
```

```python
import functools

import jax
import jax.numpy as jnp
from jax import lax
from jax.experimental import pallas as pl
from jax.experimental.pallas import tpu as pltpu

F32 = jnp.float32
BF16 = jnp.bfloat16

D_MODEL = 2048
HEAD_DIM = 64
ATTN_WIDTH = 1024
KV_WIDTH = 128
WINDOW = 128
SSM_WIDTH = 1024
SSM_HEADS = 16
SSM_GROUPS = 2
SSM_STATE = 128
SSM_CONV = 4
SSM_CHUNK = 128
BC_WIDTH = SSM_GROUPS * SSM_STATE
CONV_WIDTH = SSM_WIDTH + 2 * BC_WIDTH
N_EXPERTS = 32
TOP_K = 4
D_FF = 2048
SWIGLU_LIMIT = 7.0
SWIGLU_ALPHA = 1.702
EPS = 1e-6

LANES = 128
SUBLANES = 8
VMEM_LIMIT = 56 * 1024 * 1024

PROJ_WIDTH = 4096
COL_Q = 0
COL_Z = ATTN_WIDTH
COL_XBC = COL_Z + SSM_WIDTH
COL_K = COL_XBC + CONV_WIDTH
COL_V = COL_K + KV_WIDTH
COL_DT = COL_V + KV_WIDTH
NEG_BIG = -1e30


def _params(*sem):
    return pltpu.CompilerParams(dimension_semantics=sem, vmem_limit_bytes=VMEM_LIMIT)


def _split2(v):
    hi = v.astype(BF16)
    lo = (v - hi.astype(F32)).astype(BF16)
    return hi, lo


def _split3(v):
    hi = v.astype(BF16)
    r = v - hi.astype(F32)
    mid = r.astype(BF16)
    lo = (r - mid.astype(F32)).astype(BF16)
    return hi, mid, lo


def _dot(a, b):
    return jnp.dot(a, b, preferred_element_type=F32)


def _dot_exact_lhs(parts, m):
    acc = _dot(parts[0], m)
    for p in parts[1:]:
        acc = acc + _dot(p, m)
    return acc


def _sigmoid(v):
    return 1.0 / (1.0 + jnp.exp(-v))


def _iota(shape, dim):
    return lax.broadcasted_iota(jnp.int32, shape, dim)


def _in_proj_body(x0_ref, xn_ref, g_ref, w_ref, o_ref, hn_ref):
    i = pl.program_id(0)

    def norm(x):
        ms = jnp.mean(x * x, axis=-1, keepdims=True)
        return (x * lax.rsqrt(ms + EPS) * g_ref[...]).astype(BF16)

    @pl.when(i == 0)
    def _():
        hn_ref[0] = norm(x0_ref[...])

    slot = i % 2
    hn_ref[1 - slot] = norm(xn_ref[...])
    o_ref[...] = _dot(hn_ref[slot], w_ref[...])


def in_proj(x2d, g, w_p):
    t, d = x2d.shape
    n = w_p.shape[1]
    tm = min(512, t)
    ni = t // tm
    once = pl.Buffered(1)
    return pl.pallas_call(
        _in_proj_body,
        grid=(ni,),
        in_specs=[
            pl.BlockSpec((tm, d), lambda i: (0, 0), pipeline_mode=once),
            pl.BlockSpec((tm, d), lambda i: (jnp.minimum(i + 1, ni - 1), 0)),
            pl.BlockSpec((1, d), lambda i: (0, 0), pipeline_mode=once),
            pl.BlockSpec((d, n), lambda i: (0, 0), pipeline_mode=once),
        ],
        out_specs=pl.BlockSpec((tm, n), lambda i: (i, 0)),
        out_shape=jax.ShapeDtypeStruct((t, n), F32),
        scratch_shapes=[pltpu.VMEM((2, tm, d), BF16)],
        compiler_params=_params("arbitrary"),
        name="in_proj",
    )(x2d, x2d, g, w_p)


def _head_pair_norm(v, w, blockdiag):
    ss = _dot_exact_lhs(_split2(v * v), blockdiag)
    return v * lax.rsqrt(ss * (1.0 / HEAD_DIM) + EPS) * w


def _cast_side_streams(refs, n_side):
    for k in range(n_side):
        refs[n_side + 1 + k][...] = refs[k][...].astype(BF16)
    return refs[n_side]


def _side_specs(side, n_steps, step_index):
    specs, shapes = [], []
    for w in side:
        rows, cols = w.shape
        specs.append(pl.BlockSpec((rows // n_steps, cols), step_index))
        shapes.append(jax.ShapeDtypeStruct((rows, cols), BF16))
    return specs, shapes


def _attn_body(n_side, sinks_ref, q_ref, kvc_ref, kvp_ref, qw_ref, kw_ref, onw_ref, *rest):
    o_ref = _cast_side_streams(rest, n_side)
    acc_ref = rest[-1]
    i = pl.program_id(1)
    tq = q_ref.shape[0]
    nsub = tq // WINDOW
    nk = tq + WINDOW
    lo_lanes = _iota((1, LANES), 1) < HEAD_DIM
    blockdiag = jnp.where(
        _iota((LANES, LANES), 0) // HEAD_DIM == _iota((LANES, LANES), 1) // HEAD_DIM, 1.0, 0.0
    ).astype(BF16)

    kv_all = jnp.concatenate([kvp_ref[...], kvc_ref[...]], axis=0)
    k_all = _head_pair_norm(kv_all[:, :KV_WIDTH], kw_ref[...], blockdiag)
    v_all = kv_all[:, KV_WIDTH:]
    kt = k_all.T
    zero_half = jnp.zeros((HEAD_DIM, nk), F32)
    kt_var = [
        [jnp.concatenate([kt[j * HEAD_DIM:(j + 1) * HEAD_DIM], zero_half], axis=0).astype(BF16),
         jnp.concatenate([zero_half, kt[j * HEAD_DIM:(j + 1) * HEAD_DIM]], axis=0).astype(BF16)]
        for j in range(2)
    ]
    v_sw = pltpu.roll(v_all, HEAD_DIM, axis=1)
    v_var = [
        [jnp.where(lo_lanes, v_all, 0.0), jnp.where(lo_lanes, 0.0, v_sw)],
        [jnp.where(lo_lanes, v_sw, 0.0), jnp.where(lo_lanes, 0.0, v_all)],
    ]

    qn = [
        _head_pair_norm(q_ref[:, c * LANES:(c + 1) * LANES], qw_ref[...], blockdiag).astype(BF16)
        for c in range(ATTN_WIDTH // LANES)
    ]

    rows = 4 * WINDOW
    qi = _iota((rows, 2 * WINDOW), 0) % WINDOW
    kj = _iota((rows, 2 * WINDOW), 1)
    band = (kj > qi) & (kj <= qi + WINDOW)
    first_min = jnp.where(i == 0, WINDOW, 0)
    band_first = band & (kj >= first_min)
    sink_col = _iota((WINDOW, 2 * WINDOW), 1) == 0
    fill = [[jnp.concatenate([jnp.where(sink_col, sinks_ref[8 * j + 2 * c + par], NEG_BIG) for c in range(4)], axis=0)
             for par in range(2)] for j in range(2)]
    not_first_key = _iota((2 * WINDOW, LANES), 0) > 0
    lo_band = _iota((2 * WINDOW, LANES), 1) < HEAD_DIM
    ones_half = [jnp.where(lo_band, 1.0, 0.0).astype(BF16), jnp.where(lo_band, 0.0, 1.0).astype(BF16)]

    for s in range(nsub):
        mask = band_first if s == 0 else band
        keys = slice(s * WINDOW, s * WINDOW + 2 * WINDOW)
        for j in range(2):
            qs = jnp.concatenate([qn[4 * j + c][s * WINDOW:(s + 1) * WINDOW] for c in range(4)], axis=0)
            num = None
            den = None
            for par in range(2):
                sc = jnp.where(mask, _dot(qs, kt_var[j][par][:, keys]), fill[j][par])
                m = jnp.max(sc, axis=-1, keepdims=True)
                p = jnp.exp(sc - m).astype(BF16)
                v_band = jnp.where(not_first_key, v_var[j][par][keys], 0.0).astype(BF16)
                pv = _dot(p, v_band)
                ps = _dot(p, ones_half[par])
                num = pv if num is None else num + pv
                den = ps if den is None else den + ps
            out = num / den
            for c in range(4):
                acc_ref[s * WINDOW:(s + 1) * WINDOW, (4 * j + c) * LANES:(4 * j + c + 1) * LANES] = (
                    out[c * WINDOW:(c + 1) * WINDOW])

    a = acc_ref[...]
    ms = jnp.mean(a * a, axis=-1, keepdims=True)
    o_ref[...] = (a * lax.rsqrt(ms + EPS) * onw_ref[...]).astype(BF16)


def attention(proj, sinks, qw2, kw2, onw, batch, seq, side=()):
    t = proj.shape[0]
    tq = min(256, seq)
    nq = seq // tq
    sub = tq // WINDOW
    kv_col = COL_K // (2 * KV_WIDTH)
    side_specs, side_shapes = _side_specs(side, batch * nq, lambda b, i, s: (b * nq + i, 0))
    grid_spec = pltpu.PrefetchScalarGridSpec(
        num_scalar_prefetch=1,
        grid=(batch, nq),
        in_specs=[
            pl.BlockSpec((tq, ATTN_WIDTH), lambda b, i, s: (b * nq + i, COL_Q // ATTN_WIDTH)),
            pl.BlockSpec((tq, 2 * KV_WIDTH), lambda b, i, s: (b * nq + i, kv_col)),
            pl.BlockSpec((WINDOW, 2 * KV_WIDTH),
                         lambda b, i, s: (b * nq * sub + jnp.maximum(i * sub - 1, 0), kv_col)),
            pl.BlockSpec((1, LANES), lambda b, i, s: (0, 0)),
            pl.BlockSpec((1, LANES), lambda b, i, s: (0, 0)),
            pl.BlockSpec((1, ATTN_WIDTH), lambda b, i, s: (0, 0)),
        ] + side_specs,
        out_specs=[pl.BlockSpec((tq, ATTN_WIDTH), lambda b, i, s: (b * nq + i, 0))] + side_specs,
        scratch_shapes=[pltpu.VMEM((tq, ATTN_WIDTH), F32)],
    )
    outs = pl.pallas_call(
        functools.partial(_attn_body, len(side)),
        grid_spec=grid_spec,
        out_shape=[jax.ShapeDtypeStruct((t, ATTN_WIDTH), BF16)] + side_shapes,
        compiler_params=_params("arbitrary", "arbitrary"),
        name="attention",
    )(sinks, proj, proj, proj, qw2, kw2, onw, *side)
    return outs[0], outs[1:]


def _ssd_body(n_side, z_ref, xs_ref, bc_ref, dt_ref, cwx_ref, cwb_ref, cbx_ref, cbb_ref, dtb_ref, alog_ref,
              dsk_ref, nw_ref, *rest):
    o_ref = _cast_side_streams(rest, n_side)
    xpx_ref, xpb_ref, st_ref, y_ref = rest[-4:]
    i = pl.program_id(1)
    ts = xs_ref.shape[0]
    nch = ts // SSM_CHUNK
    gw = SSM_WIDTH // SSM_GROUPS
    pad = SUBLANES

    @pl.when(i == 0)
    def _():
        xpx_ref[0:pad, :] = jnp.zeros((pad, SSM_WIDTH), F32)
        xpb_ref[0:pad, :] = jnp.zeros((pad, 2 * BC_WIDTH), F32)
        st_ref[...] = jnp.zeros(st_ref.shape, F32)

    xpx_ref[pad:pad + ts, :] = xs_ref[...]
    xpb_ref[pad:pad + ts, :] = bc_ref[...]

    def conv_silu(xp_ref, w_ref, b_ref):
        acc = b_ref[...] + w_ref[SSM_CONV - 1:SSM_CONV, :] * xp_ref[pad:pad + ts, :]
        for k in range(SSM_CONV - 1):
            off = pad - (SSM_CONV - 1) + k
            acc = acc + w_ref[k:k + 1, :] * xp_ref[off:off + ts, :]
        return acc * _sigmoid(acc)

    xc = conv_silu(xpx_ref, cwx_ref, cbx_ref)
    bcc = conv_silu(xpb_ref, cwb_ref, cbb_ref)
    xpx_ref[0:pad, :] = xs_ref[ts - pad:ts, :]
    xpb_ref[0:pad, :] = bc_ref[ts - pad:ts, :]

    dtr = dt_ref[...] + dtb_ref[...]
    dt = jnp.maximum(dtr, 0.0) + jnp.log(1.0 + jnp.exp(-jnp.abs(dtr)))
    a = -jnp.exp(alog_ref[...])
    da = dt * a

    ri = _iota((ts, ts), 0)
    ci = _iota((ts, ts), 1)
    same = (ri // SSM_CHUNK) == (ci // SSM_CHUNK)
    tri_blk = jnp.where(same & (ci <= ri), 1.0, 0.0).astype(BF16)
    all_blk = jnp.where(same, 1.0, 0.0).astype(BF16)
    da3 = _split3(da)
    cs = _dot(tri_blk, da3[0]) + _dot(tri_blk, da3[1]) + _dot(tri_blk, da3[2])
    cl = _dot(all_blk, da3[0]) + _dot(all_blk, da3[1]) + _dot(all_blk, da3[2])
    ecs = jnp.exp(cs)
    dte = jnp.exp(cl - cs)

    expand = jnp.where(_iota((LANES, SSM_WIDTH), 1) // HEAD_DIM == _iota((LANES, SSM_WIDTH), 0),
                       1.0, 0.0).astype(BF16)
    dt_e = _dot_exact_lhs(_split2(dt), expand)
    dtd_e = _dot_exact_lhs(_split2(dt * dte), expand)
    ecs_e = _dot_exact_lhs(_split2(ecs), expand)
    xf = xc * dt_e
    xd = (xc * dtd_e).astype(BF16)

    lo_lanes = _iota((1, LANES), 1) < HEAD_DIM
    li = _iota((SSM_CHUNK, SSM_CHUNK), 0)
    lj = _iota((SSM_CHUNK, SSM_CHUNK), 1)
    tril = lj <= li
    upper = jnp.where(li <= lj, 1.0, 0.0).astype(BF16)

    for c in range(nch):
        r0 = c * SSM_CHUNK
        da_t = da[r0:r0 + SSM_CHUNK].T
        cs_t = _dot_exact_lhs(_split3(da_t), upper)
        cs_c = cs[r0:r0 + SSM_CHUNK]
        for g in range(SSM_GROUPS):
            b_c = bcc[r0:r0 + SSM_CHUNK, g * SSM_STATE:(g + 1) * SSM_STATE]
            c_c = bcc[r0:r0 + SSM_CHUNK, BC_WIDTH + g * SSM_STATE:BC_WIDTH + (g + 1) * SSM_STATE].astype(BF16)
            b_t = b_c.T.astype(BF16)
            cb = _dot(c_c, b_t)
            st = st_ref[g]
            y_off = _dot(c_c, st.astype(BF16)) * ecs_e[r0:r0 + SSM_CHUNK, g * gw:(g + 1) * gw]
            s_new = _dot(b_t, xd[r0:r0 + SSM_CHUNK, g * gw:(g + 1) * gw])
            parts = []
            for p in range(4):
                h0 = 8 * g + 2 * p
                ms = []
                for h in (h0, h0 + 1):
                    diff = cs_c[:, h:h + 1] - cs_t[h:h + 1, :]
                    ms.append((cb * jnp.exp(jnp.where(tril, diff, NEG_BIG))).astype(BF16))
                m2 = jnp.concatenate(ms, axis=1)
                x2 = xf[r0:r0 + SSM_CHUNK, h0 * HEAD_DIM:(h0 + 2) * HEAD_DIM]
                xbd = jnp.concatenate([jnp.where(lo_lanes, x2, 0.0), jnp.where(lo_lanes, 0.0, x2)],
                                      axis=0).astype(BF16)
                parts.append(_dot(m2, xbd))
            y_ref[r0:r0 + SSM_CHUNK, g * gw:(g + 1) * gw] = jnp.concatenate(parts, axis=1) + y_off
            dec = ecs_e[r0 + SSM_CHUNK - 1:r0 + SSM_CHUNK, g * gw:(g + 1) * gw]
            st_ref[g] = st * dec + s_new

    y = y_ref[...] + xc * dsk_ref[...]
    z = z_ref[...]
    gated = y * (z * _sigmoid(z))
    outs = []
    for g in range(SSM_GROUPS):
        gg = gated[:, g * gw:(g + 1) * gw]
        ms = jnp.mean(gg * gg, axis=-1, keepdims=True)
        outs.append(gg * lax.rsqrt(ms + EPS) * nw_ref[:, g * gw:(g + 1) * gw])
    o_ref[...] = jnp.concatenate(outs, axis=1).astype(BF16)


def ssd(proj, cwx, cwb, cbx, cbb, dtb, alog, dsk, nw, batch, seq, side=()):
    t = proj.shape[0]
    ts = min(256, seq)
    ns = seq // ts
    row = lambda b, i: b * ns + i
    const = lambda b, i: (0, 0)
    side_specs, side_shapes = _side_specs(side, batch * ns, lambda b, i: (row(b, i), 0))
    outs = pl.pallas_call(
        functools.partial(_ssd_body, len(side)),
        grid=(batch, ns),
        in_specs=[
            pl.BlockSpec((ts, SSM_WIDTH), lambda b, i: (row(b, i), COL_Z // SSM_WIDTH)),
            pl.BlockSpec((ts, SSM_WIDTH), lambda b, i: (row(b, i), COL_XBC // SSM_WIDTH)),
            pl.BlockSpec((ts, 2 * BC_WIDTH), lambda b, i: (row(b, i), (COL_XBC + SSM_WIDTH) // (2 * BC_WIDTH))),
            pl.BlockSpec((ts, LANES), lambda b, i: (row(b, i), COL_DT // LANES)),
            pl.BlockSpec((SSM_CONV, SSM_WIDTH), const),
            pl.BlockSpec((SSM_CONV, 2 * BC_WIDTH), const),
            pl.BlockSpec((1, SSM_WIDTH), const),
            pl.BlockSpec((1, 2 * BC_WIDTH), const),
            pl.BlockSpec((1, LANES), const),
            pl.BlockSpec((1, LANES), const),
            pl.BlockSpec((1, SSM_WIDTH), const),
            pl.BlockSpec((1, SSM_WIDTH), const),
        ] + side_specs,
        out_specs=[pl.BlockSpec((ts, SSM_WIDTH), lambda b, i: (row(b, i), 0))] + side_specs,
        out_shape=[jax.ShapeDtypeStruct((t, SSM_WIDTH), BF16)] + side_shapes,
        scratch_shapes=[
            pltpu.VMEM((ts + SUBLANES, SSM_WIDTH), F32),
            pltpu.VMEM((ts + SUBLANES, 2 * BC_WIDTH), F32),
            pltpu.VMEM((SSM_GROUPS, SSM_STATE, SSM_WIDTH // SSM_GROUPS), F32),
            pltpu.VMEM((ts, SSM_WIDTH), F32),
        ],
        compiler_params=_params("arbitrary", "arbitrary"),
        name="ssd",
    )(proj, proj, proj, proj, cwx, cwb, cbx, cbb, dtb, alog, dsk, nw, *side)
    return outs[0], outs[1:]


def _out_body(x_ref, a_ref, s_ref, wo_ref, fg_ref, wr_ref, br_ref, h_ref, hf_ref, idx_ref, gate_ref):
    tm = x_ref.shape[0]
    h = x_ref[...] + _dot(a_ref[...], wo_ref[0:ATTN_WIDTH, :]) + _dot(s_ref[...], wo_ref[ATTN_WIDTH:, :])
    h_ref[...] = h
    ms = jnp.mean(h * h, axis=-1, keepdims=True)
    hf = h * lax.rsqrt(ms + EPS) * fg_ref[...]
    hi, lo = _split2(hf)
    hi32 = hi.astype(F32)
    hf_ref[...] = (pltpu.bitcast(hi32[:, D_MODEL // 2:], jnp.uint32)
                   | (pltpu.bitcast(hi32[:, :D_MODEL // 2], jnp.uint32) >> 16))

    whi, wlo = _split2(wr_ref[...])
    hw = _dot(hi, jnp.concatenate([whi, wlo], axis=1))
    logits = hw[:, :LANES] + hw[:, LANES:] + _dot(lo, whi) + br_ref[...]
    lane = _iota((tm, LANES), 1)
    vals, idxs = [], []
    for _ in range(TOP_K):
        m = jnp.max(logits, axis=-1, keepdims=True)
        am = jnp.min(jnp.where(logits == m, lane, LANES), axis=-1, keepdims=True)
        vals.append(m)
        idxs.append(am)
        logits = jnp.where(lane == am, NEG_BIG * 2.0, logits)
    es = [jnp.exp(v - vals[0]) for v in vals]
    tot = es[0] + es[1] + es[2] + es[3]
    idx_out = jnp.zeros((tm, LANES), jnp.int32)
    gate_out = jnp.zeros((tm, LANES), F32)
    for k in range(TOP_K):
        idx_out = jnp.where(lane == k, idxs[k], idx_out)
        gate_out = jnp.where(lane == k, es[k] / tot, gate_out)
    idx_ref[...] = idx_out
    gate_ref[...] = gate_out


def out_proj_router(x2d, attn, ssm, wo, fg, wr, br):
    t, d = x2d.shape
    tm = min(512, t)
    const = lambda i: (0, 0)
    once = pl.Buffered(1)
    return pl.pallas_call(
        _out_body,
        grid=(t // tm,),
        in_specs=[
            pl.BlockSpec((tm, d), lambda i: (i, 0)),
            pl.BlockSpec((tm, ATTN_WIDTH), lambda i: (i, 0)),
            pl.BlockSpec((tm, SSM_WIDTH), lambda i: (i, 0)),
            pl.BlockSpec((d, d), const, pipeline_mode=once),
            pl.BlockSpec((1, d), const, pipeline_mode=once),
            pl.BlockSpec((d, LANES), const, pipeline_mode=once),
            pl.BlockSpec((1, LANES), const, pipeline_mode=once),
        ],
        out_specs=[
            pl.BlockSpec((tm, d), lambda i: (i, 0)),
            pl.BlockSpec((tm, d // 2), lambda i: (i, 0)),
            pl.BlockSpec((tm, LANES), lambda i: (i, 0)),
            pl.BlockSpec((tm, LANES), lambda i: (i, 0)),
        ],
        out_shape=[
            jax.ShapeDtypeStruct((t, d), F32),
            jax.ShapeDtypeStruct((t, d // 2), jnp.uint32),
            jax.ShapeDtypeStruct((t, LANES), jnp.int32),
            jax.ShapeDtypeStruct((t, LANES), F32),
        ],
        compiler_params=_params("arbitrary"),
        name="out_proj_router",
    )(x2d, attn, ssm, wo, fg, wr, br)


MOE_ROWS = 512
FF_TILE = 1024
N_FF_TILES = D_FF // FF_TILE


def _expert_body(be_ref, nv_ref, tok_ref, tokn_ref, dst_ref, dstp_ref, hf_ref, wg_ref, wu_ref, wd_ref, bg_ref, bu_ref,
                 bd_ref, yk_ref, xrow_ref, x_ref, h_ref, y_ref, gu_ref, dn_ref, row_sem, out_sem, gu_sem, dn_sem):
    b = pl.program_id(0)
    nv = nv_ref[0]
    tm = h_ref.shape[0]
    half = D_MODEL // 2

    def row_copy(table_ref, r):
        return pltpu.make_async_copy(hf_ref.at[pl.ds(table_ref[0, 0, r], 1)], xrow_ref.at[pl.ds(r, 1)], row_sem)

    def rows_wait():
        pltpu.make_async_copy(hf_ref.at[pl.ds(0, tm)], xrow_ref, row_sem).wait()

    def unpack_rows(slot):
        w = xrow_ref[...]
        x_ref[slot, :, 0:half] = pltpu.bitcast(w << 16, F32).astype(BF16)
        x_ref[slot, :, half:] = pltpu.bitcast(w & jnp.uint32(0xFFFF0000), F32).astype(BF16)

    def out_copy(table_ref, r, ys):
        return pltpu.make_async_copy(y_ref.at[ys, pl.ds(r, 1)], yk_ref.at[pl.ds(table_ref[0, 0, r], 1)],
                                     out_sem.at[ys])

    def out_wait(ys):
        pltpu.make_async_copy(y_ref.at[ys], yk_ref.at[pl.ds(0, tm)], out_sem.at[ys]).wait()

    n_chunks = 2 * N_FF_TILES

    def chunk_copies(e, c):
        j = c % N_FF_TILES
        cols = pl.ds(j * FF_TILE, FF_TILE)
        if c < N_FF_TILES:
            return (pltpu.make_async_copy(wg_ref.at[e, :, cols], gu_ref.at[j, 0], gu_sem.at[j]),
                    pltpu.make_async_copy(wu_ref.at[e, :, cols], gu_ref.at[j, 1], gu_sem.at[j]))
        return (pltpu.make_async_copy(wd_ref.at[e, :, cols], dn_ref.at[j], dn_sem.at[j]),)

    def start_chunk(e, c):
        for cp in chunk_copies(e, c):
            cp.start()

    def next_chunk(e, c):
        for cp in chunk_copies(e, c):
            cp.wait()
        if c == 0:
            start_chunk(e, n_chunks - 1)
        else:
            @pl.when(b + 1 < nv)
            def _():
                start_chunk(be_ref[b + 1], c - 1)

    per_batch = tm // N_FF_TILES

    @pl.when(b < nv)
    def _():
        e = be_ref[b]
        slot = b % 2
        prev = 1 - slot

        @pl.when(b == 0)
        def _():
            def issue(r, carry):
                row_copy(tok_ref, r).start()
                return carry
            lax.fori_loop(0, tm, issue, 0, unroll=8)
            for c in range(n_chunks - 1):
                start_chunk(e, c)
            y_ref[1] = jnp.zeros(y_ref.shape[1:], F32)
            rows_wait()
            unpack_rows(0)

        x = x_ref[slot]
        for j in range(N_FF_TILES):
            next_chunk(e, j)
            cols = slice(j * FF_TILE, (j + 1) * FF_TILE)
            g_acc = _dot(x, gu_ref[j, 0])
            u_acc = _dot(x, gu_ref[j, 1])
            for r in range(j * per_batch, (j + 1) * per_batch):
                row_copy(tokn_ref, r).start()
                out_copy(dstp_ref, r, prev).start()
            gl = jnp.minimum(g_acc + bg_ref[0, :, cols], SWIGLU_LIMIT)
            li = jnp.clip(u_acc + bu_ref[0, :, cols], -SWIGLU_LIMIT, SWIGLU_LIMIT)
            h_ref[:, cols] = (gl * _sigmoid(SWIGLU_ALPHA * gl) * (li + 1.0)).astype(BF16)

        @pl.when(b > 0)
        def _():
            out_wait(slot)

        h = h_ref[...]
        for j in range(N_FF_TILES):
            next_chunk(e, N_FF_TILES + j)
            if j == N_FF_TILES - 1:
                rows_wait()
                unpack_rows(1 - slot)
            cols = slice(j * FF_TILE, (j + 1) * FF_TILE)
            y_ref[slot, :, cols] = _dot(h, dn_ref[j]) + bd_ref[0, :, cols]

        @pl.when(b + 1 == nv)
        def _():
            def issue(r, carry):
                out_copy(dst_ref, r, slot).start()
                return carry
            lax.fori_loop(0, tm, issue, 0, unroll=8)
            out_wait(slot)
            out_wait(prev)


def experts(block_expert, n_valid, slot_tok, slot_dst, hf_packed, wg, bg, wu, bu, wd, bd, n_out_rows):
    n_blocks = block_expert.shape[0]
    tm = MOE_ROWS
    d = D_MODEL

    def blk(b, nv):
        return jnp.minimum(b, nv[0] - 1)

    any_spec = pl.BlockSpec(memory_space=pl.ANY)
    grid_spec = pltpu.PrefetchScalarGridSpec(
        num_scalar_prefetch=2,
        grid=(n_blocks,),
        in_specs=[
            pl.BlockSpec((1, 1, tm), lambda b, be, nv: (blk(b, nv), 0, 0), memory_space=pltpu.SMEM),
            pl.BlockSpec((1, 1, tm), lambda b, be, nv: (blk(b + 1, nv), 0, 0), memory_space=pltpu.SMEM),
            pl.BlockSpec((1, 1, tm), lambda b, be, nv: (blk(b, nv) + 1, 0, 0), memory_space=pltpu.SMEM),
            pl.BlockSpec((1, 1, tm), lambda b, be, nv: (blk(b, nv), 0, 0), memory_space=pltpu.SMEM),
            any_spec, any_spec, any_spec, any_spec,
            pl.BlockSpec((1, 1, D_FF), lambda b, be, nv: (be[blk(b, nv)], 0, 0)),
            pl.BlockSpec((1, 1, D_FF), lambda b, be, nv: (be[blk(b, nv)], 0, 0)),
            pl.BlockSpec((1, 1, d), lambda b, be, nv: (be[blk(b, nv)], 0, 0)),
        ],
        out_specs=any_spec,
        scratch_shapes=[
            pltpu.VMEM((tm, d // 2), jnp.uint32),
            pltpu.VMEM((2, tm, d), BF16),
            pltpu.VMEM((tm, D_FF), BF16),
            pltpu.VMEM((2, tm, d), F32),
            pltpu.VMEM((2, 2, d, FF_TILE), BF16),
            pltpu.VMEM((2, D_FF, FF_TILE), BF16),
            pltpu.SemaphoreType.DMA,
            pltpu.SemaphoreType.DMA((2,)),
            pltpu.SemaphoreType.DMA((2,)),
            pltpu.SemaphoreType.DMA((2,)),
        ],
    )
    return pl.pallas_call(
        _expert_body,
        grid_spec=grid_spec,
        out_shape=jax.ShapeDtypeStruct((n_out_rows, d), F32),
        compiler_params=_params("arbitrary"),
        name="experts",
    )(block_expert, n_valid, slot_tok, slot_tok, slot_dst, slot_dst, hf_packed, wg, wu, wd, bg, bu, bd)


COMBINE_ROWS = 256


def _combine_body(h_ref, gate_ref, yk_ref, o_ref):
    tm = h_ref.shape[0]
    gates = gate_ref[...]
    acc = h_ref[...]
    for k in range(TOP_K):
        acc = acc + gates[:, k:k + 1] * yk_ref[k * tm:(k + 1) * tm, :]
    o_ref[...] = acc


def combine(h, gates, yk):
    t, d = h.shape
    tm = min(COMBINE_ROWS, t)
    return pl.pallas_call(
        _combine_body,
        grid=(t // tm,),
        in_specs=[
            pl.BlockSpec((tm, d), lambda i: (i, 0)),
            pl.BlockSpec((tm, LANES), lambda i: (i, 0)),
            pl.BlockSpec((TOP_K * tm, d), lambda i: (i, 0)),
        ],
        out_specs=pl.BlockSpec((tm, d), lambda i: (i, 0)),
        out_shape=jax.ShapeDtypeStruct((t, d), F32),
        compiler_params=_params("arbitrary"),
        name="combine",
    )(h, gates, yk)


def routing_tables(top_idx, t):
    n_assign = t * TOP_K
    tm_c = min(COMBINE_ROWS, t)
    e_flat = top_idx.reshape(-1).astype(jnp.int32)
    _, order = lax.sort((e_flat, jnp.arange(n_assign, dtype=jnp.int32)), num_keys=1, is_stable=True)
    experts_iota = jnp.arange(N_EXPERTS, dtype=jnp.int32)
    counts = jnp.sum((e_flat[None, :] == experts_iota[:, None]).astype(jnp.int32), axis=1)
    starts = jnp.cumsum(counts) - counts
    padded = (counts + MOE_ROWS - 1) // MOE_ROWS * MOE_ROWS
    pends = jnp.cumsum(padded)
    pstarts = pends - padded
    n_blocks = -(-n_assign // MOE_ROWS) + N_EXPERTS
    block_start = jnp.arange(n_blocks, dtype=jnp.int32) * MOE_ROWS
    block_expert = jnp.minimum(
        jnp.sum((pends[None, :] <= block_start[:, None]).astype(jnp.int32), axis=1), N_EXPERTS - 1)
    n_valid = (pends[-1:] // MOE_ROWS).astype(jnp.int32)

    row = jnp.arange(MOE_ROWS, dtype=jnp.int32)[None, :]
    e_b = block_expert[:, None]
    local = jnp.arange(n_blocks, dtype=jnp.int32)[:, None] * MOE_ROWS + row - pstarts[e_b]
    used = (local >= 0) & (local < counts[e_b])
    asg = order[jnp.clip(starts[e_b] + local, 0, n_assign - 1)]
    tok = asg // TOP_K
    dst = (tok // tm_c) * (TOP_K * tm_c) + (asg % TOP_K) * tm_c + tok % tm_c
    slot_tok = jnp.where(used, tok, 0).reshape(n_blocks, 1, MOE_ROWS)
    slot_dst = jnp.concatenate([n_assign + row, jnp.where(used, dst, n_assign + row)], axis=0)
    return block_expert, n_valid, slot_tok, slot_dst.reshape(n_blocks + 1, 1, MOE_ROWS)


def _pad_lanes(v, width, value=0.0):
    return jnp.pad(v, ((0, 0), (0, width - v.shape[1])), constant_values=value)


def hybrid_layer(x, attn_norm_w, w_in, q_norm_w, k_norm_w, attn_sinks, conv_w, conv_b, dt_bias, a_log, d_skip,
                 ssm_norm_w, attn_out_norm_w, w_out, ffn_norm_w, w_router, b_router, w_gate, b_gate, w_up, b_up,
                 w_down, b_down):
    batch, seq, d = x.shape
    t = batch * seq
    x2d = x.reshape(t, d)

    q_w, k_w, v_w, z_w, xbc_w, dt_w = jnp.split(
        w_in, [ATTN_WIDTH, ATTN_WIDTH + KV_WIDTH, ATTN_WIDTH + 2 * KV_WIDTH,
               ATTN_WIDTH + 2 * KV_WIDTH + SSM_WIDTH, ATTN_WIDTH + 2 * KV_WIDTH + SSM_WIDTH + CONV_WIDTH], axis=1)
    w_p = jnp.concatenate([q_w, z_w, xbc_w, k_w, v_w, dt_w], axis=1)
    w_p = _pad_lanes(w_p, PROJ_WIDTH).astype(BF16)
    proj = in_proj(x2d, attn_norm_w.reshape(1, d), w_p)

    qw2 = (jnp.tile(q_norm_w, 2) * (HEAD_DIM ** -0.5)).reshape(1, LANES)
    kw2 = jnp.tile(k_norm_w, 2).reshape(1, LANES)
    attn, (wg16, wu16) = attention(
        proj, attn_sinks, qw2, kw2, attn_out_norm_w.reshape(1, ATTN_WIDTH), batch, seq,
        side=(w_gate.reshape(N_EXPERTS * D_MODEL, D_FF), w_up.reshape(N_EXPERTS * D_MODEL, D_FF)))

    ssm, (wd16,) = ssd(
        proj,
        conv_w[:, :SSM_WIDTH], conv_w[:, SSM_WIDTH:],
        conv_b[:SSM_WIDTH].reshape(1, -1), conv_b[SSM_WIDTH:].reshape(1, -1),
        _pad_lanes(dt_bias.reshape(1, -1), LANES), _pad_lanes(a_log.reshape(1, -1), LANES),
        jnp.repeat(d_skip, HEAD_DIM).reshape(1, SSM_WIDTH), ssm_norm_w.reshape(1, SSM_WIDTH),
        batch, seq, side=(w_down.reshape(N_EXPERTS * D_FF, D_MODEL),))

    h, hf_packed, idx_pad, gate_pad = out_proj_router(
        x2d, attn, ssm, w_out.astype(BF16), ffn_norm_w.reshape(1, d),
        _pad_lanes(w_router, LANES), _pad_lanes(b_router.reshape(1, -1), LANES, NEG_BIG))

    block_expert, n_valid, slot_tok, slot_dst = routing_tables(idx_pad[:, :TOP_K], t)
    yk = experts(
        block_expert, n_valid, slot_tok, slot_dst, hf_packed,
        wg16.reshape(N_EXPERTS, D_MODEL, D_FF), b_gate.reshape(N_EXPERTS, 1, D_FF),
        wu16.reshape(N_EXPERTS, D_MODEL, D_FF), b_up.reshape(N_EXPERTS, 1, D_FF),
        wd16.reshape(N_EXPERTS, D_FF, D_MODEL), b_down.reshape(N_EXPERTS, 1, D_MODEL),
        t * TOP_K + MOE_ROWS)

    out = combine(h, gate_pad, yk)
    return out.reshape(batch, seq, d)


def kernel(x, attn_norm_w, w_in, q_norm_w, k_norm_w, attn_sinks, conv_w, conv_b, dt_bias, a_log, d_skip, ssm_norm_w,
           attn_out_norm_w, w_out, ffn_norm_w, w_router, b_router, w_gate, b_gate, w_up, b_up, w_down, b_down):
    for i in range(attn_norm_w.shape[0]):
        x = hybrid_layer(x, attn_norm_w[i], w_in[i], q_norm_w[i], k_norm_w[i], attn_sinks[i], conv_w[i], conv_b[i],
                         dt_bias[i], a_log[i], d_skip[i], ssm_norm_w[i], attn_out_norm_w[i], w_out[i], ffn_norm_w[i],
                         w_router[i], b_router[i], w_gate[i], b_gate[i], w_up[i], b_up[i], w_down[i], b_down[i])
    return x
```

```python
import functools

import jax
import jax.numpy as jnp
from jax import lax
from jax.experimental import pallas as pl
from jax.experimental.pallas import tpu as pltpu

F32 = jnp.float32
BF16 = jnp.bfloat16

D_MODEL = 2048
HEAD_DIM = 64
ATTN_WIDTH = 1024
KV_WIDTH = 128
WINDOW = 128
SSM_WIDTH = 1024
SSM_HEADS = 16
SSM_GROUPS = 2
SSM_STATE = 128
SSM_CONV = 4
SSM_CHUNK = 128
BC_WIDTH = SSM_GROUPS * SSM_STATE
CONV_WIDTH = SSM_WIDTH + 2 * BC_WIDTH
N_EXPERTS = 32
TOP_K = 4
D_FF = 2048
SWIGLU_LIMIT = 7.0
SWIGLU_ALPHA = 1.702
EPS = 1e-6

LANES = 128
SUBLANES = 8
VMEM_LIMIT = 56 * 1024 * 1024

PROJ_WIDTH = 4096
COL_Q = 0
COL_Z = ATTN_WIDTH
COL_XBC = COL_Z + SSM_WIDTH
COL_K = COL_XBC + CONV_WIDTH
COL_V = COL_K + KV_WIDTH
COL_DT = COL_V + KV_WIDTH
NEG_BIG = -1e30


def _params(*sem):
    return pltpu.CompilerParams(dimension_semantics=sem, vmem_limit_bytes=VMEM_LIMIT)


def _split2(v):
    hi = v.astype(BF16)
    lo = (v - hi.astype(F32)).astype(BF16)
    return hi, lo


def _split3(v):
    hi = v.astype(BF16)
    r = v - hi.astype(F32)
    mid = r.astype(BF16)
    lo = (r - mid.astype(F32)).astype(BF16)
    return hi, mid, lo


def _dot(a, b):
    return jnp.dot(a, b, preferred_element_type=F32)


def _dot_exact_lhs(parts, m):
    acc = _dot(parts[0], m)
    for p in parts[1:]:
        acc = acc + _dot(p, m)
    return acc


def _sigmoid(v):
    return 1.0 / (1.0 + jnp.exp(-v))


def _iota(shape, dim):
    return lax.broadcasted_iota(jnp.int32, shape, dim)


def _in_proj_body(x0_ref, xn_ref, g_ref, w_ref, o_ref, hn_ref):
    i = pl.program_id(0)

    def norm(x):
        ms = jnp.mean(x * x, axis=-1, keepdims=True)
        return (x * lax.rsqrt(ms + EPS) * g_ref[...]).astype(BF16)

    @pl.when(i == 0)
    def _():
        hn_ref[0] = norm(x0_ref[...])

    slot = i % 2
    hn_ref[1 - slot] = norm(xn_ref[...])
    o_ref[...] = _dot(hn_ref[slot], w_ref[...])


def in_proj(x2d, g, w_p):
    t, d = x2d.shape
    n = w_p.shape[1]
    tm = min(512, t)
    ni = t // tm
    once = pl.Buffered(1)
    return pl.pallas_call(
        _in_proj_body,
        grid=(ni,),
        in_specs=[
            pl.BlockSpec((tm, d), lambda i: (0, 0), pipeline_mode=once),
            pl.BlockSpec((tm, d), lambda i: (jnp.minimum(i + 1, ni - 1), 0)),
            pl.BlockSpec((1, d), lambda i: (0, 0), pipeline_mode=once),
            pl.BlockSpec((d, n), lambda i: (0, 0), pipeline_mode=once),
        ],
        out_specs=pl.BlockSpec((tm, n), lambda i: (i, 0)),
        out_shape=jax.ShapeDtypeStruct((t, n), F32),
        scratch_shapes=[pltpu.VMEM((2, tm, d), BF16)],
        compiler_params=_params("arbitrary"),
        name="in_proj",
    )(x2d, x2d, g, w_p)


def _head_pair_norm(v, w, blockdiag):
    ss = _dot_exact_lhs(_split2(v * v), blockdiag)
    return v * lax.rsqrt(ss * (1.0 / HEAD_DIM) + EPS) * w


def _cast_side_streams(refs, n_side):
    for k in range(n_side):
        refs[n_side + 1 + k][...] = refs[k][...].astype(BF16)
    return refs[n_side]


def _side_specs(side, n_steps, step_index):
    specs, shapes = [], []
    for w in side:
        rows, cols = w.shape
        specs.append(pl.BlockSpec((rows // n_steps, cols), step_index))
        shapes.append(jax.ShapeDtypeStruct((rows, cols), BF16))
    return specs, shapes


def _attn_body(n_side, sinks_ref, q_ref, kvc_ref, kvp_ref, qw_ref, kw_ref, onw_ref, *rest):
    o_ref = _cast_side_streams(rest, n_side)
    acc_ref = rest[-1]
    i = pl.program_id(1)
    tq = q_ref.shape[0]
    nsub = tq // WINDOW
    nk = tq + WINDOW
    lo_lanes = _iota((1, LANES), 1) < HEAD_DIM
    blockdiag = jnp.where(
        _iota((LANES, LANES), 0) // HEAD_DIM == _iota((LANES, LANES), 1) // HEAD_DIM, 1.0, 0.0
    ).astype(BF16)

    kv_all = jnp.concatenate([kvp_ref[...], kvc_ref[...]], axis=0)
    k_all = _head_pair_norm(kv_all[:, :KV_WIDTH], kw_ref[...], blockdiag)
    v_all = kv_all[:, KV_WIDTH:]
    kt = k_all.T
    zero_half = jnp.zeros((HEAD_DIM, nk), F32)
    kt_var = [
        [jnp.concatenate([kt[j * HEAD_DIM:(j + 1) * HEAD_DIM], zero_half], axis=0).astype(BF16),
         jnp.concatenate([zero_half, kt[j * HEAD_DIM:(j + 1) * HEAD_DIM]], axis=0).astype(BF16)]
        for j in range(2)
    ]
    v_sw = pltpu.roll(v_all, HEAD_DIM, axis=1)
    v_var = [
        [jnp.where(lo_lanes, v_all, 0.0), jnp.where(lo_lanes, 0.0, v_sw)],
        [jnp.where(lo_lanes, v_sw, 0.0), jnp.where(lo_lanes, 0.0, v_all)],
    ]

    qn = [
        _head_pair_norm(q_ref[:, c * LANES:(c + 1) * LANES], qw_ref[...], blockdiag).astype(BF16)
        for c in range(ATTN_WIDTH // LANES)
    ]

    rows = 4 * WINDOW
    qi = _iota((rows, 2 * WINDOW), 0) % WINDOW
    kj = _iota((rows, 2 * WINDOW), 1)
    band = (kj > qi) & (kj <= qi + WINDOW)
    first_min = jnp.where(i == 0, WINDOW, 0)
    band_first = band & (kj >= first_min)
    sink_col = _iota((WINDOW, 2 * WINDOW), 1) == 0
    fill = [[jnp.concatenate([jnp.where(sink_col, sinks_ref[8 * j + 2 * c + par], NEG_BIG) for c in range(4)], axis=0)
             for par in range(2)] for j in range(2)]
    not_first_key = _iota((2 * WINDOW, LANES), 0) > 0
    lo_band = _iota((2 * WINDOW, LANES), 1) < HEAD_DIM
    ones_half = [jnp.where(lo_band, 1.0, 0.0).astype(BF16), jnp.where(lo_band, 0.0, 1.0).astype(BF16)]

    for s in range(nsub):
        mask = band_first if s == 0 else band
        keys = slice(s * WINDOW, s * WINDOW + 2 * WINDOW)
        for j in range(2):
            qs = jnp.concatenate([qn[4 * j + c][s * WINDOW:(s + 1) * WINDOW] for c in range(4)], axis=0)
            num = None
            den = None
            for par in range(2):
                sc = jnp.where(mask, _dot(qs, kt_var[j][par][:, keys]), fill[j][par])
                m = jnp.max(sc, axis=-1, keepdims=True)
                p = jnp.exp(sc - m).astype(BF16)
                v_band = jnp.where(not_first_key, v_var[j][par][keys], 0.0).astype(BF16)
                pv = _dot(p, v_band)
                ps = _dot(p, ones_half[par])
                num = pv if num is None else num + pv
                den = ps if den is None else den + ps
            out = num / den
            for c in range(4):
                acc_ref[s * WINDOW:(s + 1) * WINDOW, (4 * j + c) * LANES:(4 * j + c + 1) * LANES] = (
                    out[c * WINDOW:(c + 1) * WINDOW])

    a = acc_ref[...]
    ms = jnp.mean(a * a, axis=-1, keepdims=True)
    o_ref[...] = (a * lax.rsqrt(ms + EPS) * onw_ref[...]).astype(BF16)


def attention(proj, sinks, qw2, kw2, onw, batch, seq, side=()):
    t = proj.shape[0]
    tq = min(256, seq)
    nq = seq // tq
    sub = tq // WINDOW
    kv_col = COL_K // (2 * KV_WIDTH)
    side_specs, side_shapes = _side_specs(side, batch * nq, lambda b, i, s: (b * nq + i, 0))
    grid_spec = pltpu.PrefetchScalarGridSpec(
        num_scalar_prefetch=1,
        grid=(batch, nq),
        in_specs=[
            pl.BlockSpec((tq, ATTN_WIDTH), lambda b, i, s: (b * nq + i, COL_Q // ATTN_WIDTH)),
            pl.BlockSpec((tq, 2 * KV_WIDTH), lambda b, i, s: (b * nq + i, kv_col)),
            pl.BlockSpec((WINDOW, 2 * KV_WIDTH),
                         lambda b, i, s: (b * nq * sub + jnp.maximum(i * sub - 1, 0), kv_col)),
            pl.BlockSpec((1, LANES), lambda b, i, s: (0, 0)),
            pl.BlockSpec((1, LANES), lambda b, i, s: (0, 0)),
            pl.BlockSpec((1, ATTN_WIDTH), lambda b, i, s: (0, 0)),
        ] + side_specs,
        out_specs=[pl.BlockSpec((tq, ATTN_WIDTH), lambda b, i, s: (b * nq + i, 0))] + side_specs,
        scratch_shapes=[pltpu.VMEM((tq, ATTN_WIDTH), F32)],
    )
    outs = pl.pallas_call(
        functools.partial(_attn_body, len(side)),
        grid_spec=grid_spec,
        out_shape=[jax.ShapeDtypeStruct((t, ATTN_WIDTH), BF16)] + side_shapes,
        compiler_params=_params("arbitrary", "arbitrary"),
        name="attention",
    )(sinks, proj, proj, proj, qw2, kw2, onw, *side)
    return outs[0], outs[1:]


def _ssd_body(n_side, z_ref, xs_ref, bc_ref, dt_ref, cwx_ref, cwb_ref, cbx_ref, cbb_ref, dtb_ref, alog_ref,
              dsk_ref, nw_ref, *rest):
    o_ref = _cast_side_streams(rest, n_side)
    xpx_ref, xpb_ref, st_ref, y_ref = rest[-4:]
    i = pl.program_id(1)
    ts = xs_ref.shape[0]
    nch = ts // SSM_CHUNK
    gw = SSM_WIDTH // SSM_GROUPS
    pad = SUBLANES

    @pl.when(i == 0)
    def _():
        xpx_ref[0:pad, :] = jnp.zeros((pad, SSM_WIDTH), F32)
        xpb_ref[0:pad, :] = jnp.zeros((pad, 2 * BC_WIDTH), F32)
        st_ref[...] = jnp.zeros(st_ref.shape, F32)

    xpx_ref[pad:pad + ts, :] = xs_ref[...]
    xpb_ref[pad:pad + ts, :] = bc_ref[...]

    def conv_silu(xp_ref, w_ref, b_ref):
        acc = b_ref[...] + w_ref[SSM_CONV - 1:SSM_CONV, :] * xp_ref[pad:pad + ts, :]
        for k in range(SSM_CONV - 1):
            off = pad - (SSM_CONV - 1) + k
            acc = acc + w_ref[k:k + 1, :] * xp_ref[off:off + ts, :]
        return acc * _sigmoid(acc)

    xc = conv_silu(xpx_ref, cwx_ref, cbx_ref)
    bcc = conv_silu(xpb_ref, cwb_ref, cbb_ref)
    xpx_ref[0:pad, :] = xs_ref[ts - pad:ts, :]
    xpb_ref[0:pad, :] = bc_ref[ts - pad:ts, :]

    dtr = dt_ref[...] + dtb_ref[...]
    dt = jnp.maximum(dtr, 0.0) + jnp.log(1.0 + jnp.exp(-jnp.abs(dtr)))
    a = -jnp.exp(alog_ref[...])
    da = dt * a

    ri = _iota((ts, ts), 0)
    ci = _iota((ts, ts), 1)
    same = (ri // SSM_CHUNK) == (ci // SSM_CHUNK)
    tri_blk = jnp.where(same & (ci <= ri), 1.0, 0.0).astype(BF16)
    all_blk = jnp.where(same, 1.0, 0.0).astype(BF16)
    da3 = _split3(da)
    cs = _dot(tri_blk, da3[0]) + _dot(tri_blk, da3[1]) + _dot(tri_blk, da3[2])
    cl = _dot(all_blk, da3[0]) + _dot(all_blk, da3[1]) + _dot(all_blk, da3[2])
    ecs = jnp.exp(cs)
    dte = jnp.exp(cl - cs)

    expand = jnp.where(_iota((LANES, SSM_WIDTH), 1) // HEAD_DIM == _iota((LANES, SSM_WIDTH), 0),
                       1.0, 0.0).astype(BF16)
    dt_e = _dot_exact_lhs(_split2(dt), expand)
    dtd_e = _dot_exact_lhs(_split2(dt * dte), expand)
    ecs_e = _dot_exact_lhs(_split2(ecs), expand)
    xf = xc * dt_e
    xd = (xc * dtd_e).astype(BF16)

    lo_lanes = _iota((1, LANES), 1) < HEAD_DIM
    li = _iota((SSM_CHUNK, SSM_CHUNK), 0)
    lj = _iota((SSM_CHUNK, SSM_CHUNK), 1)
    tril = lj <= li
    upper = jnp.where(li <= lj, 1.0, 0.0).astype(BF16)

    for c in range(nch):
        r0 = c * SSM_CHUNK
        da_t = da[r0:r0 + SSM_CHUNK].T
        cs_t = _dot_exact_lhs(_split3(da_t), upper)
        cs_c = cs[r0:r0 + SSM_CHUNK]
        for g in range(SSM_GROUPS):
            b_c = bcc[r0:r0 + SSM_CHUNK, g * SSM_STATE:(g + 1) * SSM_STATE]
            c_c = bcc[r0:r0 + SSM_CHUNK, BC_WIDTH + g * SSM_STATE:BC_WIDTH + (g + 1) * SSM_STATE].astype(BF16)
            b_t = b_c.T.astype(BF16)
            cb = _dot(c_c, b_t)
            st = st_ref[g]
            y_off = _dot(c_c, st.astype(BF16)) * ecs_e[r0:r0 + SSM_CHUNK, g * gw:(g + 1) * gw]
            s_new = _dot(b_t, xd[r0:r0 + SSM_CHUNK, g * gw:(g + 1) * gw])
            parts = []
            for p in range(4):
                h0 = 8 * g + 2 * p
                ms = []
                for h in (h0, h0 + 1):
                    diff = cs_c[:, h:h + 1] - cs_t[h:h + 1, :]
                    ms.append((cb * jnp.exp(jnp.where(tril, diff, NEG_BIG))).astype(BF16))
                m2 = jnp.concatenate(ms, axis=1)
                x2 = xf[r0:r0 + SSM_CHUNK, h0 * HEAD_DIM:(h0 + 2) * HEAD_DIM]
                xbd = jnp.concatenate([jnp.where(lo_lanes, x2, 0.0), jnp.where(lo_lanes, 0.0, x2)],
                                      axis=0).astype(BF16)
                parts.append(_dot(m2, xbd))
            y_ref[r0:r0 + SSM_CHUNK, g * gw:(g + 1) * gw] = jnp.concatenate(parts, axis=1) + y_off
            dec = ecs_e[r0 + SSM_CHUNK - 1:r0 + SSM_CHUNK, g * gw:(g + 1) * gw]
            st_ref[g] = st * dec + s_new

    y = y_ref[...] + xc * dsk_ref[...]
    z = z_ref[...]
    gated = y * (z * _sigmoid(z))
    outs = []
    for g in range(SSM_GROUPS):
        gg = gated[:, g * gw:(g + 1) * gw]
        ms = jnp.mean(gg * gg, axis=-1, keepdims=True)
        outs.append(gg * lax.rsqrt(ms + EPS) * nw_ref[:, g * gw:(g + 1) * gw])
    o_ref[...] = jnp.concatenate(outs, axis=1).astype(BF16)


def ssd(proj, cwx, cwb, cbx, cbb, dtb, alog, dsk, nw, batch, seq, side=()):
    t = proj.shape[0]
    ts = min(256, seq)
    ns = seq // ts
    row = lambda b, i: b * ns + i
    const = lambda b, i: (0, 0)
    side_specs, side_shapes = _side_specs(side, batch * ns, lambda b, i: (row(b, i), 0))
    outs = pl.pallas_call(
        functools.partial(_ssd_body, len(side)),
        grid=(batch, ns),
        in_specs=[
            pl.BlockSpec((ts, SSM_WIDTH), lambda b, i: (row(b, i), COL_Z // SSM_WIDTH)),
            pl.BlockSpec((ts, SSM_WIDTH), lambda b, i: (row(b, i), COL_XBC // SSM_WIDTH)),
            pl.BlockSpec((ts, 2 * BC_WIDTH), lambda b, i: (row(b, i), (COL_XBC + SSM_WIDTH) // (2 * BC_WIDTH))),
            pl.BlockSpec((ts, LANES), lambda b, i: (row(b, i), COL_DT // LANES)),
            pl.BlockSpec((SSM_CONV, SSM_WIDTH), const),
            pl.BlockSpec((SSM_CONV, 2 * BC_WIDTH), const),
            pl.BlockSpec((1, SSM_WIDTH), const),
            pl.BlockSpec((1, 2 * BC_WIDTH), const),
            pl.BlockSpec((1, LANES), const),
            pl.BlockSpec((1, LANES), const),
            pl.BlockSpec((1, SSM_WIDTH), const),
            pl.BlockSpec((1, SSM_WIDTH), const),
        ] + side_specs,
        out_specs=[pl.BlockSpec((ts, SSM_WIDTH), lambda b, i: (row(b, i), 0))] + side_specs,
        out_shape=[jax.ShapeDtypeStruct((t, SSM_WIDTH), BF16)] + side_shapes,
        scratch_shapes=[
            pltpu.VMEM((ts + SUBLANES, SSM_WIDTH), F32),
            pltpu.VMEM((ts + SUBLANES, 2 * BC_WIDTH), F32),
            pltpu.VMEM((SSM_GROUPS, SSM_STATE, SSM_WIDTH // SSM_GROUPS), F32),
            pltpu.VMEM((ts, SSM_WIDTH), F32),
        ],
        compiler_params=_params("arbitrary", "arbitrary"),
        name="ssd",
    )(proj, proj, proj, proj, cwx, cwb, cbx, cbb, dtb, alog, dsk, nw, *side)
    return outs[0], outs[1:]


def _out_body(x_ref, a_ref, s_ref, wo_ref, fg_ref, wr_ref, br_ref, h_ref, hf_ref, idx_ref, gate_ref):
    tm = x_ref.shape[0]
    h = x_ref[...] + _dot(a_ref[...], wo_ref[0:ATTN_WIDTH, :]) + _dot(s_ref[...], wo_ref[ATTN_WIDTH:, :])
    h_ref[...] = h
    ms = jnp.mean(h * h, axis=-1, keepdims=True)
    hf = h * lax.rsqrt(ms + EPS) * fg_ref[...]
    hi, lo = _split2(hf)
    hi32 = hi.astype(F32)
    hf_ref[...] = (pltpu.bitcast(hi32[:, D_MODEL // 2:], jnp.uint32)
                   | (pltpu.bitcast(hi32[:, :D_MODEL // 2], jnp.uint32) >> 16))

    whi, wlo = _split2(wr_ref[...])
    hw = _dot(hi, jnp.concatenate([whi, wlo], axis=1))
    logits = hw[:, :LANES] + hw[:, LANES:] + _dot(lo, whi) + br_ref[...]
    lane = _iota((tm, LANES), 1)
    vals, idxs = [], []
    for _ in range(TOP_K):
        m = jnp.max(logits, axis=-1, keepdims=True)
        am = jnp.min(jnp.where(logits == m, lane, LANES), axis=-1, keepdims=True)
        vals.append(m)
        idxs.append(am)
        logits = jnp.where(lane == am, NEG_BIG * 2.0, logits)
    es = [jnp.exp(v - vals[0]) for v in vals]
    tot = es[0] + es[1] + es[2] + es[3]
    idx_out = jnp.zeros((tm, LANES), jnp.int32)
    gate_out = jnp.zeros((tm, LANES), F32)
    for k in range(TOP_K):
        idx_out = jnp.where(lane == k, idxs[k], idx_out)
        gate_out = jnp.where(lane == k, es[k] / tot, gate_out)
    idx_ref[...] = idx_out
    gate_ref[...] = gate_out


def out_proj_router(x2d, attn, ssm, wo, fg, wr, br):
    t, d = x2d.shape
    tm = min(512, t)
    const = lambda i: (0, 0)
    once = pl.Buffered(1)
    return pl.pallas_call(
        _out_body,
        grid=(t // tm,),
        in_specs=[
            pl.BlockSpec((tm, d), lambda i: (i, 0)),
            pl.BlockSpec((tm, ATTN_WIDTH), lambda i: (i, 0)),
            pl.BlockSpec((tm, SSM_WIDTH), lambda i: (i, 0)),
            pl.BlockSpec((d, d), const, pipeline_mode=once),
            pl.BlockSpec((1, d), const, pipeline_mode=once),
            pl.BlockSpec((d, LANES), const, pipeline_mode=once),
            pl.BlockSpec((1, LANES), const, pipeline_mode=once),
        ],
        out_specs=[
            pl.BlockSpec((tm, d), lambda i: (i, 0)),
            pl.BlockSpec((tm, d // 2), lambda i: (i, 0)),
            pl.BlockSpec((tm, LANES), lambda i: (i, 0)),
            pl.BlockSpec((tm, LANES), lambda i: (i, 0)),
        ],
        out_shape=[
            jax.ShapeDtypeStruct((t, d), F32),
            jax.ShapeDtypeStruct((t, d // 2), jnp.uint32),
            jax.ShapeDtypeStruct((t, LANES), jnp.int32),
            jax.ShapeDtypeStruct((t, LANES), F32),
        ],
        compiler_params=_params("arbitrary"),
        name="out_proj_router",
    )(x2d, attn, ssm, wo, fg, wr, br)


MOE_ROWS = 512
FF_TILE = 1024
N_FF_TILES = D_FF // FF_TILE


def _expert_body(be_ref, nv_ref, tok_ref, tokn_ref, hf_ref, wg_ref, wu_ref, wd_ref, bg_ref, bu_ref, bd_ref, o_ref,
                 xrow_ref, x_ref, h_ref, gu_ref, dn_ref, row_sem, gu_sem, dn_sem):
    b = pl.program_id(0)
    nv = nv_ref[0]
    tm = h_ref.shape[0]
    half = D_MODEL // 2

    def row_copy(table_ref, r):
        return pltpu.make_async_copy(hf_ref.at[pl.ds(table_ref[0, 0, r], 1)], xrow_ref.at[pl.ds(r, 1)], row_sem)

    def rows_wait():
        pltpu.make_async_copy(hf_ref.at[pl.ds(0, tm)], xrow_ref, row_sem).wait()

    def unpack_rows(slot):
        w = xrow_ref[...]
        x_ref[slot, :, 0:half] = pltpu.bitcast(w << 16, F32).astype(BF16)
        x_ref[slot, :, half:] = pltpu.bitcast(w & jnp.uint32(0xFFFF0000), F32).astype(BF16)

    n_chunks = 2 * N_FF_TILES

    def chunk_copies(e, c):
        j = c % N_FF_TILES
        cols = pl.ds(j * FF_TILE, FF_TILE)
        if c < N_FF_TILES:
            return (pltpu.make_async_copy(wg_ref.at[e, :, cols], gu_ref.at[j, 0], gu_sem.at[j]),
                    pltpu.make_async_copy(wu_ref.at[e, :, cols], gu_ref.at[j, 1], gu_sem.at[j]))
        return (pltpu.make_async_copy(wd_ref.at[e, :, cols], dn_ref.at[j], dn_sem.at[j]),)

    def start_chunk(e, c):
        for cp in chunk_copies(e, c):
            cp.start()

    def next_chunk(e, c):
        for cp in chunk_copies(e, c):
            cp.wait()
        if c == 0:
            start_chunk(e, n_chunks - 1)
        else:
            @pl.when(b + 1 < nv)
            def _():
                start_chunk(be_ref[b + 1], c - 1)

    per_batch = tm // N_FF_TILES

    @pl.when(b < nv)
    def _():
        e = be_ref[b]
        slot = b % 2

        @pl.when(b == 0)
        def _():
            def issue(r, carry):
                row_copy(tok_ref, r).start()
                return carry
            lax.fori_loop(0, tm, issue, 0, unroll=8)
            for c in range(n_chunks - 1):
                start_chunk(e, c)
            rows_wait()
            unpack_rows(0)

        x = x_ref[slot]
        for j in range(N_FF_TILES):
            next_chunk(e, j)
            cols = slice(j * FF_TILE, (j + 1) * FF_TILE)
            g_acc = _dot(x, gu_ref[j, 0])
            u_acc = _dot(x, gu_ref[j, 1])
            for r in range(j * per_batch, (j + 1) * per_batch):
                row_copy(tokn_ref, r).start()
            gl = jnp.minimum(g_acc + bg_ref[0, :, cols], SWIGLU_LIMIT)
            li = jnp.clip(u_acc + bu_ref[0, :, cols], -SWIGLU_LIMIT, SWIGLU_LIMIT)
            h_ref[:, cols] = (gl * _sigmoid(SWIGLU_ALPHA * gl) * (li + 1.0)).astype(BF16)

        h = h_ref[...]
        for j in range(N_FF_TILES):
            next_chunk(e, N_FF_TILES + j)
            if j == N_FF_TILES - 1:
                rows_wait()
                unpack_rows(1 - slot)
            cols = slice(j * FF_TILE, (j + 1) * FF_TILE)
            o_ref[:, cols] = _dot(h, dn_ref[j]) + bd_ref[0, :, cols]


def experts(block_expert, n_valid, slot_tok, hf_packed, wg, bg, wu, bu, wd, bd):
    n_blocks = block_expert.shape[0]
    tm = MOE_ROWS
    d = D_MODEL

    def blk(b, nv):
        return jnp.minimum(b, nv[0] - 1)

    any_spec = pl.BlockSpec(memory_space=pl.ANY)
    grid_spec = pltpu.PrefetchScalarGridSpec(
        num_scalar_prefetch=2,
        grid=(n_blocks,),
        in_specs=[
            pl.BlockSpec((1, 1, tm), lambda b, be, nv: (blk(b, nv), 0, 0), memory_space=pltpu.SMEM),
            pl.BlockSpec((1, 1, tm), lambda b, be, nv: (blk(b + 1, nv), 0, 0), memory_space=pltpu.SMEM),
            any_spec, any_spec, any_spec, any_spec,
            pl.BlockSpec((1, 1, D_FF), lambda b, be, nv: (be[blk(b, nv)], 0, 0)),
            pl.BlockSpec((1, 1, D_FF), lambda b, be, nv: (be[blk(b, nv)], 0, 0)),
            pl.BlockSpec((1, 1, d), lambda b, be, nv: (be[blk(b, nv)], 0, 0)),
        ],
        out_specs=pl.BlockSpec((tm, d), lambda b, be, nv: (blk(b, nv), 0)),
        scratch_shapes=[
            pltpu.VMEM((tm, d // 2), jnp.uint32),
            pltpu.VMEM((2, tm, d), BF16),
            pltpu.VMEM((tm, D_FF), BF16),
            pltpu.VMEM((2, 2, d, FF_TILE), BF16),
            pltpu.VMEM((2, D_FF, FF_TILE), BF16),
            pltpu.SemaphoreType.DMA,
            pltpu.SemaphoreType.DMA((2,)),
            pltpu.SemaphoreType.DMA((2,)),
        ],
    )
    return pl.pallas_call(
        _expert_body,
        grid_spec=grid_spec,
        out_shape=jax.ShapeDtypeStruct((n_blocks * tm, d), F32),
        compiler_params=_params("arbitrary"),
        name="experts",
    )(block_expert, n_valid, slot_tok, slot_tok, hf_packed, wg, wu, wd, bg, bu, bd)


COMBINE_ROWS = 256


def _combine_body(pos_ref, posn_ref, h_ref, gate_ref, ys_ref, o_ref, buf_ref, sem):
    i = pl.program_id(0)
    tm = h_ref.shape[0]
    slot = i % 2

    def row_copy(table_ref, r, k, s):
        return pltpu.make_async_copy(ys_ref.at[pl.ds(table_ref[0, 0, r * TOP_K + k], 1)],
                                     buf_ref.at[s, pl.ds(k * tm + r, 1)], sem.at[s])

    def rows_wait(s):
        pltpu.make_async_copy(ys_ref.at[pl.ds(0, TOP_K * tm)], buf_ref.at[s], sem.at[s]).wait()

    @pl.when(i == 0)
    def _():
        def issue(r, carry):
            for k in range(TOP_K):
                row_copy(pos_ref, r, k, 0).start()
            return carry
        lax.fori_loop(0, tm, issue, 0, unroll=4)

    for r in range(tm):
        for k in range(TOP_K):
            row_copy(posn_ref, r, k, 1 - slot).start()

    rows_wait(slot)
    gates = gate_ref[...]
    acc = h_ref[...]
    for k in range(TOP_K):
        acc = acc + gates[:, k:k + 1] * buf_ref[slot, k * tm:(k + 1) * tm, :]
    o_ref[...] = acc

    @pl.when(i + 1 == pl.num_programs(0))
    def _():
        rows_wait(1 - slot)


def combine(pos, h, gates, ys):
    t, d = h.shape
    tm = min(COMBINE_ROWS, t)
    n = t // tm
    return pl.pallas_call(
        _combine_body,
        grid=(n,),
        in_specs=[
            pl.BlockSpec((1, 1, tm * TOP_K), lambda i: (i, 0, 0), memory_space=pltpu.SMEM),
            pl.BlockSpec((1, 1, tm * TOP_K), lambda i: (jnp.minimum(i + 1, n - 1), 0, 0), memory_space=pltpu.SMEM),
            pl.BlockSpec((tm, d), lambda i: (i, 0)),
            pl.BlockSpec((tm, LANES), lambda i: (i, 0)),
            pl.BlockSpec(memory_space=pl.ANY),
        ],
        out_specs=pl.BlockSpec((tm, d), lambda i: (i, 0)),
        out_shape=jax.ShapeDtypeStruct((t, d), F32),
        scratch_shapes=[
            pltpu.VMEM((2, TOP_K * tm, d), F32),
            pltpu.SemaphoreType.DMA((2,)),
        ],
        compiler_params=_params("arbitrary"),
        name="combine",
    )(pos, pos, h, gates, ys)


def routing_tables(top_idx, t):
    n_assign = t * TOP_K
    tm_c = min(COMBINE_ROWS, t)
    e_flat = top_idx.reshape(-1).astype(jnp.int32)
    position = jnp.arange(n_assign, dtype=jnp.int32)
    se, order = lax.sort((e_flat, position), num_keys=1, is_stable=True)
    experts_iota = jnp.arange(N_EXPERTS, dtype=jnp.int32)
    counts = jnp.sum((e_flat[None, :] == experts_iota[:, None]).astype(jnp.int32), axis=1)
    starts = jnp.cumsum(counts) - counts
    padded = (counts + MOE_ROWS - 1) // MOE_ROWS * MOE_ROWS
    pends = jnp.cumsum(padded)
    pstarts = pends - padded
    n_blocks = -(-n_assign // MOE_ROWS) + N_EXPERTS
    block_start = jnp.arange(n_blocks, dtype=jnp.int32) * MOE_ROWS
    block_expert = jnp.minimum(
        jnp.sum((pends[None, :] <= block_start[:, None]).astype(jnp.int32), axis=1), N_EXPERTS - 1)
    n_valid = (pends[-1:] // MOE_ROWS).astype(jnp.int32)

    row = jnp.arange(MOE_ROWS, dtype=jnp.int32)[None, :]
    e_b = block_expert[:, None]
    local = jnp.arange(n_blocks, dtype=jnp.int32)[:, None] * MOE_ROWS + row - pstarts[e_b]
    used = (local >= 0) & (local < counts[e_b])
    asg = order[jnp.clip(starts[e_b] + local, 0, n_assign - 1)]
    slot_tok = jnp.where(used, asg // TOP_K, 0).reshape(n_blocks, 1, MOE_ROWS)
    slot_sorted = pstarts[se] + position - starts[se]
    _, pos = lax.sort((order, slot_sorted), num_keys=1)
    return block_expert, n_valid, slot_tok, pos.reshape(t // tm_c, 1, tm_c * TOP_K)


def _pad_lanes(v, width, value=0.0):
    return jnp.pad(v, ((0, 0), (0, width - v.shape[1])), constant_values=value)


def hybrid_layer(x, attn_norm_w, w_in, q_norm_w, k_norm_w, attn_sinks, conv_w, conv_b, dt_bias, a_log, d_skip,
                 ssm_norm_w, attn_out_norm_w, w_out, ffn_norm_w, w_router, b_router, w_gate, b_gate, w_up, b_up,
                 w_down, b_down):
    batch, seq, d = x.shape
    t = batch * seq
    x2d = x.reshape(t, d)

    q_w, k_w, v_w, z_w, xbc_w, dt_w = jnp.split(
        w_in, [ATTN_WIDTH, ATTN_WIDTH + KV_WIDTH, ATTN_WIDTH + 2 * KV_WIDTH,
               ATTN_WIDTH + 2 * KV_WIDTH + SSM_WIDTH, ATTN_WIDTH + 2 * KV_WIDTH + SSM_WIDTH + CONV_WIDTH], axis=1)
    w_p = jnp.concatenate([q_w, z_w, xbc_w, k_w, v_w, dt_w], axis=1)
    w_p = _pad_lanes(w_p, PROJ_WIDTH).astype(BF16)
    proj = in_proj(x2d, attn_norm_w.reshape(1, d), w_p)

    qw2 = (jnp.tile(q_norm_w, 2) * (HEAD_DIM ** -0.5)).reshape(1, LANES)
    kw2 = jnp.tile(k_norm_w, 2).reshape(1, LANES)
    attn, (wg16, wu16) = attention(
        proj, attn_sinks, qw2, kw2, attn_out_norm_w.reshape(1, ATTN_WIDTH), batch, seq,
        side=(w_gate.reshape(N_EXPERTS * D_MODEL, D_FF), w_up.reshape(N_EXPERTS * D_MODEL, D_FF)))

    ssm, (wd16,) = ssd(
        proj,
        conv_w[:, :SSM_WIDTH], conv_w[:, SSM_WIDTH:],
        conv_b[:SSM_WIDTH].reshape(1, -1), conv_b[SSM_WIDTH:].reshape(1, -1),
        _pad_lanes(dt_bias.reshape(1, -1), LANES), _pad_lanes(a_log.reshape(1, -1), LANES),
        jnp.repeat(d_skip, HEAD_DIM).reshape(1, SSM_WIDTH), ssm_norm_w.reshape(1, SSM_WIDTH),
        batch, seq, side=(w_down.reshape(N_EXPERTS * D_FF, D_MODEL),))

    h, hf_packed, idx_pad, gate_pad = out_proj_router(
        x2d, attn, ssm, w_out.astype(BF16), ffn_norm_w.reshape(1, d),
        _pad_lanes(w_router, LANES), _pad_lanes(b_router.reshape(1, -1), LANES, NEG_BIG))

    block_expert, n_valid, slot_tok, pos = routing_tables(idx_pad[:, :TOP_K], t)
    ys = experts(
        block_expert, n_valid, slot_tok, hf_packed,
        wg16.reshape(N_EXPERTS, D_MODEL, D_FF), b_gate.reshape(N_EXPERTS, 1, D_FF),
        wu16.reshape(N_EXPERTS, D_MODEL, D_FF), b_up.reshape(N_EXPERTS, 1, D_FF),
        wd16.reshape(N_EXPERTS, D_FF, D_MODEL), b_down.reshape(N_EXPERTS, 1, D_MODEL))

    out = combine(pos, h, gate_pad, ys)
    return out.reshape(batch, seq, d)


def kernel(x, attn_norm_w, w_in, q_norm_w, k_norm_w, attn_sinks, conv_w, conv_b, dt_bias, a_log, d_skip, ssm_norm_w,
           attn_out_norm_w, w_out, ffn_norm_w, w_router, b_router, w_gate, b_gate, w_up, b_up, w_down, b_down):
    for i in range(attn_norm_w.shape[0]):
        x = hybrid_layer(x, attn_norm_w[i], w_in[i], q_norm_w[i], k_norm_w[i], attn_sinks[i], conv_w[i], conv_b[i],
                         dt_bias[i], a_log[i], d_skip[i], ssm_norm_w[i], attn_out_norm_w[i], w_out[i], ffn_norm_w[i],
                         w_router[i], b_router[i], w_gate[i], b_gate[i], w_up[i], b_up[i], w_down[i], b_down[i])
    return x
```

```python
import functools

import jax
import jax.numpy as jnp
from jax import lax
from jax.experimental import pallas as pl
from jax.experimental.pallas import tpu as pltpu

F32 = jnp.float32
BF16 = jnp.bfloat16

D_MODEL = 2048
HEAD_DIM = 64
ATTN_WIDTH = 1024
KV_WIDTH = 128
WINDOW = 128
SSM_WIDTH = 1024
SSM_HEADS = 16
SSM_GROUPS = 2
SSM_STATE = 128
SSM_CONV = 4
SSM_CHUNK = 128
BC_WIDTH = SSM_GROUPS * SSM_STATE
CONV_WIDTH = SSM_WIDTH + 2 * BC_WIDTH
N_EXPERTS = 32
TOP_K = 4
D_FF = 2048
SWIGLU_LIMIT = 7.0
SWIGLU_ALPHA = 1.702
EPS = 1e-6

LANES = 128
SUBLANES = 8
VMEM_LIMIT = 56 * 1024 * 1024

PROJ_WIDTH = 4096
COL_Q = 0
COL_Z = ATTN_WIDTH
COL_XBC = COL_Z + SSM_WIDTH
COL_K = COL_XBC + CONV_WIDTH
COL_V = COL_K + KV_WIDTH
COL_DT = COL_V + KV_WIDTH
NEG_BIG = -1e30


def _params(*sem):
    return pltpu.CompilerParams(dimension_semantics=sem, vmem_limit_bytes=VMEM_LIMIT)


def _split2(v):
    hi = v.astype(BF16)
    lo = (v - hi.astype(F32)).astype(BF16)
    return hi, lo


def _split3(v):
    hi = v.astype(BF16)
    r = v - hi.astype(F32)
    mid = r.astype(BF16)
    lo = (r - mid.astype(F32)).astype(BF16)
    return hi, mid, lo


def _dot(a, b):
    return jnp.dot(a, b, preferred_element_type=F32)


def _dot_exact_lhs(parts, m):
    acc = _dot(parts[0], m)
    for p in parts[1:]:
        acc = acc + _dot(p, m)
    return acc


def _sigmoid(v):
    return 1.0 / (1.0 + jnp.exp(-v))


def _iota(shape, dim):
    return lax.broadcasted_iota(jnp.int32, shape, dim)


def _in_proj_body(x0_ref, xn_ref, g_ref, w_ref, o_ref, hn_ref):
    i = pl.program_id(0)

    def norm(x):
        ms = jnp.mean(x * x, axis=-1, keepdims=True)
        return (x * lax.rsqrt(ms + EPS) * g_ref[...]).astype(BF16)

    @pl.when(i == 0)
    def _():
        hn_ref[0] = norm(x0_ref[...])

    slot = i % 2
    hn_ref[1 - slot] = norm(xn_ref[...])
    o_ref[...] = _dot(hn_ref[slot], w_ref[...])


def in_proj(x2d, g, w_p):
    t, d = x2d.shape
    n = w_p.shape[1]
    tm = min(512, t)
    ni = t // tm
    once = pl.Buffered(1)
    return pl.pallas_call(
        _in_proj_body,
        grid=(ni,),
        in_specs=[
            pl.BlockSpec((tm, d), lambda i: (0, 0), pipeline_mode=once),
            pl.BlockSpec((tm, d), lambda i: (jnp.minimum(i + 1, ni - 1), 0)),
            pl.BlockSpec((1, d), lambda i: (0, 0), pipeline_mode=once),
            pl.BlockSpec((d, n), lambda i: (0, 0), pipeline_mode=once),
        ],
        out_specs=pl.BlockSpec((tm, n), lambda i: (i, 0)),
        out_shape=jax.ShapeDtypeStruct((t, n), F32),
        scratch_shapes=[pltpu.VMEM((2, tm, d), BF16)],
        compiler_params=_params("arbitrary"),
        name="in_proj",
    )(x2d, x2d, g, w_p)


def _head_pair_norm(v, w, blockdiag):
    ss = _dot_exact_lhs(_split2(v * v), blockdiag)
    return v * lax.rsqrt(ss * (1.0 / HEAD_DIM) + EPS) * w


def _cast_side_streams(refs, n_side):
    for k in range(n_side):
        refs[n_side + 1 + k][...] = refs[k][...].astype(BF16)
    return refs[n_side]


def _side_specs(side, n_steps, step_index):
    specs, shapes = [], []
    for w in side:
        rows, cols = w.shape
        specs.append(pl.BlockSpec((rows // n_steps, cols), step_index))
        shapes.append(jax.ShapeDtypeStruct((rows, cols), BF16))
    return specs, shapes


def _attn_body(n_side, sinks_ref, q_ref, kvc_ref, kvp_ref, qw_ref, kw_ref, onw_ref, *rest):
    o_ref = _cast_side_streams(rest, n_side)
    acc_ref = rest[-1]
    i = pl.program_id(1)
    tq = q_ref.shape[0]
    nsub = tq // WINDOW
    nk = tq + WINDOW
    lo_lanes = _iota((1, LANES), 1) < HEAD_DIM
    blockdiag = jnp.where(
        _iota((LANES, LANES), 0) // HEAD_DIM == _iota((LANES, LANES), 1) // HEAD_DIM, 1.0, 0.0
    ).astype(BF16)

    kv_all = jnp.concatenate([kvp_ref[...], kvc_ref[...]], axis=0)
    k_all = _head_pair_norm(kv_all[:, :KV_WIDTH], kw_ref[...], blockdiag)
    v_all = kv_all[:, KV_WIDTH:]
    kt = k_all.T
    zero_half = jnp.zeros((HEAD_DIM, nk), F32)
    kt_var = [
        [jnp.concatenate([kt[j * HEAD_DIM:(j + 1) * HEAD_DIM], zero_half], axis=0).astype(BF16),
         jnp.concatenate([zero_half, kt[j * HEAD_DIM:(j + 1) * HEAD_DIM]], axis=0).astype(BF16)]
        for j in range(2)
    ]
    v_sw = pltpu.roll(v_all, HEAD_DIM, axis=1)
    v_var = [
        [jnp.where(lo_lanes, v_all, 0.0), jnp.where(lo_lanes, 0.0, v_sw)],
        [jnp.where(lo_lanes, v_sw, 0.0), jnp.where(lo_lanes, 0.0, v_all)],
    ]

    qn = [
        _head_pair_norm(q_ref[:, c * LANES:(c + 1) * LANES], qw_ref[...], blockdiag).astype(BF16)
        for c in range(ATTN_WIDTH // LANES)
    ]

    rows = 4 * WINDOW
    qi = _iota((rows, 2 * WINDOW), 0) % WINDOW
    kj = _iota((rows, 2 * WINDOW), 1)
    band = (kj > qi) & (kj <= qi + WINDOW)
    first_min = jnp.where(i == 0, WINDOW, 0)
    band_first = band & (kj >= first_min)
    sink_col = _iota((WINDOW, 2 * WINDOW), 1) == 0
    fill = [[jnp.concatenate([jnp.where(sink_col, sinks_ref[8 * j + 2 * c + par], NEG_BIG) for c in range(4)], axis=0)
             for par in range(2)] for j in range(2)]
    not_first_key = _iota((2 * WINDOW, LANES), 0) > 0
    lo_band = _iota((2 * WINDOW, LANES), 1) < HEAD_DIM
    ones_half = [jnp.where(lo_band, 1.0, 0.0).astype(BF16), jnp.where(lo_band, 0.0, 1.0).astype(BF16)]

    for s in range(nsub):
        mask = band_first if s == 0 else band
        keys = slice(s * WINDOW, s * WINDOW + 2 * WINDOW)
        for j in range(2):
            qs = jnp.concatenate([qn[4 * j + c][s * WINDOW:(s + 1) * WINDOW] for c in range(4)], axis=0)
            num = None
            den = None
            for par in range(2):
                sc = jnp.where(mask, _dot(qs, kt_var[j][par][:, keys]), fill[j][par])
                m = jnp.max(sc, axis=-1, keepdims=True)
                p = jnp.exp(sc - m).astype(BF16)
                v_band = jnp.where(not_first_key, v_var[j][par][keys], 0.0).astype(BF16)
                pv = _dot(p, v_band)
                ps = _dot(p, ones_half[par])
                num = pv if num is None else num + pv
                den = ps if den is None else den + ps
            out = num / den
            for c in range(4):
                acc_ref[s * WINDOW:(s + 1) * WINDOW, (4 * j + c) * LANES:(4 * j + c + 1) * LANES] = (
                    out[c * WINDOW:(c + 1) * WINDOW])

    a = acc_ref[...]
    ms = jnp.mean(a * a, axis=-1, keepdims=True)
    o_ref[...] = (a * lax.rsqrt(ms + EPS) * onw_ref[...]).astype(BF16)


def attention(proj, sinks, qw2, kw2, onw, batch, seq, side=()):
    t = proj.shape[0]
    tq = min(256, seq)
    nq = seq // tq
    sub = tq // WINDOW
    kv_col = COL_K // (2 * KV_WIDTH)
    side_specs, side_shapes = _side_specs(side, batch * nq, lambda b, i, s: (b * nq + i, 0))
    grid_spec = pltpu.PrefetchScalarGridSpec(
        num_scalar_prefetch=1,
        grid=(batch, nq),
        in_specs=[
            pl.BlockSpec((tq, ATTN_WIDTH), lambda b, i, s: (b * nq + i, COL_Q // ATTN_WIDTH)),
            pl.BlockSpec((tq, 2 * KV_WIDTH), lambda b, i, s: (b * nq + i, kv_col)),
            pl.BlockSpec((WINDOW, 2 * KV_WIDTH),
                         lambda b, i, s: (b * nq * sub + jnp.maximum(i * sub - 1, 0), kv_col)),
            pl.BlockSpec((1, LANES), lambda b, i, s: (0, 0)),
            pl.BlockSpec((1, LANES), lambda b, i, s: (0, 0)),
            pl.BlockSpec((1, ATTN_WIDTH), lambda b, i, s: (0, 0)),
        ] + side_specs,
        out_specs=[pl.BlockSpec((tq, ATTN_WIDTH), lambda b, i, s: (b * nq + i, 0))] + side_specs,
        scratch_shapes=[pltpu.VMEM((tq, ATTN_WIDTH), F32)],
    )
    outs = pl.pallas_call(
        functools.partial(_attn_body, len(side)),
        grid_spec=grid_spec,
        out_shape=[jax.ShapeDtypeStruct((t, ATTN_WIDTH), BF16)] + side_shapes,
        compiler_params=_params("arbitrary", "arbitrary"),
        name="attention",
    )(sinks, proj, proj, proj, qw2, kw2, onw, *side)
    return outs[0], outs[1:]


def _ssd_body(n_side, z_ref, xs_ref, bc_ref, dt_ref, cwx_ref, cwb_ref, cbx_ref, cbb_ref, dtb_ref, alog_ref,
              dsk_ref, nw_ref, *rest):
    o_ref = _cast_side_streams(rest, n_side)
    xpx_ref, xpb_ref, st_ref, y_ref = rest[-4:]
    i = pl.program_id(1)
    ts = xs_ref.shape[0]
    nch = ts // SSM_CHUNK
    gw = SSM_WIDTH // SSM_GROUPS
    pad = SUBLANES

    @pl.when(i == 0)
    def _():
        xpx_ref[0:pad, :] = jnp.zeros((pad, SSM_WIDTH), F32)
        xpb_ref[0:pad, :] = jnp.zeros((pad, 2 * BC_WIDTH), F32)
        st_ref[...] = jnp.zeros(st_ref.shape, F32)

    xpx_ref[pad:pad + ts, :] = xs_ref[...]
    xpb_ref[pad:pad + ts, :] = bc_ref[...]

    def conv_silu(xp_ref, w_ref, b_ref):
        acc = b_ref[...] + w_ref[SSM_CONV - 1:SSM_CONV, :] * xp_ref[pad:pad + ts, :]
        for k in range(SSM_CONV - 1):
            off = pad - (SSM_CONV - 1) + k
            acc = acc + w_ref[k:k + 1, :] * xp_ref[off:off + ts, :]
        return acc * _sigmoid(acc)

    xc = conv_silu(xpx_ref, cwx_ref, cbx_ref)
    bcc = conv_silu(xpb_ref, cwb_ref, cbb_ref)
    xpx_ref[0:pad, :] = xs_ref[ts - pad:ts, :]
    xpb_ref[0:pad, :] = bc_ref[ts - pad:ts, :]

    dtr = dt_ref[...] + dtb_ref[...]
    dt = jnp.maximum(dtr, 0.0) + jnp.log(1.0 + jnp.exp(-jnp.abs(dtr)))
    a = -jnp.exp(alog_ref[...])
    da = dt * a

    ri = _iota((ts, ts), 0)
    ci = _iota((ts, ts), 1)
    same = (ri // SSM_CHUNK) == (ci // SSM_CHUNK)
    tri_blk = jnp.where(same & (ci <= ri), 1.0, 0.0).astype(BF16)
    all_blk = jnp.where(same, 1.0, 0.0).astype(BF16)
    da3 = _split3(da)
    cs = _dot(tri_blk, da3[0]) + _dot(tri_blk, da3[1]) + _dot(tri_blk, da3[2])
    cl = _dot(all_blk, da3[0]) + _dot(all_blk, da3[1]) + _dot(all_blk, da3[2])
    ecs = jnp.exp(cs)
    dte = jnp.exp(cl - cs)

    expand = jnp.where(_iota((LANES, SSM_WIDTH), 1) // HEAD_DIM == _iota((LANES, SSM_WIDTH), 0),
                       1.0, 0.0).astype(BF16)
    dt_e = _dot_exact_lhs(_split2(dt), expand)
    dtd_e = _dot_exact_lhs(_split2(dt * dte), expand)
    ecs_e = _dot_exact_lhs(_split2(ecs), expand)
    xf = xc * dt_e
    xd = (xc * dtd_e).astype(BF16)

    lo_lanes = _iota((1, LANES), 1) < HEAD_DIM
    li = _iota((SSM_CHUNK, SSM_CHUNK), 0)
    lj = _iota((SSM_CHUNK, SSM_CHUNK), 1)
    tril = lj <= li
    upper = jnp.where(li <= lj, 1.0, 0.0).astype(BF16)

    for c in range(nch):
        r0 = c * SSM_CHUNK
        da_t = da[r0:r0 + SSM_CHUNK].T
        cs_t = _dot_exact_lhs(_split3(da_t), upper)
        cs_c = cs[r0:r0 + SSM_CHUNK]
        for g in range(SSM_GROUPS):
            b_c = bcc[r0:r0 + SSM_CHUNK, g * SSM_STATE:(g + 1) * SSM_STATE]
            c_c = bcc[r0:r0 + SSM_CHUNK, BC_WIDTH + g * SSM_STATE:BC_WIDTH + (g + 1) * SSM_STATE].astype(BF16)
            b_t = b_c.T.astype(BF16)
            cb = _dot(c_c, b_t)
            st = st_ref[g]
            y_off = _dot(c_c, st.astype(BF16)) * ecs_e[r0:r0 + SSM_CHUNK, g * gw:(g + 1) * gw]
            s_new = _dot(b_t, xd[r0:r0 + SSM_CHUNK, g * gw:(g + 1) * gw])
            parts = []
            for p in range(4):
                h0 = 8 * g + 2 * p
                ms = []
                for h in (h0, h0 + 1):
                    diff = cs_c[:, h:h + 1] - cs_t[h:h + 1, :]
                    ms.append((cb * jnp.exp(jnp.where(tril, diff, NEG_BIG))).astype(BF16))
                m2 = jnp.concatenate(ms, axis=1)
                x2 = xf[r0:r0 + SSM_CHUNK, h0 * HEAD_DIM:(h0 + 2) * HEAD_DIM]
                xbd = jnp.concatenate([jnp.where(lo_lanes, x2, 0.0), jnp.where(lo_lanes, 0.0, x2)],
                                      axis=0).astype(BF16)
                parts.append(_dot(m2, xbd))
            y_ref[r0:r0 + SSM_CHUNK, g * gw:(g + 1) * gw] = jnp.concatenate(parts, axis=1) + y_off
            dec = ecs_e[r0 + SSM_CHUNK - 1:r0 + SSM_CHUNK, g * gw:(g + 1) * gw]
            st_ref[g] = st * dec + s_new

    y = y_ref[...] + xc * dsk_ref[...]
    z = z_ref[...]
    gated = y * (z * _sigmoid(z))
    outs = []
    for g in range(SSM_GROUPS):
        gg = gated[:, g * gw:(g + 1) * gw]
        ms = jnp.mean(gg * gg, axis=-1, keepdims=True)
        outs.append(gg * lax.rsqrt(ms + EPS) * nw_ref[:, g * gw:(g + 1) * gw])
    o_ref[...] = jnp.concatenate(outs, axis=1).astype(BF16)


def ssd(proj, cwx, cwb, cbx, cbb, dtb, alog, dsk, nw, batch, seq, side=()):
    t = proj.shape[0]
    ts = min(256, seq)
    ns = seq // ts
    row = lambda b, i: b * ns + i
    const = lambda b, i: (0, 0)
    side_specs, side_shapes = _side_specs(side, batch * ns, lambda b, i: (row(b, i), 0))
    outs = pl.pallas_call(
        functools.partial(_ssd_body, len(side)),
        grid=(batch, ns),
        in_specs=[
            pl.BlockSpec((ts, SSM_WIDTH), lambda b, i: (row(b, i), COL_Z // SSM_WIDTH)),
            pl.BlockSpec((ts, SSM_WIDTH), lambda b, i: (row(b, i), COL_XBC // SSM_WIDTH)),
            pl.BlockSpec((ts, 2 * BC_WIDTH), lambda b, i: (row(b, i), (COL_XBC + SSM_WIDTH) // (2 * BC_WIDTH))),
            pl.BlockSpec((ts, LANES), lambda b, i: (row(b, i), COL_DT // LANES)),
            pl.BlockSpec((SSM_CONV, SSM_WIDTH), const),
            pl.BlockSpec((SSM_CONV, 2 * BC_WIDTH), const),
            pl.BlockSpec((1, SSM_WIDTH), const),
            pl.BlockSpec((1, 2 * BC_WIDTH), const),
            pl.BlockSpec((1, LANES), const),
            pl.BlockSpec((1, LANES), const),
            pl.BlockSpec((1, SSM_WIDTH), const),
            pl.BlockSpec((1, SSM_WIDTH), const),
        ] + side_specs,
        out_specs=[pl.BlockSpec((ts, SSM_WIDTH), lambda b, i: (row(b, i), 0))] + side_specs,
        out_shape=[jax.ShapeDtypeStruct((t, SSM_WIDTH), BF16)] + side_shapes,
        scratch_shapes=[
            pltpu.VMEM((ts + SUBLANES, SSM_WIDTH), F32),
            pltpu.VMEM((ts + SUBLANES, 2 * BC_WIDTH), F32),
            pltpu.VMEM((SSM_GROUPS, SSM_STATE, SSM_WIDTH // SSM_GROUPS), F32),
            pltpu.VMEM((ts, SSM_WIDTH), F32),
        ],
        compiler_params=_params("arbitrary", "arbitrary"),
        name="ssd",
    )(proj, proj, proj, proj, cwx, cwb, cbx, cbb, dtb, alog, dsk, nw, *side)
    return outs[0], outs[1:]


def _out_body(x_ref, a_ref, s_ref, wo_ref, fg_ref, wr_ref, br_ref, h_ref, hf_ref, idx_ref, gate_ref):
    tm = x_ref.shape[0]
    h = x_ref[...] + _dot(a_ref[...], wo_ref[0:ATTN_WIDTH, :]) + _dot(s_ref[...], wo_ref[ATTN_WIDTH:, :])
    h_ref[...] = h
    ms = jnp.mean(h * h, axis=-1, keepdims=True)
    hf = h * lax.rsqrt(ms + EPS) * fg_ref[...]
    hi, lo = _split2(hf)
    hi32 = hi.astype(F32)
    hf_ref[...] = (pltpu.bitcast(hi32[:, D_MODEL // 2:], jnp.uint32)
                   | (pltpu.bitcast(hi32[:, :D_MODEL // 2], jnp.uint32) >> 16))

    whi, wlo = _split2(wr_ref[...])
    hw = _dot(hi, jnp.concatenate([whi, wlo], axis=1))
    logits = hw[:, :LANES] + hw[:, LANES:] + _dot(lo, whi) + br_ref[...]
    lane = _iota((tm, LANES), 1)
    vals, idxs = [], []
    for _ in range(TOP_K):
        m = jnp.max(logits, axis=-1, keepdims=True)
        am = jnp.min(jnp.where(logits == m, lane, LANES), axis=-1, keepdims=True)
        vals.append(m)
        idxs.append(am)
        logits = jnp.where(lane == am, NEG_BIG * 2.0, logits)
    es = [jnp.exp(v - vals[0]) for v in vals]
    tot = es[0] + es[1] + es[2] + es[3]
    idx_out = jnp.zeros((tm, LANES), jnp.int32)
    gate_out = jnp.zeros((tm, LANES), F32)
    for k in range(TOP_K):
        idx_out = jnp.where(lane == k, idxs[k], idx_out)
        gate_out = jnp.where(lane == k, es[k] / tot, gate_out)
    idx_ref[...] = idx_out
    gate_ref[...] = gate_out


def out_proj_router(x2d, attn, ssm, wo, fg, wr, br):
    t, d = x2d.shape
    tm = min(512, t)
    const = lambda i: (0, 0)
    once = pl.Buffered(1)
    return pl.pallas_call(
        _out_body,
        grid=(t // tm,),
        in_specs=[
            pl.BlockSpec((tm, d), lambda i: (i, 0)),
            pl.BlockSpec((tm, ATTN_WIDTH), lambda i: (i, 0)),
            pl.BlockSpec((tm, SSM_WIDTH), lambda i: (i, 0)),
            pl.BlockSpec((d, d), const, pipeline_mode=once),
            pl.BlockSpec((1, d), const, pipeline_mode=once),
            pl.BlockSpec((d, LANES), const, pipeline_mode=once),
            pl.BlockSpec((1, LANES), const, pipeline_mode=once),
        ],
        out_specs=[
            pl.BlockSpec((tm, d), lambda i: (i, 0)),
            pl.BlockSpec((tm, d // 2), lambda i: (i, 0)),
            pl.BlockSpec((tm, LANES), lambda i: (i, 0)),
            pl.BlockSpec((tm, LANES), lambda i: (i, 0)),
        ],
        out_shape=[
            jax.ShapeDtypeStruct((t, d), F32),
            jax.ShapeDtypeStruct((t, d // 2), jnp.uint32),
            jax.ShapeDtypeStruct((t, LANES), jnp.int32),
            jax.ShapeDtypeStruct((t, LANES), F32),
        ],
        compiler_params=_params("arbitrary"),
        name="out_proj_router",
    )(x2d, attn, ssm, wo, fg, wr, br)


MOE_ROWS = 512
FF_TILE = 1024
N_FF_TILES = D_FF // FF_TILE


def _expert_body(be_ref, nv_ref, tok_ref, tokn_ref, hf_ref, wg_ref, wu_ref, wd_ref, bg_ref, bu_ref, bd_ref, o_ref,
                 xrow_ref, x_ref, h_ref, gu_ref, dn_ref, row_sem, gu_sem, dn_sem):
    b = pl.program_id(0)
    nv = nv_ref[0]
    tm = h_ref.shape[0]
    half = D_MODEL // 2

    def row_copy(table_ref, r):
        return pltpu.make_async_copy(hf_ref.at[pl.ds(table_ref[0, 0, r], 1)], xrow_ref.at[pl.ds(r, 1)], row_sem)

    def rows_wait():
        pltpu.make_async_copy(hf_ref.at[pl.ds(0, tm)], xrow_ref, row_sem).wait()

    def unpack_rows(slot):
        w = xrow_ref[...]
        x_ref[slot, :, 0:half] = pltpu.bitcast(w << 16, F32).astype(BF16)
        x_ref[slot, :, half:] = pltpu.bitcast(w & jnp.uint32(0xFFFF0000), F32).astype(BF16)

    n_chunks = 2 * N_FF_TILES

    def chunk_copies(e, c):
        j = c % N_FF_TILES
        cols = pl.ds(j * FF_TILE, FF_TILE)
        if c < N_FF_TILES:
            return (pltpu.make_async_copy(wg_ref.at[e, :, cols], gu_ref.at[j, 0], gu_sem.at[j]),
                    pltpu.make_async_copy(wu_ref.at[e, :, cols], gu_ref.at[j, 1], gu_sem.at[j]))
        return (pltpu.make_async_copy(wd_ref.at[e, :, cols], dn_ref.at[j], dn_sem.at[j]),)

    def start_chunk(e, c):
        for cp in chunk_copies(e, c):
            cp.start()

    def next_chunk(e, c):
        for cp in chunk_copies(e, c):
            cp.wait()
        if c == 0:
            start_chunk(e, n_chunks - 1)
        else:
            @pl.when(b + 1 < nv)
            def _():
                start_chunk(be_ref[b + 1], c - 1)

    per_batch = tm // N_FF_TILES

    @pl.when(b >= nv)
    def _():
        o_ref[...] = jnp.zeros(o_ref.shape, F32)

    @pl.when(b < nv)
    def _():
        e = be_ref[b]
        slot = b % 2

        @pl.when(b == 0)
        def _():
            def issue(r, carry):
                row_copy(tok_ref, r).start()
                return carry
            lax.fori_loop(0, tm, issue, 0, unroll=8)
            for c in range(n_chunks - 1):
                start_chunk(e, c)
            rows_wait()
            unpack_rows(0)

        x = x_ref[slot]
        for j in range(N_FF_TILES):
            next_chunk(e, j)
            cols = slice(j * FF_TILE, (j + 1) * FF_TILE)
            g_acc = _dot(x, gu_ref[j, 0])
            u_acc = _dot(x, gu_ref[j, 1])
            for r in range(j * per_batch, (j + 1) * per_batch):
                row_copy(tokn_ref, r).start()
            gl = jnp.minimum(g_acc + bg_ref[0, :, cols], SWIGLU_LIMIT)
            li = jnp.clip(u_acc + bu_ref[0, :, cols], -SWIGLU_LIMIT, SWIGLU_LIMIT)
            h_ref[:, cols] = (gl * _sigmoid(SWIGLU_ALPHA * gl) * (li + 1.0)).astype(BF16)

        h = h_ref[...]
        for j in range(N_FF_TILES):
            next_chunk(e, N_FF_TILES + j)
            if j == N_FF_TILES - 1:
                rows_wait()
                unpack_rows(1 - slot)
            cols = slice(j * FF_TILE, (j + 1) * FF_TILE)
            o_ref[:, cols] = _dot(h, dn_ref[j]) + bd_ref[0, :, cols]


def experts(block_expert, n_valid, slot_tok, hf_packed, wg, bg, wu, bu, wd, bd):
    n_blocks = block_expert.shape[0]
    tm = MOE_ROWS
    d = D_MODEL

    def blk(b, nv):
        return jnp.minimum(b, nv[0] - 1)

    any_spec = pl.BlockSpec(memory_space=pl.ANY)
    grid_spec = pltpu.PrefetchScalarGridSpec(
        num_scalar_prefetch=2,
        grid=(n_blocks,),
        in_specs=[
            pl.BlockSpec((1, 1, tm), lambda b, be, nv: (blk(b, nv), 0, 0), memory_space=pltpu.SMEM),
            pl.BlockSpec((1, 1, tm), lambda b, be, nv: (blk(b + 1, nv), 0, 0), memory_space=pltpu.SMEM),
            any_spec, any_spec, any_spec, any_spec,
            pl.BlockSpec((1, 1, D_FF), lambda b, be, nv: (be[blk(b, nv)], 0, 0)),
            pl.BlockSpec((1, 1, D_FF), lambda b, be, nv: (be[blk(b, nv)], 0, 0)),
            pl.BlockSpec((1, 1, d), lambda b, be, nv: (be[blk(b, nv)], 0, 0)),
        ],
        out_specs=pl.BlockSpec((tm, d), lambda b, be, nv: (b, 0)),
        scratch_shapes=[
            pltpu.VMEM((tm, d // 2), jnp.uint32),
            pltpu.VMEM((2, tm, d), BF16),
            pltpu.VMEM((tm, D_FF), BF16),
            pltpu.VMEM((2, 2, d, FF_TILE), BF16),
            pltpu.VMEM((2, D_FF, FF_TILE), BF16),
            pltpu.SemaphoreType.DMA,
            pltpu.SemaphoreType.DMA((2,)),
            pltpu.SemaphoreType.DMA((2,)),
        ],
    )
    return pl.pallas_call(
        _expert_body,
        grid_spec=grid_spec,
        out_shape=jax.ShapeDtypeStruct((n_blocks * tm, d), F32),
        compiler_params=_params("arbitrary"),
        name="experts",
    )(block_expert, n_valid, slot_tok, slot_tok, hf_packed, wg, wu, wd, bg, bu, bd)


COMBINE_ROWS = 256


def _combine_body(pos_ref, posn_ref, h_ref, gate_ref, ys_ref, o_ref, buf_ref, sem):
    i = pl.program_id(0)
    tm = h_ref.shape[0]
    slot = i % 2

    def row_copy(table_ref, r, k, s):
        return pltpu.make_async_copy(ys_ref.at[pl.ds(table_ref[0, 0, r * TOP_K + k], 1)],
                                     buf_ref.at[s, pl.ds(k * tm + r, 1)], sem.at[s])

    def rows_wait(s):
        pltpu.make_async_copy(ys_ref.at[pl.ds(0, TOP_K * tm)], buf_ref.at[s], sem.at[s]).wait()

    @pl.when(i == 0)
    def _():
        def issue(r, carry):
            for k in range(TOP_K):
                row_copy(pos_ref, r, k, 0).start()
            return carry
        lax.fori_loop(0, tm, issue, 0, unroll=4)

    for r in range(tm):
        for k in range(TOP_K):
            row_copy(posn_ref, r, k, 1 - slot).start()

    rows_wait(slot)
    gates = gate_ref[...]
    acc = h_ref[...]
    for k in range(TOP_K):
        acc = acc + gates[:, k:k + 1] * buf_ref[slot, k * tm:(k + 1) * tm, :]
    o_ref[...] = acc

    @pl.when(i + 1 == pl.num_programs(0))
    def _():
        rows_wait(1 - slot)


def combine(pos, h, gates, ys):
    t, d = h.shape
    tm = min(COMBINE_ROWS, t)
    n = t // tm
    return pl.pallas_call(
        _combine_body,
        grid=(n,),
        in_specs=[
            pl.BlockSpec((1, 1, tm * TOP_K), lambda i: (i, 0, 0), memory_space=pltpu.SMEM),
            pl.BlockSpec((1, 1, tm * TOP_K), lambda i: (jnp.minimum(i + 1, n - 1), 0, 0), memory_space=pltpu.SMEM),
            pl.BlockSpec((tm, d), lambda i: (i, 0)),
            pl.BlockSpec((tm, LANES), lambda i: (i, 0)),
            pl.BlockSpec(memory_space=pl.ANY),
        ],
        out_specs=pl.BlockSpec((tm, d), lambda i: (i, 0)),
        out_shape=jax.ShapeDtypeStruct((t, d), F32),
        scratch_shapes=[
            pltpu.VMEM((2, TOP_K * tm, d), F32),
            pltpu.SemaphoreType.DMA((2,)),
        ],
        compiler_params=_params("arbitrary"),
        name="combine",
    )(pos, pos, h, gates, ys)


def routing_tables(top_idx, t):
    n_assign = t * TOP_K
    tm_c = min(COMBINE_ROWS, t)
    e_flat = top_idx.reshape(-1).astype(jnp.int32)
    position = jnp.arange(n_assign, dtype=jnp.int32)
    se, order = lax.sort((e_flat, position), num_keys=1, is_stable=True)
    experts_iota = jnp.arange(N_EXPERTS, dtype=jnp.int32)
    counts = jnp.sum((e_flat[None, :] == experts_iota[:, None]).astype(jnp.int32), axis=1)
    starts = jnp.cumsum(counts) - counts
    padded = (counts + MOE_ROWS - 1) // MOE_ROWS * MOE_ROWS
    pends = jnp.cumsum(padded)
    pstarts = pends - padded
    n_blocks = -(-n_assign // MOE_ROWS) + N_EXPERTS
    block_start = jnp.arange(n_blocks, dtype=jnp.int32) * MOE_ROWS
    block_expert = jnp.minimum(
        jnp.sum((pends[None, :] <= block_start[:, None]).astype(jnp.int32), axis=1), N_EXPERTS - 1)
    n_valid = (pends[-1:] // MOE_ROWS).astype(jnp.int32)

    row = jnp.arange(MOE_ROWS, dtype=jnp.int32)[None, :]
    e_b = block_expert[:, None]
    local = jnp.arange(n_blocks, dtype=jnp.int32)[:, None] * MOE_ROWS + row - pstarts[e_b]
    used = (local >= 0) & (local < counts[e_b])
    asg = order[jnp.clip(starts[e_b] + local, 0, n_assign - 1)]
    slot_tok = jnp.where(used, asg // TOP_K, 0).reshape(n_blocks, 1, MOE_ROWS)
    slot_sorted = pstarts[se] + position - starts[se]
    _, pos = lax.sort((order, slot_sorted), num_keys=1)
    return block_expert, n_valid, slot_tok, pos.reshape(t // tm_c, 1, tm_c * TOP_K)


def _pad_lanes(v, width, value=0.0):
    return jnp.pad(v, ((0, 0), (0, width - v.shape[1])), constant_values=value)


def hybrid_layer(x, attn_norm_w, w_in, q_norm_w, k_norm_w, attn_sinks, conv_w, conv_b, dt_bias, a_log, d_skip,
                 ssm_norm_w, attn_out_norm_w, w_out, ffn_norm_w, w_router, b_router, w_gate, b_gate, w_up, b_up,
                 w_down, b_down):
    batch, seq, d = x.shape
    t = batch * seq
    x2d = x.reshape(t, d)

    q_w, k_w, v_w, z_w, xbc_w, dt_w = jnp.split(
        w_in, [ATTN_WIDTH, ATTN_WIDTH + KV_WIDTH, ATTN_WIDTH + 2 * KV_WIDTH,
               ATTN_WIDTH + 2 * KV_WIDTH + SSM_WIDTH, ATTN_WIDTH + 2 * KV_WIDTH + SSM_WIDTH + CONV_WIDTH], axis=1)
    w_p = jnp.concatenate([q_w, z_w, xbc_w, k_w, v_w, dt_w], axis=1)
    w_p = _pad_lanes(w_p, PROJ_WIDTH).astype(BF16)
    proj = in_proj(x2d, attn_norm_w.reshape(1, d), w_p)

    qw2 = (jnp.tile(q_norm_w, 2) * (HEAD_DIM ** -0.5)).reshape(1, LANES)
    kw2 = jnp.tile(k_norm_w, 2).reshape(1, LANES)
    attn, (wg16, wu16) = attention(
        proj, attn_sinks, qw2, kw2, attn_out_norm_w.reshape(1, ATTN_WIDTH), batch, seq,
        side=(w_gate.reshape(N_EXPERTS * D_MODEL, D_FF), w_up.reshape(N_EXPERTS * D_MODEL, D_FF)))

    ssm, (wd16,) = ssd(
        proj,
        conv_w[:, :SSM_WIDTH], conv_w[:, SSM_WIDTH:],
        conv_b[:SSM_WIDTH].reshape(1, -1), conv_b[SSM_WIDTH:].reshape(1, -1),
        _pad_lanes(dt_bias.reshape(1, -1), LANES), _pad_lanes(a_log.reshape(1, -1), LANES),
        jnp.repeat(d_skip, HEAD_DIM).reshape(1, SSM_WIDTH), ssm_norm_w.reshape(1, SSM_WIDTH),
        batch, seq, side=(w_down.reshape(N_EXPERTS * D_FF, D_MODEL),))

    h, hf_packed, idx_pad, gate_pad = out_proj_router(
        x2d, attn, ssm, w_out.astype(BF16), ffn_norm_w.reshape(1, d),
        _pad_lanes(w_router, LANES), _pad_lanes(b_router.reshape(1, -1), LANES, NEG_BIG))

    block_expert, n_valid, slot_tok, pos = routing_tables(idx_pad[:, :TOP_K], t)
    ys = experts(
        block_expert, n_valid, slot_tok, hf_packed,
        wg16.reshape(N_EXPERTS, D_MODEL, D_FF), b_gate.reshape(N_EXPERTS, 1, D_FF),
        wu16.reshape(N_EXPERTS, D_MODEL, D_FF), b_up.reshape(N_EXPERTS, 1, D_FF),
        wd16.reshape(N_EXPERTS, D_FF, D_MODEL), b_down.reshape(N_EXPERTS, 1, D_MODEL))

    out = combine(pos, h, gate_pad, ys)
    return out.reshape(batch, seq, d)


def kernel(x, attn_norm_w, w_in, q_norm_w, k_norm_w, attn_sinks, conv_w, conv_b, dt_bias, a_log, d_skip, ssm_norm_w,
           attn_out_norm_w, w_out, ffn_norm_w, w_router, b_router, w_gate, b_gate, w_up, b_up, w_down, b_down):
    for i in range(attn_norm_w.shape[0]):
        x = hybrid_layer(x, attn_norm_w[i], w_in[i], q_norm_w[i], k_norm_w[i], attn_sinks[i], conv_w[i], conv_b[i],
                         dt_bias[i], a_log[i], d_skip[i], ssm_norm_w[i], attn_out_norm_w[i], w_out[i], ffn_norm_w[i],
                         w_router[i], b_router[i], w_gate[i], b_gate[i], w_up[i], b_up[i], w_down[i], b_down[i])
    return x
```

```python
import functools

import jax
import jax.numpy as jnp
from jax import lax
from jax.experimental import pallas as pl
from jax.experimental.pallas import tpu as pltpu

F32 = jnp.float32
BF16 = jnp.bfloat16

D_MODEL = 2048
HEAD_DIM = 64
ATTN_WIDTH = 1024
KV_WIDTH = 128
WINDOW = 128
SSM_WIDTH = 1024
SSM_HEADS = 16
SSM_GROUPS = 2
SSM_STATE = 128
SSM_CONV = 4
SSM_CHUNK = 128
BC_WIDTH = SSM_GROUPS * SSM_STATE
CONV_WIDTH = SSM_WIDTH + 2 * BC_WIDTH
N_EXPERTS = 32
TOP_K = 4
D_FF = 2048
SWIGLU_LIMIT = 7.0
SWIGLU_ALPHA = 1.702
EPS = 1e-6

LANES = 128
SUBLANES = 8
VMEM_LIMIT = 56 * 1024 * 1024

PROJ_WIDTH = 4096
COL_Q = 0
COL_Z = ATTN_WIDTH
COL_XBC = COL_Z + SSM_WIDTH
COL_K = COL_XBC + CONV_WIDTH
COL_V = COL_K + KV_WIDTH
COL_DT = COL_V + KV_WIDTH
NEG_BIG = -1e30


def _params(*sem):
    return pltpu.CompilerParams(dimension_semantics=sem, vmem_limit_bytes=VMEM_LIMIT)


def _split2(v):
    hi = v.astype(BF16)
    lo = (v - hi.astype(F32)).astype(BF16)
    return hi, lo


def _split3(v):
    hi = v.astype(BF16)
    r = v - hi.astype(F32)
    mid = r.astype(BF16)
    lo = (r - mid.astype(F32)).astype(BF16)
    return hi, mid, lo


def _dot(a, b):
    return jnp.dot(a, b, preferred_element_type=F32)


def _dot_exact_lhs(parts, m):
    acc = _dot(parts[0], m)
    for p in parts[1:]:
        acc = acc + _dot(p, m)
    return acc


def _sigmoid(v):
    return 1.0 / (1.0 + jnp.exp(-v))


def _iota(shape, dim):
    return lax.broadcasted_iota(jnp.int32, shape, dim)


def _in_proj_body(x0_ref, xn_ref, g_ref, w_ref, o_ref, hn_ref):
    i = pl.program_id(0)

    def norm(x):
        ms = jnp.mean(x * x, axis=-1, keepdims=True)
        return (x * lax.rsqrt(ms + EPS) * g_ref[...]).astype(BF16)

    @pl.when(i == 0)
    def _():
        hn_ref[0] = norm(x0_ref[...])

    slot = i % 2
    hn_ref[1 - slot] = norm(xn_ref[...])
    o_ref[...] = _dot(hn_ref[slot], w_ref[...])


def in_proj(x2d, g, w_p):
    t, d = x2d.shape
    n = w_p.shape[1]
    tm = min(512, t)
    ni = t // tm
    once = pl.Buffered(1)
    return pl.pallas_call(
        _in_proj_body,
        grid=(ni,),
        in_specs=[
            pl.BlockSpec((tm, d), lambda i: (0, 0), pipeline_mode=once),
            pl.BlockSpec((tm, d), lambda i: (jnp.minimum(i + 1, ni - 1), 0)),
            pl.BlockSpec((1, d), lambda i: (0, 0), pipeline_mode=once),
            pl.BlockSpec((d, n), lambda i: (0, 0), pipeline_mode=once),
        ],
        out_specs=pl.BlockSpec((tm, n), lambda i: (i, 0)),
        out_shape=jax.ShapeDtypeStruct((t, n), F32),
        scratch_shapes=[pltpu.VMEM((2, tm, d), BF16)],
        compiler_params=_params("arbitrary"),
        name="in_proj",
    )(x2d, x2d, g, w_p)


def _head_pair_norm(v, w, blockdiag):
    ss = _dot_exact_lhs(_split2(v * v), blockdiag)
    return v * lax.rsqrt(ss * (1.0 / HEAD_DIM) + EPS) * w


def _cast_side_streams(refs, n_side):
    for k in range(n_side):
        refs[n_side + 1 + k][...] = refs[k][...].astype(BF16)
    return refs[n_side]


def _side_specs(side, n_steps, step_index):
    specs, shapes = [], []
    for w in side:
        rows, cols = w.shape
        specs.append(pl.BlockSpec((rows // n_steps, cols), step_index))
        shapes.append(jax.ShapeDtypeStruct((rows, cols), BF16))
    return specs, shapes


def _attn_body(n_side, sinks_ref, q_ref, kvc_ref, kvp_ref, qw_ref, kw_ref, onw_ref, *rest):
    o_ref = _cast_side_streams(rest, n_side)
    acc_ref = rest[-1]
    i = pl.program_id(1)
    tq = q_ref.shape[0]
    nsub = tq // WINDOW
    nk = tq + WINDOW
    lo_lanes = _iota((1, LANES), 1) < HEAD_DIM
    blockdiag = jnp.where(
        _iota((LANES, LANES), 0) // HEAD_DIM == _iota((LANES, LANES), 1) // HEAD_DIM, 1.0, 0.0
    ).astype(BF16)

    kv_all = jnp.concatenate([kvp_ref[...], kvc_ref[...]], axis=0)
    k_all = _head_pair_norm(kv_all[:, :KV_WIDTH], kw_ref[...], blockdiag)
    v_all = kv_all[:, KV_WIDTH:]
    kt = k_all.T
    zero_half = jnp.zeros((HEAD_DIM, nk), F32)
    kt_var = [
        [jnp.concatenate([kt[j * HEAD_DIM:(j + 1) * HEAD_DIM], zero_half], axis=0).astype(BF16),
         jnp.concatenate([zero_half, kt[j * HEAD_DIM:(j + 1) * HEAD_DIM]], axis=0).astype(BF16)]
        for j in range(2)
    ]
    v_sw = pltpu.roll(v_all, HEAD_DIM, axis=1)
    v_var = [
        [jnp.where(lo_lanes, v_all, 0.0), jnp.where(lo_lanes, 0.0, v_sw)],
        [jnp.where(lo_lanes, v_sw, 0.0), jnp.where(lo_lanes, 0.0, v_all)],
    ]

    qn = [
        _head_pair_norm(q_ref[:, c * LANES:(c + 1) * LANES], qw_ref[...], blockdiag).astype(BF16)
        for c in range(ATTN_WIDTH // LANES)
    ]

    rows = 4 * WINDOW
    qi = _iota((rows, 2 * WINDOW), 0) % WINDOW
    kj = _iota((rows, 2 * WINDOW), 1)
    band = (kj > qi) & (kj <= qi + WINDOW)
    first_min = jnp.where(i == 0, WINDOW, 0)
    band_first = band & (kj >= first_min)
    sink_col = _iota((WINDOW, 2 * WINDOW), 1) == 0
    fill = [[jnp.concatenate([jnp.where(sink_col, sinks_ref[8 * j + 2 * c + par], NEG_BIG) for c in range(4)], axis=0)
             for par in range(2)] for j in range(2)]
    not_first_key = _iota((2 * WINDOW, LANES), 0) > 0
    lo_band = _iota((2 * WINDOW, LANES), 1) < HEAD_DIM
    ones_half = [jnp.where(lo_band, 1.0, 0.0).astype(BF16), jnp.where(lo_band, 0.0, 1.0).astype(BF16)]

    for s in range(nsub):
        mask = band_first if s == 0 else band
        keys = slice(s * WINDOW, s * WINDOW + 2 * WINDOW)
        for j in range(2):
            qs = jnp.concatenate([qn[4 * j + c][s * WINDOW:(s + 1) * WINDOW] for c in range(4)], axis=0)
            num = None
            den = None
            for par in range(2):
                sc = jnp.where(mask, _dot(qs, kt_var[j][par][:, keys]), fill[j][par])
                m = jnp.max(sc, axis=-1, keepdims=True)
                p = jnp.exp(sc - m).astype(BF16)
                v_band = jnp.where(not_first_key, v_var[j][par][keys], 0.0).astype(BF16)
                pv = _dot(p, v_band)
                ps = _dot(p, ones_half[par])
                num = pv if num is None else num + pv
                den = ps if den is None else den + ps
            out = num / den
            for c in range(4):
                acc_ref[s * WINDOW:(s + 1) * WINDOW, (4 * j + c) * LANES:(4 * j + c + 1) * LANES] = (
                    out[c * WINDOW:(c + 1) * WINDOW])

    a = acc_ref[...]
    ms = jnp.mean(a * a, axis=-1, keepdims=True)
    o_ref[...] = (a * lax.rsqrt(ms + EPS) * onw_ref[...]).astype(BF16)


def attention(proj, sinks, qw2, kw2, onw, batch, seq, side=()):
    t = proj.shape[0]
    tq = min(256, seq)
    nq = seq // tq
    sub = tq // WINDOW
    kv_col = COL_K // (2 * KV_WIDTH)
    side_specs, side_shapes = _side_specs(side, batch * nq, lambda b, i, s: (b * nq + i, 0))
    grid_spec = pltpu.PrefetchScalarGridSpec(
        num_scalar_prefetch=1,
        grid=(batch, nq),
        in_specs=[
            pl.BlockSpec((tq, ATTN_WIDTH), lambda b, i, s: (b * nq + i, COL_Q // ATTN_WIDTH)),
            pl.BlockSpec((tq, 2 * KV_WIDTH), lambda b, i, s: (b * nq + i, kv_col)),
            pl.BlockSpec((WINDOW, 2 * KV_WIDTH),
                         lambda b, i, s: (b * nq * sub + jnp.maximum(i * sub - 1, 0), kv_col)),
            pl.BlockSpec((1, LANES), lambda b, i, s: (0, 0)),
            pl.BlockSpec((1, LANES), lambda b, i, s: (0, 0)),
            pl.BlockSpec((1, ATTN_WIDTH), lambda b, i, s: (0, 0)),
        ] + side_specs,
        out_specs=[pl.BlockSpec((tq, ATTN_WIDTH), lambda b, i, s: (b * nq + i, 0))] + side_specs,
        scratch_shapes=[pltpu.VMEM((tq, ATTN_WIDTH), F32)],
    )
    outs = pl.pallas_call(
        functools.partial(_attn_body, len(side)),
        grid_spec=grid_spec,
        out_shape=[jax.ShapeDtypeStruct((t, ATTN_WIDTH), BF16)] + side_shapes,
        compiler_params=_params("arbitrary", "arbitrary"),
        name="attention",
    )(sinks, proj, proj, proj, qw2, kw2, onw, *side)
    return outs[0], outs[1:]


def _ssd_body(n_side, z_ref, xs_ref, bc_ref, dt_ref, cwx_ref, cwb_ref, cbx_ref, cbb_ref, dtb_ref, alog_ref,
              dsk_ref, nw_ref, *rest):
    o_ref = _cast_side_streams(rest, n_side)
    xpx_ref, xpb_ref, st_ref, y_ref = rest[-4:]
    i = pl.program_id(1)
    ts = xs_ref.shape[0]
    nch = ts // SSM_CHUNK
    gw = SSM_WIDTH // SSM_GROUPS
    pad = SUBLANES

    @pl.when(i == 0)
    def _():
        xpx_ref[0:pad, :] = jnp.zeros((pad, SSM_WIDTH), F32)
        xpb_ref[0:pad, :] = jnp.zeros((pad, 2 * BC_WIDTH), F32)
        st_ref[...] = jnp.zeros(st_ref.shape, F32)

    xpx_ref[pad:pad + ts, :] = xs_ref[...]
    xpb_ref[pad:pad + ts, :] = bc_ref[...]

    def conv_silu(xp_ref, w_ref, b_ref):
        acc = b_ref[...] + w_ref[SSM_CONV - 1:SSM_CONV, :] * xp_ref[pad:pad + ts, :]
        for k in range(SSM_CONV - 1):
            off = pad - (SSM_CONV - 1) + k
            acc = acc + w_ref[k:k + 1, :] * xp_ref[off:off + ts, :]
        return acc * _sigmoid(acc)

    xc = conv_silu(xpx_ref, cwx_ref, cbx_ref)
    bcc = conv_silu(xpb_ref, cwb_ref, cbb_ref)
    xpx_ref[0:pad, :] = xs_ref[ts - pad:ts, :]
    xpb_ref[0:pad, :] = bc_ref[ts - pad:ts, :]

    dtr = dt_ref[...] + dtb_ref[...]
    dt = jnp.maximum(dtr, 0.0) + jnp.log(1.0 + jnp.exp(-jnp.abs(dtr)))
    a = -jnp.exp(alog_ref[...])
    da = dt * a

    ri = _iota((ts, ts), 0)
    ci = _iota((ts, ts), 1)
    same = (ri // SSM_CHUNK) == (ci // SSM_CHUNK)
    tri_blk = jnp.where(same & (ci <= ri), 1.0, 0.0).astype(BF16)
    all_blk = jnp.where(same, 1.0, 0.0).astype(BF16)
    da3 = _split3(da)
    cs = _dot(tri_blk, da3[0]) + _dot(tri_blk, da3[1]) + _dot(tri_blk, da3[2])
    cl = _dot(all_blk, da3[0]) + _dot(all_blk, da3[1]) + _dot(all_blk, da3[2])
    ecs = jnp.exp(cs)
    dte = jnp.exp(cl - cs)

    expand = jnp.where(_iota((LANES, SSM_WIDTH), 1) // HEAD_DIM == _iota((LANES, SSM_WIDTH), 0),
                       1.0, 0.0).astype(BF16)
    dt_e = _dot_exact_lhs(_split2(dt), expand)
    dtd_e = _dot_exact_lhs(_split2(dt * dte), expand)
    ecs_e = _dot_exact_lhs(_split2(ecs), expand)
    xf = xc * dt_e
    xd = (xc * dtd_e).astype(BF16)

    lo_lanes = _iota((1, LANES), 1) < HEAD_DIM
    li = _iota((SSM_CHUNK, SSM_CHUNK), 0)
    lj = _iota((SSM_CHUNK, SSM_CHUNK), 1)
    tril = lj <= li
    upper = jnp.where(li <= lj, 1.0, 0.0).astype(BF16)

    for c in range(nch):
        r0 = c * SSM_CHUNK
        da_t = da[r0:r0 + SSM_CHUNK].T
        cs_t = _dot_exact_lhs(_split3(da_t), upper)
        cs_c = cs[r0:r0 + SSM_CHUNK]
        for g in range(SSM_GROUPS):
            b_c = bcc[r0:r0 + SSM_CHUNK, g * SSM_STATE:(g + 1) * SSM_STATE]
            c_c = bcc[r0:r0 + SSM_CHUNK, BC_WIDTH + g * SSM_STATE:BC_WIDTH + (g + 1) * SSM_STATE].astype(BF16)
            b_t = b_c.T.astype(BF16)
            cb = _dot(c_c, b_t)
            st = st_ref[g]
            y_off = _dot(c_c, st.astype(BF16)) * ecs_e[r0:r0 + SSM_CHUNK, g * gw:(g + 1) * gw]
            s_new = _dot(b_t, xd[r0:r0 + SSM_CHUNK, g * gw:(g + 1) * gw])
            parts = []
            for p in range(4):
                h0 = 8 * g + 2 * p
                ms = []
                for h in (h0, h0 + 1):
                    diff = cs_c[:, h:h + 1] - cs_t[h:h + 1, :]
                    ms.append((cb * jnp.exp(jnp.where(tril, diff, NEG_BIG))).astype(BF16))
                m2 = jnp.concatenate(ms, axis=1)
                x2 = xf[r0:r0 + SSM_CHUNK, h0 * HEAD_DIM:(h0 + 2) * HEAD_DIM]
                xbd = jnp.concatenate([jnp.where(lo_lanes, x2, 0.0), jnp.where(lo_lanes, 0.0, x2)],
                                      axis=0).astype(BF16)
                parts.append(_dot(m2, xbd))
            y_ref[r0:r0 + SSM_CHUNK, g * gw:(g + 1) * gw] = jnp.concatenate(parts, axis=1) + y_off
            dec = ecs_e[r0 + SSM_CHUNK - 1:r0 + SSM_CHUNK, g * gw:(g + 1) * gw]
            st_ref[g] = st * dec + s_new

    y = y_ref[...] + xc * dsk_ref[...]
    z = z_ref[...]
    gated = y * (z * _sigmoid(z))
    outs = []
    for g in range(SSM_GROUPS):
        gg = gated[:, g * gw:(g + 1) * gw]
        ms = jnp.mean(gg * gg, axis=-1, keepdims=True)
        outs.append(gg * lax.rsqrt(ms + EPS) * nw_ref[:, g * gw:(g + 1) * gw])
    o_ref[...] = jnp.concatenate(outs, axis=1).astype(BF16)


def ssd(proj, cwx, cwb, cbx, cbb, dtb, alog, dsk, nw, batch, seq, side=()):
    t = proj.shape[0]
    ts = min(256, seq)
    ns = seq // ts
    row = lambda b, i: b * ns + i
    const = lambda b, i: (0, 0)
    side_specs, side_shapes = _side_specs(side, batch * ns, lambda b, i: (row(b, i), 0))
    outs = pl.pallas_call(
        functools.partial(_ssd_body, len(side)),
        grid=(batch, ns),
        in_specs=[
            pl.BlockSpec((ts, SSM_WIDTH), lambda b, i: (row(b, i), COL_Z // SSM_WIDTH)),
            pl.BlockSpec((ts, SSM_WIDTH), lambda b, i: (row(b, i), COL_XBC // SSM_WIDTH)),
            pl.BlockSpec((ts, 2 * BC_WIDTH), lambda b, i: (row(b, i), (COL_XBC + SSM_WIDTH) // (2 * BC_WIDTH))),
            pl.BlockSpec((ts, LANES), lambda b, i: (row(b, i), COL_DT // LANES)),
            pl.BlockSpec((SSM_CONV, SSM_WIDTH), const),
            pl.BlockSpec((SSM_CONV, 2 * BC_WIDTH), const),
            pl.BlockSpec((1, SSM_WIDTH), const),
            pl.BlockSpec((1, 2 * BC_WIDTH), const),
            pl.BlockSpec((1, LANES), const),
            pl.BlockSpec((1, LANES), const),
            pl.BlockSpec((1, SSM_WIDTH), const),
            pl.BlockSpec((1, SSM_WIDTH), const),
        ] + side_specs,
        out_specs=[pl.BlockSpec((ts, SSM_WIDTH), lambda b, i: (row(b, i), 0))] + side_specs,
        out_shape=[jax.ShapeDtypeStruct((t, SSM_WIDTH), BF16)] + side_shapes,
        scratch_shapes=[
            pltpu.VMEM((ts + SUBLANES, SSM_WIDTH), F32),
            pltpu.VMEM((ts + SUBLANES, 2 * BC_WIDTH), F32),
            pltpu.VMEM((SSM_GROUPS, SSM_STATE, SSM_WIDTH // SSM_GROUPS), F32),
            pltpu.VMEM((ts, SSM_WIDTH), F32),
        ],
        compiler_params=_params("arbitrary", "arbitrary"),
        name="ssd",
    )(proj, proj, proj, proj, cwx, cwb, cbx, cbb, dtb, alog, dsk, nw, *side)
    return outs[0], outs[1:]


def _out_body(x_ref, a_ref, s_ref, wo_ref, fg_ref, wr_ref, br_ref, h_ref, hf_ref, idx_ref, gate_ref, cnt_ref):
    tm = x_ref.shape[0]
    h = x_ref[...] + _dot(a_ref[...], wo_ref[0:ATTN_WIDTH, :]) + _dot(s_ref[...], wo_ref[ATTN_WIDTH:, :])
    h_ref[...] = h
    ms = jnp.mean(h * h, axis=-1, keepdims=True)
    hf = h * lax.rsqrt(ms + EPS) * fg_ref[...]
    hi, lo = _split2(hf)
    hi32 = hi.astype(F32)
    hf_ref[...] = (pltpu.bitcast(hi32[:, D_MODEL // 2:], jnp.uint32)
                   | (pltpu.bitcast(hi32[:, :D_MODEL // 2], jnp.uint32) >> 16))

    whi, wlo = _split2(wr_ref[...])
    hw = _dot(hi, jnp.concatenate([whi, wlo], axis=1))
    logits = hw[:, :LANES] + hw[:, LANES:] + _dot(lo, whi) + br_ref[...]
    lane = _iota((tm, LANES), 1)
    vals, idxs = [], []
    for _ in range(TOP_K):
        m = jnp.max(logits, axis=-1, keepdims=True)
        am = jnp.min(jnp.where(logits == m, lane, LANES), axis=-1, keepdims=True)
        vals.append(m)
        idxs.append(am)
        logits = jnp.where(lane == am, NEG_BIG * 2.0, logits)
    es = [jnp.exp(v - vals[0]) for v in vals]
    tot = es[0] + es[1] + es[2] + es[3]

    @pl.when(pl.program_id(0) == 0)
    def _():
        cnt_ref[...] = jnp.zeros(cnt_ref.shape, F32)

    chosen = jnp.zeros((tm, LANES), F32)
    for k in range(TOP_K):
        chosen = chosen + jnp.where(lane == idxs[k], 1.0, 0.0)
    earlier = jnp.where(_iota((tm, tm), 0) > _iota((tm, tm), 1), 1.0, 0.0).astype(BF16)
    before = _dot(earlier, chosen.astype(BF16)) + cnt_ref[...]
    cnt_ref[...] = cnt_ref[...] + jnp.sum(chosen, axis=0, keepdims=True)

    idx_out = jnp.zeros((tm, LANES), jnp.int32)
    gate_out = jnp.zeros((tm, LANES), F32)
    for k in range(TOP_K):
        rank = jnp.sum(jnp.where(lane == idxs[k], before, 0.0), axis=-1, keepdims=True)
        idx_out = jnp.where(lane == k, idxs[k], idx_out)
        idx_out = jnp.where(lane == TOP_K + k, rank.astype(jnp.int32), idx_out)
        gate_out = jnp.where(lane == k, es[k] / tot, gate_out)
    idx_ref[...] = idx_out
    gate_ref[...] = gate_out


def out_proj_router(x2d, attn, ssm, wo, fg, wr, br):
    t, d = x2d.shape
    tm = min(512, t)
    const = lambda i: (0, 0)
    once = pl.Buffered(1)
    return pl.pallas_call(
        _out_body,
        grid=(t // tm,),
        in_specs=[
            pl.BlockSpec((tm, d), lambda i: (i, 0)),
            pl.BlockSpec((tm, ATTN_WIDTH), lambda i: (i, 0)),
            pl.BlockSpec((tm, SSM_WIDTH), lambda i: (i, 0)),
            pl.BlockSpec((d, d), const, pipeline_mode=once),
            pl.BlockSpec((1, d), const, pipeline_mode=once),
            pl.BlockSpec((d, LANES), const, pipeline_mode=once),
            pl.BlockSpec((1, LANES), const, pipeline_mode=once),
        ],
        out_specs=[
            pl.BlockSpec((tm, d), lambda i: (i, 0)),
            pl.BlockSpec((tm, d // 2), lambda i: (i, 0)),
            pl.BlockSpec((tm, LANES), lambda i: (i, 0)),
            pl.BlockSpec((tm, LANES), lambda i: (i, 0)),
        ],
        out_shape=[
            jax.ShapeDtypeStruct((t, d), F32),
            jax.ShapeDtypeStruct((t, d // 2), jnp.uint32),
            jax.ShapeDtypeStruct((t, LANES), jnp.int32),
            jax.ShapeDtypeStruct((t, LANES), F32),
        ],
        scratch_shapes=[pltpu.VMEM((1, LANES), F32)],
        compiler_params=_params("arbitrary"),
        name="out_proj_router",
    )(x2d, attn, ssm, wo, fg, wr, br)


MOE_ROWS = 512
FF_TILE = 1024
N_FF_TILES = D_FF // FF_TILE


def _expert_body(be_ref, nv_ref, tok_ref, tokn_ref, hf_ref, wg_ref, wu_ref, wd_ref, bg_ref, bu_ref, bd_ref, o_ref,
                 xrow_ref, x_ref, h_ref, gu_ref, dn_ref, row_sem, gu_sem, dn_sem):
    b = pl.program_id(0)
    nv = nv_ref[0]
    tm = h_ref.shape[0]
    half = D_MODEL // 2

    def row_copy(table_ref, r):
        return pltpu.make_async_copy(hf_ref.at[pl.ds(table_ref[0, 0, r], 1)], xrow_ref.at[pl.ds(r, 1)], row_sem)

    def rows_wait():
        pltpu.make_async_copy(hf_ref.at[pl.ds(0, tm)], xrow_ref, row_sem).wait()

    def unpack_rows(slot):
        w = xrow_ref[...]
        x_ref[slot, :, 0:half] = pltpu.bitcast(w << 16, F32).astype(BF16)
        x_ref[slot, :, half:] = pltpu.bitcast(w & jnp.uint32(0xFFFF0000), F32).astype(BF16)

    n_chunks = 2 * N_FF_TILES

    def chunk_copies(e, c):
        j = c % N_FF_TILES
        cols = pl.ds(j * FF_TILE, FF_TILE)
        if c < N_FF_TILES:
            return (pltpu.make_async_copy(wg_ref.at[e, :, cols], gu_ref.at[j, 0], gu_sem.at[j]),
                    pltpu.make_async_copy(wu_ref.at[e, :, cols], gu_ref.at[j, 1], gu_sem.at[j]))
        return (pltpu.make_async_copy(wd_ref.at[e, :, cols], dn_ref.at[j], dn_sem.at[j]),)

    def start_chunk(e, c):
        for cp in chunk_copies(e, c):
            cp.start()

    def next_chunk(e, c):
        for cp in chunk_copies(e, c):
            cp.wait()
        if c == 0:
            start_chunk(e, n_chunks - 1)
        else:
            @pl.when(b + 1 < nv)
            def _():
                start_chunk(be_ref[b + 1], c - 1)

    per_batch = tm // N_FF_TILES

    @pl.when(b >= nv)
    def _():
        o_ref[...] = jnp.zeros(o_ref.shape, F32)

    @pl.when(b < nv)
    def _():
        e = be_ref[b]
        slot = b % 2

        @pl.when(b == 0)
        def _():
            def issue(r, carry):
                row_copy(tok_ref, r).start()
                return carry
            lax.fori_loop(0, tm, issue, 0, unroll=8)
            for c in range(n_chunks - 1):
                start_chunk(e, c)
            rows_wait()
            unpack_rows(0)

        x = x_ref[slot]
        for j in range(N_FF_TILES):
            next_chunk(e, j)
            cols = slice(j * FF_TILE, (j + 1) * FF_TILE)
            g_acc = _dot(x, gu_ref[j, 0])
            u_acc = _dot(x, gu_ref[j, 1])
            for r in range(j * per_batch, (j + 1) * per_batch):
                row_copy(tokn_ref, r).start()
            gl = jnp.minimum(g_acc + bg_ref[0, :, cols], SWIGLU_LIMIT)
            li = jnp.clip(u_acc + bu_ref[0, :, cols], -SWIGLU_LIMIT, SWIGLU_LIMIT)
            h_ref[:, cols] = (gl * _sigmoid(SWIGLU_ALPHA * gl) * (li + 1.0)).astype(BF16)

        h = h_ref[...]
        for j in range(N_FF_TILES):
            next_chunk(e, N_FF_TILES + j)
            if j == N_FF_TILES - 1:
                rows_wait()
                unpack_rows(1 - slot)
            cols = slice(j * FF_TILE, (j + 1) * FF_TILE)
            o_ref[:, cols] = _dot(h, dn_ref[j]) + bd_ref[0, :, cols]


def experts(block_expert, n_valid, slot_tok, hf_packed, wg, bg, wu, bu, wd, bd):
    n_blocks = block_expert.shape[0]
    tm = MOE_ROWS
    d = D_MODEL

    def blk(b, nv):
        return jnp.minimum(b, nv[0] - 1)

    any_spec = pl.BlockSpec(memory_space=pl.ANY)
    grid_spec = pltpu.PrefetchScalarGridSpec(
        num_scalar_prefetch=2,
        grid=(n_blocks,),
        in_specs=[
            pl.BlockSpec((1, 1, tm), lambda b, be, nv: (blk(b, nv), 0, 0), memory_space=pltpu.SMEM),
            pl.BlockSpec((1, 1, tm), lambda b, be, nv: (blk(b + 1, nv), 0, 0), memory_space=pltpu.SMEM),
            any_spec, any_spec, any_spec, any_spec,
            pl.BlockSpec((1, 1, D_FF), lambda b, be, nv: (be[blk(b, nv)], 0, 0)),
            pl.BlockSpec((1, 1, D_FF), lambda b, be, nv: (be[blk(b, nv)], 0, 0)),
            pl.BlockSpec((1, 1, d), lambda b, be, nv: (be[blk(b, nv)], 0, 0)),
        ],
        out_specs=pl.BlockSpec((tm, d), lambda b, be, nv: (b, 0)),
        scratch_shapes=[
            pltpu.VMEM((tm, d // 2), jnp.uint32),
            pltpu.VMEM((2, tm, d), BF16),
            pltpu.VMEM((tm, D_FF), BF16),
            pltpu.VMEM((2, 2, d, FF_TILE), BF16),
            pltpu.VMEM((2, D_FF, FF_TILE), BF16),
            pltpu.SemaphoreType.DMA,
            pltpu.SemaphoreType.DMA((2,)),
            pltpu.SemaphoreType.DMA((2,)),
        ],
    )
    return pl.pallas_call(
        _expert_body,
        grid_spec=grid_spec,
        out_shape=jax.ShapeDtypeStruct((n_blocks * tm, d), F32),
        compiler_params=_params("arbitrary"),
        name="experts",
    )(block_expert, n_valid, slot_tok, slot_tok, hf_packed, wg, wu, wd, bg, bu, bd)


COMBINE_ROWS = 256


def _combine_body(pos_ref, posn_ref, h_ref, gate_ref, ys_ref, o_ref, buf_ref, sem):
    i = pl.program_id(0)
    tm = h_ref.shape[0]
    slot = i % 2

    def row_copy(table_ref, r, k, s):
        return pltpu.make_async_copy(ys_ref.at[pl.ds(table_ref[0, 0, r * TOP_K + k], 1)],
                                     buf_ref.at[s, pl.ds(k * tm + r, 1)], sem.at[s])

    def rows_wait(s):
        pltpu.make_async_copy(ys_ref.at[pl.ds(0, TOP_K * tm)], buf_ref.at[s], sem.at[s]).wait()

    @pl.when(i == 0)
    def _():
        def issue(r, carry):
            for k in range(TOP_K):
                row_copy(pos_ref, r, k, 0).start()
            return carry
        lax.fori_loop(0, tm, issue, 0, unroll=4)

    for r in range(tm):
        for k in range(TOP_K):
            row_copy(posn_ref, r, k, 1 - slot).start()

    rows_wait(slot)
    gates = gate_ref[...]
    acc = h_ref[...]
    for k in range(TOP_K):
        acc = acc + gates[:, k:k + 1] * buf_ref[slot, k * tm:(k + 1) * tm, :]
    o_ref[...] = acc

    @pl.when(i + 1 == pl.num_programs(0))
    def _():
        rows_wait(1 - slot)


def combine(pos, h, gates, ys):
    t, d = h.shape
    tm = min(COMBINE_ROWS, t)
    n = t // tm
    return pl.pallas_call(
        _combine_body,
        grid=(n,),
        in_specs=[
            pl.BlockSpec((1, 1, tm * TOP_K), lambda i: (i, 0, 0), memory_space=pltpu.SMEM),
            pl.BlockSpec((1, 1, tm * TOP_K), lambda i: (jnp.minimum(i + 1, n - 1), 0, 0), memory_space=pltpu.SMEM),
            pl.BlockSpec((tm, d), lambda i: (i, 0)),
            pl.BlockSpec((tm, LANES), lambda i: (i, 0)),
            pl.BlockSpec(memory_space=pl.ANY),
        ],
        out_specs=pl.BlockSpec((tm, d), lambda i: (i, 0)),
        out_shape=jax.ShapeDtypeStruct((t, d), F32),
        scratch_shapes=[
            pltpu.VMEM((2, TOP_K * tm, d), F32),
            pltpu.SemaphoreType.DMA((2,)),
        ],
        compiler_params=_params("arbitrary"),
        name="combine",
    )(pos, pos, h, gates, ys)


def routing_tables(top_idx, rank, t):
    n_assign = t * TOP_K
    tm_c = min(COMBINE_ROWS, t)
    e_flat = top_idx.reshape(-1).astype(jnp.int32)
    position = jnp.arange(n_assign, dtype=jnp.int32)
    _, order = lax.sort((e_flat, position), num_keys=1, is_stable=True)
    experts_iota = jnp.arange(N_EXPERTS, dtype=jnp.int32)
    counts = jnp.sum((e_flat[None, :] == experts_iota[:, None]).astype(jnp.int32), axis=1)
    starts = jnp.cumsum(counts) - counts
    padded = (counts + MOE_ROWS - 1) // MOE_ROWS * MOE_ROWS
    pends = jnp.cumsum(padded)
    pstarts = pends - padded
    n_blocks = -(-n_assign // MOE_ROWS) + N_EXPERTS
    block_start = jnp.arange(n_blocks, dtype=jnp.int32) * MOE_ROWS
    block_expert = jnp.minimum(
        jnp.sum((pends[None, :] <= block_start[:, None]).astype(jnp.int32), axis=1), N_EXPERTS - 1)
    n_valid = (pends[-1:] // MOE_ROWS).astype(jnp.int32)

    row = jnp.arange(MOE_ROWS, dtype=jnp.int32)[None, :]
    e_b = block_expert[:, None]
    local = jnp.arange(n_blocks, dtype=jnp.int32)[:, None] * MOE_ROWS + row - pstarts[e_b]
    used = (local >= 0) & (local < counts[e_b])
    asg = order[jnp.clip(starts[e_b] + local, 0, n_assign - 1)]
    slot_tok = jnp.where(used, asg // TOP_K, 0).reshape(n_blocks, 1, MOE_ROWS)
    pos = pstarts[e_flat] + rank.reshape(-1)
    return block_expert, n_valid, slot_tok, pos.reshape(t // tm_c, 1, tm_c * TOP_K)


def _pad_lanes(v, width, value=0.0):
    return jnp.pad(v, ((0, 0), (0, width - v.shape[1])), constant_values=value)


def hybrid_layer(x, attn_norm_w, w_in, q_norm_w, k_norm_w, attn_sinks, conv_w, conv_b, dt_bias, a_log, d_skip,
                 ssm_norm_w, attn_out_norm_w, w_out, ffn_norm_w, w_router, b_router, w_gate, b_gate, w_up, b_up,
                 w_down, b_down):
    batch, seq, d = x.shape
    t = batch * seq
    x2d = x.reshape(t, d)

    q_w, k_w, v_w, z_w, xbc_w, dt_w = jnp.split(
        w_in, [ATTN_WIDTH, ATTN_WIDTH + KV_WIDTH, ATTN_WIDTH + 2 * KV_WIDTH,
               ATTN_WIDTH + 2 * KV_WIDTH + SSM_WIDTH, ATTN_WIDTH + 2 * KV_WIDTH + SSM_WIDTH + CONV_WIDTH], axis=1)
    w_p = jnp.concatenate([q_w, z_w, xbc_w, k_w, v_w, dt_w], axis=1)
    w_p = _pad_lanes(w_p, PROJ_WIDTH).astype(BF16)
    proj = in_proj(x2d, attn_norm_w.reshape(1, d), w_p)

    qw2 = (jnp.tile(q_norm_w, 2) * (HEAD_DIM ** -0.5)).reshape(1, LANES)
    kw2 = jnp.tile(k_norm_w, 2).reshape(1, LANES)
    attn, (wg16, wu16) = attention(
        proj, attn_sinks, qw2, kw2, attn_out_norm_w.reshape(1, ATTN_WIDTH), batch, seq,
        side=(w_gate.reshape(N_EXPERTS * D_MODEL, D_FF), w_up.reshape(N_EXPERTS * D_MODEL, D_FF)))

    ssm, (wd16,) = ssd(
        proj,
        conv_w[:, :SSM_WIDTH], conv_w[:, SSM_WIDTH:],
        conv_b[:SSM_WIDTH].reshape(1, -1), conv_b[SSM_WIDTH:].reshape(1, -1),
        _pad_lanes(dt_bias.reshape(1, -1), LANES), _pad_lanes(a_log.reshape(1, -1), LANES),
        jnp.repeat(d_skip, HEAD_DIM).reshape(1, SSM_WIDTH), ssm_norm_w.reshape(1, SSM_WIDTH),
        batch, seq, side=(w_down.reshape(N_EXPERTS * D_FF, D_MODEL),))

    h, hf_packed, idx_pad, gate_pad = out_proj_router(
        x2d, attn, ssm, w_out.astype(BF16), ffn_norm_w.reshape(1, d),
        _pad_lanes(w_router, LANES), _pad_lanes(b_router.reshape(1, -1), LANES, NEG_BIG))

    block_expert, n_valid, slot_tok, pos = routing_tables(idx_pad[:, :TOP_K], idx_pad[:, TOP_K:2 * TOP_K], t)
    ys = experts(
        block_expert, n_valid, slot_tok, hf_packed,
        wg16.reshape(N_EXPERTS, D_MODEL, D_FF), b_gate.reshape(N_EXPERTS, 1, D_FF),
        wu16.reshape(N_EXPERTS, D_MODEL, D_FF), b_up.reshape(N_EXPERTS, 1, D_FF),
        wd16.reshape(N_EXPERTS, D_FF, D_MODEL), b_down.reshape(N_EXPERTS, 1, D_MODEL))

    out = combine(pos, h, gate_pad, ys)
    return out.reshape(batch, seq, d)


def kernel(x, attn_norm_w, w_in, q_norm_w, k_norm_w, attn_sinks, conv_w, conv_b, dt_bias, a_log, d_skip, ssm_norm_w,
           attn_out_norm_w, w_out, ffn_norm_w, w_router, b_router, w_gate, b_gate, w_up, b_up, w_down, b_down):
    for i in range(attn_norm_w.shape[0]):
        x = hybrid_layer(x, attn_norm_w[i], w_in[i], q_norm_w[i], k_norm_w[i], attn_sinks[i], conv_w[i], conv_b[i],
                         dt_bias[i], a_log[i], d_skip[i], ssm_norm_w[i], attn_out_norm_w[i], w_out[i], ffn_norm_w[i],
                         w_router[i], b_router[i], w_gate[i], b_gate[i], w_up[i], b_up[i], w_down[i], b_down[i])
    return x
```

```python
import functools

import jax
import jax.numpy as jnp
from jax import lax
from jax.experimental import pallas as pl
from jax.experimental.pallas import tpu as pltpu

F32 = jnp.float32
BF16 = jnp.bfloat16

D_MODEL = 2048
HEAD_DIM = 64
ATTN_WIDTH = 1024
KV_WIDTH = 128
WINDOW = 128
SSM_WIDTH = 1024
SSM_HEADS = 16
SSM_GROUPS = 2
SSM_STATE = 128
SSM_CONV = 4
SSM_CHUNK = 128
BC_WIDTH = SSM_GROUPS * SSM_STATE
CONV_WIDTH = SSM_WIDTH + 2 * BC_WIDTH
N_EXPERTS = 32
TOP_K = 4
D_FF = 2048
SWIGLU_LIMIT = 7.0
SWIGLU_ALPHA = 1.702
EPS = 1e-6

LANES = 128
SUBLANES = 8
VMEM_LIMIT = 56 * 1024 * 1024

PROJ_WIDTH = 4096
COL_Q = 0
COL_Z = ATTN_WIDTH
COL_XBC = COL_Z + SSM_WIDTH
COL_K = COL_XBC + CONV_WIDTH
COL_V = COL_K + KV_WIDTH
COL_DT = COL_V + KV_WIDTH
NEG_BIG = -1e30


def _params(*sem):
    return pltpu.CompilerParams(dimension_semantics=sem, vmem_limit_bytes=VMEM_LIMIT)


def _split2(v):
    hi = v.astype(BF16)
    lo = (v - hi.astype(F32)).astype(BF16)
    return hi, lo


def _split3(v):
    hi = v.astype(BF16)
    r = v - hi.astype(F32)
    mid = r.astype(BF16)
    lo = (r - mid.astype(F32)).astype(BF16)
    return hi, mid, lo


def _dot(a, b):
    return jnp.dot(a, b, preferred_element_type=F32)


def _dot_exact_lhs(parts, m):
    acc = _dot(parts[0], m)
    for p in parts[1:]:
        acc = acc + _dot(p, m)
    return acc


def _sigmoid(v):
    return 1.0 / (1.0 + jnp.exp(-v))


def _iota(shape, dim):
    return lax.broadcasted_iota(jnp.int32, shape, dim)


def _in_proj_body(x0_ref, xn_ref, g_ref, w_ref, o_ref, hn_ref):
    i = pl.program_id(0)

    def norm(x):
        ms = jnp.mean(x * x, axis=-1, keepdims=True)
        return (x * lax.rsqrt(ms + EPS) * g_ref[...]).astype(BF16)

    @pl.when(i == 0)
    def _():
        hn_ref[0] = norm(x0_ref[...])

    slot = i % 2
    hn_ref[1 - slot] = norm(xn_ref[...])
    o_ref[...] = _dot(hn_ref[slot], w_ref[...])


def in_proj(x2d, g, w_p):
    t, d = x2d.shape
    n = w_p.shape[1]
    tm = min(512, t)
    ni = t // tm
    once = pl.Buffered(1)
    return pl.pallas_call(
        _in_proj_body,
        grid=(ni,),
        in_specs=[
            pl.BlockSpec((tm, d), lambda i: (0, 0), pipeline_mode=once),
            pl.BlockSpec((tm, d), lambda i: (jnp.minimum(i + 1, ni - 1), 0)),
            pl.BlockSpec((1, d), lambda i: (0, 0), pipeline_mode=once),
            pl.BlockSpec((d, n), lambda i: (0, 0), pipeline_mode=once),
        ],
        out_specs=pl.BlockSpec((tm, n), lambda i: (i, 0)),
        out_shape=jax.ShapeDtypeStruct((t, n), F32),
        scratch_shapes=[pltpu.VMEM((2, tm, d), BF16)],
        compiler_params=_params("arbitrary"),
        name="in_proj",
    )(x2d, x2d, g, w_p)


def _head_pair_norm(v, w, blockdiag):
    ss = _dot_exact_lhs(_split2(v * v), blockdiag)
    return v * lax.rsqrt(ss * (1.0 / HEAD_DIM) + EPS) * w


def _cast_side_streams(refs, n_side):
    for k in range(n_side):
        refs[n_side + 1 + k][...] = refs[k][...].astype(BF16)
    return refs[n_side]


def _side_specs(side, n_steps, step_index):
    specs, shapes = [], []
    for w in side:
        rows, cols = w.shape
        specs.append(pl.BlockSpec((rows // n_steps, cols), step_index))
        shapes.append(jax.ShapeDtypeStruct((rows, cols), BF16))
    return specs, shapes


def _attn_body(n_side, sinks_ref, q_ref, kvc_ref, kvp_ref, qw_ref, kw_ref, onw_ref, *rest):
    o_ref = _cast_side_streams(rest, n_side)
    acc_ref = rest[-1]
    i = pl.program_id(1)
    tq = q_ref.shape[0]
    nsub = tq // WINDOW
    nk = tq + WINDOW
    lo_lanes = _iota((1, LANES), 1) < HEAD_DIM
    blockdiag = jnp.where(
        _iota((LANES, LANES), 0) // HEAD_DIM == _iota((LANES, LANES), 1) // HEAD_DIM, 1.0, 0.0
    ).astype(BF16)

    kv_all = jnp.concatenate([kvp_ref[...], kvc_ref[...]], axis=0)
    k_all = _head_pair_norm(kv_all[:, :KV_WIDTH], kw_ref[...], blockdiag)
    v_all = kv_all[:, KV_WIDTH:]
    kt = k_all.T
    zero_half = jnp.zeros((HEAD_DIM, nk), F32)
    kt_var = [
        [jnp.concatenate([kt[j * HEAD_DIM:(j + 1) * HEAD_DIM], zero_half], axis=0).astype(BF16),
         jnp.concatenate([zero_half, kt[j * HEAD_DIM:(j + 1) * HEAD_DIM]], axis=0).astype(BF16)]
        for j in range(2)
    ]
    v_sw = pltpu.roll(v_all, HEAD_DIM, axis=1)
    v_var = [
        [jnp.where(lo_lanes, v_all, 0.0), jnp.where(lo_lanes, 0.0, v_sw)],
        [jnp.where(lo_lanes, v_sw, 0.0), jnp.where(lo_lanes, 0.0, v_all)],
    ]

    qn = [
        _head_pair_norm(q_ref[:, c * LANES:(c + 1) * LANES], qw_ref[...], blockdiag).astype(BF16)
        for c in range(ATTN_WIDTH // LANES)
    ]

    rows = 4 * WINDOW
    qi = _iota((rows, 2 * WINDOW), 0) % WINDOW
    kj = _iota((rows, 2 * WINDOW), 1)
    band = (kj > qi) & (kj <= qi + WINDOW)
    first_min = jnp.where(i == 0, WINDOW, 0)
    band_first = band & (kj >= first_min)
    sink_col = _iota((WINDOW, 2 * WINDOW), 1) == 0
    fill = [[jnp.concatenate([jnp.where(sink_col, sinks_ref[8 * j + 2 * c + par], NEG_BIG) for c in range(4)], axis=0)
             for par in range(2)] for j in range(2)]
    not_first_key = _iota((2 * WINDOW, LANES), 0) > 0
    lo_band = _iota((2 * WINDOW, LANES), 1) < HEAD_DIM
    ones_half = [jnp.where(lo_band, 1.0, 0.0).astype(BF16), jnp.where(lo_band, 0.0, 1.0).astype(BF16)]

    for s in range(nsub):
        mask = band_first if s == 0 else band
        keys = slice(s * WINDOW, s * WINDOW + 2 * WINDOW)
        for j in range(2):
            qs = jnp.concatenate([qn[4 * j + c][s * WINDOW:(s + 1) * WINDOW] for c in range(4)], axis=0)
            num = None
            den = None
            for par in range(2):
                sc = jnp.where(mask, _dot(qs, kt_var[j][par][:, keys]), fill[j][par])
                m = jnp.max(sc, axis=-1, keepdims=True)
                p = jnp.exp(sc - m).astype(BF16)
                v_band = jnp.where(not_first_key, v_var[j][par][keys], 0.0).astype(BF16)
                pv = _dot(p, v_band)
                ps = _dot(p, ones_half[par])
                num = pv if num is None else num + pv
                den = ps if den is None else den + ps
            out = num / den
            for c in range(4):
                acc_ref[s * WINDOW:(s + 1) * WINDOW, (4 * j + c) * LANES:(4 * j + c + 1) * LANES] = (
                    out[c * WINDOW:(c + 1) * WINDOW])

    a = acc_ref[...]
    ms = jnp.mean(a * a, axis=-1, keepdims=True)
    o_ref[...] = (a * lax.rsqrt(ms + EPS) * onw_ref[...]).astype(BF16)


def attention(proj, sinks, qw2, kw2, onw, batch, seq, side=()):
    t = proj.shape[0]
    tq = min(256, seq)
    nq = seq // tq
    sub = tq // WINDOW
    kv_col = COL_K // (2 * KV_WIDTH)
    side_specs, side_shapes = _side_specs(side, batch * nq, lambda b, i, s: (b * nq + i, 0))
    grid_spec = pltpu.PrefetchScalarGridSpec(
        num_scalar_prefetch=1,
        grid=(batch, nq),
        in_specs=[
            pl.BlockSpec((tq, ATTN_WIDTH), lambda b, i, s: (b * nq + i, COL_Q // ATTN_WIDTH)),
            pl.BlockSpec((tq, 2 * KV_WIDTH), lambda b, i, s: (b * nq + i, kv_col)),
            pl.BlockSpec((WINDOW, 2 * KV_WIDTH),
                         lambda b, i, s: (b * nq * sub + jnp.maximum(i * sub - 1, 0), kv_col)),
            pl.BlockSpec((1, LANES), lambda b, i, s: (0, 0)),
            pl.BlockSpec((1, LANES), lambda b, i, s: (0, 0)),
            pl.BlockSpec((1, ATTN_WIDTH), lambda b, i, s: (0, 0)),
        ] + side_specs,
        out_specs=[pl.BlockSpec((tq, ATTN_WIDTH), lambda b, i, s: (b * nq + i, 0))] + side_specs,
        scratch_shapes=[pltpu.VMEM((tq, ATTN_WIDTH), F32)],
    )
    outs = pl.pallas_call(
        functools.partial(_attn_body, len(side)),
        grid_spec=grid_spec,
        out_shape=[jax.ShapeDtypeStruct((t, ATTN_WIDTH), BF16)] + side_shapes,
        compiler_params=_params("arbitrary", "arbitrary"),
        name="attention",
    )(sinks, proj, proj, proj, qw2, kw2, onw, *side)
    return outs[0], outs[1:]


def _ssd_body(n_side, z_ref, xs_ref, bc_ref, dt_ref, cwx_ref, cwb_ref, cbx_ref, cbb_ref, dtb_ref, alog_ref,
              dsk_ref, nw_ref, *rest):
    o_ref = _cast_side_streams(rest, n_side)
    xpx_ref, xpb_ref, st_ref, y_ref = rest[-4:]
    i = pl.program_id(1)
    ts = xs_ref.shape[0]
    nch = ts // SSM_CHUNK
    gw = SSM_WIDTH // SSM_GROUPS
    pad = SUBLANES

    @pl.when(i == 0)
    def _():
        xpx_ref[0:pad, :] = jnp.zeros((pad, SSM_WIDTH), F32)
        xpb_ref[0:pad, :] = jnp.zeros((pad, 2 * BC_WIDTH), F32)
        st_ref[...] = jnp.zeros(st_ref.shape, F32)

    xpx_ref[pad:pad + ts, :] = xs_ref[...]
    xpb_ref[pad:pad + ts, :] = bc_ref[...]

    def conv_silu(xp_ref, w_ref, b_ref):
        acc = b_ref[...] + w_ref[SSM_CONV - 1:SSM_CONV, :] * xp_ref[pad:pad + ts, :]
        for k in range(SSM_CONV - 1):
            off = pad - (SSM_CONV - 1) + k
            acc = acc + w_ref[k:k + 1, :] * xp_ref[off:off + ts, :]
        return acc * _sigmoid(acc)

    xc = conv_silu(xpx_ref, cwx_ref, cbx_ref)
    bcc = conv_silu(xpb_ref, cwb_ref, cbb_ref)
    xpx_ref[0:pad, :] = xs_ref[ts - pad:ts, :]
    xpb_ref[0:pad, :] = bc_ref[ts - pad:ts, :]

    dtr = dt_ref[...] + dtb_ref[...]
    dt = jnp.maximum(dtr, 0.0) + jnp.log(1.0 + jnp.exp(-jnp.abs(dtr)))
    a = -jnp.exp(alog_ref[...])
    da = dt * a

    ri = _iota((ts, ts), 0)
    ci = _iota((ts, ts), 1)
    same = (ri // SSM_CHUNK) == (ci // SSM_CHUNK)
    tri_blk = jnp.where(same & (ci <= ri), 1.0, 0.0).astype(BF16)
    all_blk = jnp.where(same, 1.0, 0.0).astype(BF16)
    da3 = _split3(da)
    cs = _dot(tri_blk, da3[0]) + _dot(tri_blk, da3[1]) + _dot(tri_blk, da3[2])
    cl = _dot(all_blk, da3[0]) + _dot(all_blk, da3[1]) + _dot(all_blk, da3[2])
    ecs = jnp.exp(cs)
    dte = jnp.exp(cl - cs)

    expand = jnp.where(_iota((LANES, SSM_WIDTH), 1) // HEAD_DIM == _iota((LANES, SSM_WIDTH), 0),
                       1.0, 0.0).astype(BF16)
    dt_e = _dot_exact_lhs(_split2(dt), expand)
    dtd_e = _dot_exact_lhs(_split2(dt * dte), expand)
    ecs_e = _dot_exact_lhs(_split2(ecs), expand)
    xf = xc * dt_e
    xd = (xc * dtd_e).astype(BF16)

    lo_lanes = _iota((1, LANES), 1) < HEAD_DIM
    li = _iota((SSM_CHUNK, SSM_CHUNK), 0)
    lj = _iota((SSM_CHUNK, SSM_CHUNK), 1)
    tril = lj <= li
    upper = jnp.where(li <= lj, 1.0, 0.0).astype(BF16)

    for c in range(nch):
        r0 = c * SSM_CHUNK
        da_t = da[r0:r0 + SSM_CHUNK].T
        cs_t = _dot_exact_lhs(_split3(da_t), upper)
        cs_c = cs[r0:r0 + SSM_CHUNK]
        for g in range(SSM_GROUPS):
            b_c = bcc[r0:r0 + SSM_CHUNK, g * SSM_STATE:(g + 1) * SSM_STATE]
            c_c = bcc[r0:r0 + SSM_CHUNK, BC_WIDTH + g * SSM_STATE:BC_WIDTH + (g + 1) * SSM_STATE].astype(BF16)
            b_t = b_c.T.astype(BF16)
            cb = _dot(c_c, b_t)
            st = st_ref[g]
            y_off = _dot(c_c, st.astype(BF16)) * ecs_e[r0:r0 + SSM_CHUNK, g * gw:(g + 1) * gw]
            s_new = _dot(b_t, xd[r0:r0 + SSM_CHUNK, g * gw:(g + 1) * gw])
            parts = []
            for p in range(4):
                h0 = 8 * g + 2 * p
                ms = []
                for h in (h0, h0 + 1):
                    diff = cs_c[:, h:h + 1] - cs_t[h:h + 1, :]
                    ms.append((cb * jnp.exp(jnp.where(tril, diff, NEG_BIG))).astype(BF16))
                m2 = jnp.concatenate(ms, axis=1)
                x2 = xf[r0:r0 + SSM_CHUNK, h0 * HEAD_DIM:(h0 + 2) * HEAD_DIM]
                xbd = jnp.concatenate([jnp.where(lo_lanes, x2, 0.0), jnp.where(lo_lanes, 0.0, x2)],
                                      axis=0).astype(BF16)
                parts.append(_dot(m2, xbd))
            y_ref[r0:r0 + SSM_CHUNK, g * gw:(g + 1) * gw] = jnp.concatenate(parts, axis=1) + y_off
            dec = ecs_e[r0 + SSM_CHUNK - 1:r0 + SSM_CHUNK, g * gw:(g + 1) * gw]
            st_ref[g] = st * dec + s_new

    y = y_ref[...] + xc * dsk_ref[...]
    z = z_ref[...]
    gated = y * (z * _sigmoid(z))
    outs = []
    for g in range(SSM_GROUPS):
        gg = gated[:, g * gw:(g + 1) * gw]
        ms = jnp.mean(gg * gg, axis=-1, keepdims=True)
        outs.append(gg * lax.rsqrt(ms + EPS) * nw_ref[:, g * gw:(g + 1) * gw])
    o_ref[...] = jnp.concatenate(outs, axis=1).astype(BF16)


def ssd(proj, cwx, cwb, cbx, cbb, dtb, alog, dsk, nw, batch, seq, side=()):
    t = proj.shape[0]
    ts = min(256, seq)
    ns = seq // ts
    row = lambda b, i: b * ns + i
    const = lambda b, i: (0, 0)
    side_specs, side_shapes = _side_specs(side, batch * ns, lambda b, i: (row(b, i), 0))
    outs = pl.pallas_call(
        functools.partial(_ssd_body, len(side)),
        grid=(batch, ns),
        in_specs=[
            pl.BlockSpec((ts, SSM_WIDTH), lambda b, i: (row(b, i), COL_Z // SSM_WIDTH)),
            pl.BlockSpec((ts, SSM_WIDTH), lambda b, i: (row(b, i), COL_XBC // SSM_WIDTH)),
            pl.BlockSpec((ts, 2 * BC_WIDTH), lambda b, i: (row(b, i), (COL_XBC + SSM_WIDTH) // (2 * BC_WIDTH))),
            pl.BlockSpec((ts, LANES), lambda b, i: (row(b, i), COL_DT // LANES)),
            pl.BlockSpec((SSM_CONV, SSM_WIDTH), const),
            pl.BlockSpec((SSM_CONV, 2 * BC_WIDTH), const),
            pl.BlockSpec((1, SSM_WIDTH), const),
            pl.BlockSpec((1, 2 * BC_WIDTH), const),
            pl.BlockSpec((1, LANES), const),
            pl.BlockSpec((1, LANES), const),
            pl.BlockSpec((1, SSM_WIDTH), const),
            pl.BlockSpec((1, SSM_WIDTH), const),
        ] + side_specs,
        out_specs=[pl.BlockSpec((ts, SSM_WIDTH), lambda b, i: (row(b, i), 0))] + side_specs,
        out_shape=[jax.ShapeDtypeStruct((t, SSM_WIDTH), BF16)] + side_shapes,
        scratch_shapes=[
            pltpu.VMEM((ts + SUBLANES, SSM_WIDTH), F32),
            pltpu.VMEM((ts + SUBLANES, 2 * BC_WIDTH), F32),
            pltpu.VMEM((SSM_GROUPS, SSM_STATE, SSM_WIDTH // SSM_GROUPS), F32),
            pltpu.VMEM((ts, SSM_WIDTH), F32),
        ],
        compiler_params=_params("arbitrary", "arbitrary"),
        name="ssd",
    )(proj, proj, proj, proj, cwx, cwb, cbx, cbb, dtb, alog, dsk, nw, *side)
    return outs[0], outs[1:]


def _out_body(x_ref, a_ref, s_ref, wo_ref, fg_ref, wr_ref, br_ref, h_ref, hf_ref, idx_ref, gate_ref):
    tm = x_ref.shape[0]
    h = x_ref[...] + _dot(a_ref[...], wo_ref[0:ATTN_WIDTH, :]) + _dot(s_ref[...], wo_ref[ATTN_WIDTH:, :])
    h_ref[...] = h
    ms = jnp.mean(h * h, axis=-1, keepdims=True)
    hf = h * lax.rsqrt(ms + EPS) * fg_ref[...]
    hi, lo = _split2(hf)
    hi32 = hi.astype(F32)
    hf_ref[...] = (pltpu.bitcast(hi32[:, D_MODEL // 2:], jnp.uint32)
                   | (pltpu.bitcast(hi32[:, :D_MODEL // 2], jnp.uint32) >> 16))

    whi, wlo = _split2(wr_ref[...])
    hw = _dot(hi, jnp.concatenate([whi, wlo], axis=1))
    logits = hw[:, :LANES] + hw[:, LANES:] + _dot(lo, whi) + br_ref[...]
    lane = _iota((tm, LANES), 1)
    vals, idxs = [], []
    for _ in range(TOP_K):
        m = jnp.max(logits, axis=-1, keepdims=True)
        am = jnp.min(jnp.where(logits == m, lane, LANES), axis=-1, keepdims=True)
        vals.append(m)
        idxs.append(am)
        logits = jnp.where(lane == am, NEG_BIG * 2.0, logits)
    es = [jnp.exp(v - vals[0]) for v in vals]
    tot = es[0] + es[1] + es[2] + es[3]
    idx_out = jnp.zeros((tm, LANES), jnp.int32)
    gate_out = jnp.zeros((tm, LANES), F32)
    for k in range(TOP_K):
        idx_out = jnp.where(lane == k, idxs[k], idx_out)
        gate_out = jnp.where(lane == k, es[k] / tot, gate_out)
    idx_ref[...] = idx_out
    gate_ref[...] = gate_out


def out_proj_router(x2d, attn, ssm, wo, fg, wr, br):
    t, d = x2d.shape
    tm = min(512, t)
    const = lambda i: (0, 0)
    once = pl.Buffered(1)
    return pl.pallas_call(
        _out_body,
        grid=(t // tm,),
        in_specs=[
            pl.BlockSpec((tm, d), lambda i: (i, 0)),
            pl.BlockSpec((tm, ATTN_WIDTH), lambda i: (i, 0)),
            pl.BlockSpec((tm, SSM_WIDTH), lambda i: (i, 0)),
            pl.BlockSpec((d, d), const, pipeline_mode=once),
            pl.BlockSpec((1, d), const, pipeline_mode=once),
            pl.BlockSpec((d, LANES), const, pipeline_mode=once),
            pl.BlockSpec((1, LANES), const, pipeline_mode=once),
        ],
        out_specs=[
            pl.BlockSpec((tm, d), lambda i: (i, 0)),
            pl.BlockSpec((tm, d // 2), lambda i: (i, 0)),
            pl.BlockSpec((tm, LANES), lambda i: (i, 0)),
            pl.BlockSpec((tm, LANES), lambda i: (i, 0)),
        ],
        out_shape=[
            jax.ShapeDtypeStruct((t, d), F32),
            jax.ShapeDtypeStruct((t, d // 2), jnp.uint32),
            jax.ShapeDtypeStruct((t, LANES), jnp.int32),
            jax.ShapeDtypeStruct((t, LANES), F32),
        ],
        compiler_params=_params("arbitrary"),
        name="out_proj_router",
    )(x2d, attn, ssm, wo, fg, wr, br)


MOE_ROWS = 512
FF_TILE = 1024
N_FF_TILES = D_FF // FF_TILE


def _expert_body(be_ref, nv_ref, tok_ref, tokn_ref, hf_ref, wg_ref, wu_ref, wd_ref, bg_ref, bu_ref, bd_ref, o_ref,
                 xrow_ref, x_ref, h_ref, gu_ref, dn_ref, row_sem, gu_sem, dn_sem):
    b = pl.program_id(0)
    nv = nv_ref[0]
    tm = h_ref.shape[0]
    half = D_MODEL // 2

    def row_copy(table_ref, r):
        return pltpu.make_async_copy(hf_ref.at[pl.ds(table_ref[0, 0, r], 1)], xrow_ref.at[pl.ds(r, 1)], row_sem)

    def rows_wait():
        pltpu.make_async_copy(hf_ref.at[pl.ds(0, tm)], xrow_ref, row_sem).wait()

    def unpack_rows(slot):
        w = xrow_ref[...]
        x_ref[slot, :, 0:half] = pltpu.bitcast(w << 16, F32).astype(BF16)
        x_ref[slot, :, half:] = pltpu.bitcast(w & jnp.uint32(0xFFFF0000), F32).astype(BF16)

    n_chunks = 2 * N_FF_TILES

    def chunk_copies(e, c):
        j = c % N_FF_TILES
        cols = pl.ds(j * FF_TILE, FF_TILE)
        if c < N_FF_TILES:
            return (pltpu.make_async_copy(wg_ref.at[e, :, cols], gu_ref.at[j, 0], gu_sem.at[j]),
                    pltpu.make_async_copy(wu_ref.at[e, :, cols], gu_ref.at[j, 1], gu_sem.at[j]))
        return (pltpu.make_async_copy(wd_ref.at[e, :, cols], dn_ref.at[j], dn_sem.at[j]),)

    def start_chunk(e, c):
        for cp in chunk_copies(e, c):
            cp.start()

    def next_chunk(e, c):
        for cp in chunk_copies(e, c):
            cp.wait()
        if c == 0:
            start_chunk(e, n_chunks - 1)
        else:
            @pl.when(b + 1 < nv)
            def _():
                start_chunk(be_ref[b + 1], c - 1)

    per_batch = tm // N_FF_TILES

    @pl.when(b >= nv)
    def _():
        o_ref[...] = jnp.zeros(o_ref.shape, F32)

    @pl.when(b < nv)
    def _():
        e = be_ref[b]
        slot = b % 2

        @pl.when(b == 0)
        def _():
            def issue(r, carry):
                row_copy(tok_ref, r).start()
                return carry
            lax.fori_loop(0, tm, issue, 0, unroll=8)
            for c in range(n_chunks - 1):
                start_chunk(e, c)
            rows_wait()
            unpack_rows(0)

        x = x_ref[slot]
        for j in range(N_FF_TILES):
            next_chunk(e, j)
            cols = slice(j * FF_TILE, (j + 1) * FF_TILE)
            g_acc = _dot(x, gu_ref[j, 0])
            u_acc = _dot(x, gu_ref[j, 1])
            for r in range(j * per_batch, (j + 1) * per_batch):
                row_copy(tokn_ref, r).start()
            gl = jnp.minimum(g_acc + bg_ref[0, :, cols], SWIGLU_LIMIT)
            li = jnp.clip(u_acc + bu_ref[0, :, cols], -SWIGLU_LIMIT, SWIGLU_LIMIT)
            h_ref[:, cols] = (gl * _sigmoid(SWIGLU_ALPHA * gl) * (li + 1.0)).astype(BF16)

        h = h_ref[...]
        for j in range(N_FF_TILES):
            next_chunk(e, N_FF_TILES + j)
            if j == N_FF_TILES - 1:
                rows_wait()
                unpack_rows(1 - slot)
            cols = slice(j * FF_TILE, (j + 1) * FF_TILE)
            o_ref[:, cols] = _dot(h, dn_ref[j]) + bd_ref[0, :, cols]


def experts(block_expert, n_valid, slot_tok, hf_packed, wg, bg, wu, bu, wd, bd):
    n_blocks = block_expert.shape[0]
    tm = MOE_ROWS
    d = D_MODEL

    def blk(b, nv):
        return jnp.minimum(b, nv[0] - 1)

    any_spec = pl.BlockSpec(memory_space=pl.ANY)
    grid_spec = pltpu.PrefetchScalarGridSpec(
        num_scalar_prefetch=2,
        grid=(n_blocks,),
        in_specs=[
            pl.BlockSpec((1, 1, tm), lambda b, be, nv: (blk(b, nv), 0, 0), memory_space=pltpu.SMEM),
            pl.BlockSpec((1, 1, tm), lambda b, be, nv: (blk(b + 1, nv), 0, 0), memory_space=pltpu.SMEM),
            any_spec, any_spec, any_spec, any_spec,
            pl.BlockSpec((1, 1, D_FF), lambda b, be, nv: (be[blk(b, nv)], 0, 0)),
            pl.BlockSpec((1, 1, D_FF), lambda b, be, nv: (be[blk(b, nv)], 0, 0)),
            pl.BlockSpec((1, 1, d), lambda b, be, nv: (be[blk(b, nv)], 0, 0)),
        ],
        out_specs=pl.BlockSpec((tm, d), lambda b, be, nv: (b, 0)),
        scratch_shapes=[
            pltpu.VMEM((tm, d // 2), jnp.uint32),
            pltpu.VMEM((2, tm, d), BF16),
            pltpu.VMEM((tm, D_FF), BF16),
            pltpu.VMEM((2, 2, d, FF_TILE), BF16),
            pltpu.VMEM((2, D_FF, FF_TILE), BF16),
            pltpu.SemaphoreType.DMA,
            pltpu.SemaphoreType.DMA((2,)),
            pltpu.SemaphoreType.DMA((2,)),
        ],
    )
    return pl.pallas_call(
        _expert_body,
        grid_spec=grid_spec,
        out_shape=jax.ShapeDtypeStruct((n_blocks * tm, d), F32),
        compiler_params=_params("arbitrary"),
        name="experts",
    )(block_expert, n_valid, slot_tok, slot_tok, hf_packed, wg, wu, wd, bg, bu, bd)


COMBINE_ROWS = 256


def _combine_body(pos_ref, posn_ref, h_ref, gate_ref, ys_ref, o_ref, buf_ref, sem):
    i = pl.program_id(0)
    tm = h_ref.shape[0]
    slot = i % 2

    def row_copy(table_ref, r, k, s):
        return pltpu.make_async_copy(ys_ref.at[pl.ds(table_ref[0, 0, r * TOP_K + k], 1)],
                                     buf_ref.at[s, pl.ds(k * tm + r, 1)], sem.at[s])

    def rows_wait(s):
        pltpu.make_async_copy(ys_ref.at[pl.ds(0, TOP_K * tm)], buf_ref.at[s], sem.at[s]).wait()

    @pl.when(i == 0)
    def _():
        def issue(r, carry):
            for k in range(TOP_K):
                row_copy(pos_ref, r, k, 0).start()
            return carry
        lax.fori_loop(0, tm, issue, 0, unroll=4)

    for r in range(tm):
        for k in range(TOP_K):
            row_copy(posn_ref, r, k, 1 - slot).start()

    rows_wait(slot)
    gates = gate_ref[...]
    acc = h_ref[...]
    for k in range(TOP_K):
        acc = acc + gates[:, k:k + 1] * buf_ref[slot, k * tm:(k + 1) * tm, :]
    o_ref[...] = acc

    @pl.when(i + 1 == pl.num_programs(0))
    def _():
        rows_wait(1 - slot)


def combine(pos, h, gates, ys):
    t, d = h.shape
    tm = min(COMBINE_ROWS, t)
    n = t // tm
    return pl.pallas_call(
        _combine_body,
        grid=(n,),
        in_specs=[
            pl.BlockSpec((1, 1, tm * TOP_K), lambda i: (i, 0, 0), memory_space=pltpu.SMEM),
            pl.BlockSpec((1, 1, tm * TOP_K), lambda i: (jnp.minimum(i + 1, n - 1), 0, 0), memory_space=pltpu.SMEM),
            pl.BlockSpec((tm, d), lambda i: (i, 0)),
            pl.BlockSpec((tm, LANES), lambda i: (i, 0)),
            pl.BlockSpec(memory_space=pl.ANY),
        ],
        out_specs=pl.BlockSpec((tm, d), lambda i: (i, 0)),
        out_shape=jax.ShapeDtypeStruct((t, d), F32),
        scratch_shapes=[
            pltpu.VMEM((2, TOP_K * tm, d), F32),
            pltpu.SemaphoreType.DMA((2,)),
        ],
        compiler_params=_params("arbitrary"),
        name="combine",
    )(pos, pos, h, gates, ys)


def routing_tables(top_idx, t):
    n_assign = t * TOP_K
    tm_c = min(COMBINE_ROWS, t)
    e_flat = top_idx.reshape(-1).astype(jnp.int32)
    position = jnp.arange(n_assign, dtype=jnp.int32)
    se, order = lax.sort((e_flat, position), num_keys=1, is_stable=True)
    experts_iota = jnp.arange(N_EXPERTS, dtype=jnp.int32)
    counts = jnp.sum((e_flat[None, :] == experts_iota[:, None]).astype(jnp.int32), axis=1)
    starts = jnp.cumsum(counts) - counts
    padded = (counts + MOE_ROWS - 1) // MOE_ROWS * MOE_ROWS
    pends = jnp.cumsum(padded)
    pstarts = pends - padded
    n_blocks = -(-n_assign // MOE_ROWS) + N_EXPERTS
    block_start = jnp.arange(n_blocks, dtype=jnp.int32) * MOE_ROWS
    block_expert = jnp.minimum(
        jnp.sum((pends[None, :] <= block_start[:, None]).astype(jnp.int32), axis=1), N_EXPERTS - 1)
    n_valid = (pends[-1:] // MOE_ROWS).astype(jnp.int32)

    row = jnp.arange(MOE_ROWS, dtype=jnp.int32)[None, :]
    e_b = block_expert[:, None]
    local = jnp.arange(n_blocks, dtype=jnp.int32)[:, None] * MOE_ROWS + row - pstarts[e_b]
    used = (local >= 0) & (local < counts[e_b])
    asg = order[jnp.clip(starts[e_b] + local, 0, n_assign - 1)]
    slot_tok = jnp.where(used, asg // TOP_K, 0).reshape(n_blocks, 1, MOE_ROWS)
    slot_sorted = pstarts[se] + position - starts[se]
    _, pos = lax.sort((order, slot_sorted), num_keys=1)
    return block_expert, n_valid, slot_tok, pos.reshape(t // tm_c, 1, tm_c * TOP_K)


def _pad_lanes(v, width, value=0.0):
    return jnp.pad(v, ((0, 0), (0, width - v.shape[1])), constant_values=value)


def hybrid_layer(x, attn_norm_w, w_in, q_norm_w, k_norm_w, attn_sinks, conv_w, conv_b, dt_bias, a_log, d_skip,
                 ssm_norm_w, attn_out_norm_w, w_out, ffn_norm_w, w_router, b_router, w_gate, b_gate, w_up, b_up,
                 w_down, b_down):
    batch, seq, d = x.shape
    t = batch * seq
    x2d = x.reshape(t, d)

    q_w, k_w, v_w, z_w, xbc_w, dt_w = jnp.split(
        w_in, [ATTN_WIDTH, ATTN_WIDTH + KV_WIDTH, ATTN_WIDTH + 2 * KV_WIDTH,
               ATTN_WIDTH + 2 * KV_WIDTH + SSM_WIDTH, ATTN_WIDTH + 2 * KV_WIDTH + SSM_WIDTH + CONV_WIDTH], axis=1)
    w_p = jnp.concatenate([q_w, z_w, xbc_w, k_w, v_w, dt_w], axis=1)
    w_p = _pad_lanes(w_p, PROJ_WIDTH).astype(BF16)
    proj = in_proj(x2d, attn_norm_w.reshape(1, d), w_p)

    qw2 = (jnp.tile(q_norm_w, 2) * (HEAD_DIM ** -0.5)).reshape(1, LANES)
    kw2 = jnp.tile(k_norm_w, 2).reshape(1, LANES)
    attn, (wg16,) = attention(
        proj, attn_sinks, qw2, kw2, attn_out_norm_w.reshape(1, ATTN_WIDTH), batch, seq,
        side=(w_gate.reshape(N_EXPERTS * D_MODEL, D_FF),))

    ssm, (wu16, wd16) = ssd(
        proj,
        conv_w[:, :SSM_WIDTH], conv_w[:, SSM_WIDTH:],
        conv_b[:SSM_WIDTH].reshape(1, -1), conv_b[SSM_WIDTH:].reshape(1, -1),
        _pad_lanes(dt_bias.reshape(1, -1), LANES), _pad_lanes(a_log.reshape(1, -1), LANES),
        jnp.repeat(d_skip, HEAD_DIM).reshape(1, SSM_WIDTH), ssm_norm_w.reshape(1, SSM_WIDTH),
        batch, seq, side=(w_up.reshape(N_EXPERTS * D_MODEL, D_FF), w_down.reshape(N_EXPERTS * D_FF, D_MODEL)))

    h, hf_packed, idx_pad, gate_pad = out_proj_router(
        x2d, attn, ssm, w_out.astype(BF16), ffn_norm_w.reshape(1, d),
        _pad_lanes(w_router, LANES), _pad_lanes(b_router.reshape(1, -1), LANES, NEG_BIG))

    block_expert, n_valid, slot_tok, pos = routing_tables(idx_pad[:, :TOP_K], t)
    ys = experts(
        block_expert, n_valid, slot_tok, hf_packed,
        wg16.reshape(N_EXPERTS, D_MODEL, D_FF), b_gate.reshape(N_EXPERTS, 1, D_FF),
        wu16.reshape(N_EXPERTS, D_MODEL, D_FF), b_up.reshape(N_EXPERTS, 1, D_FF),
        wd16.reshape(N_EXPERTS, D_FF, D_MODEL), b_down.reshape(N_EXPERTS, 1, D_MODEL))

    out = combine(pos, h, gate_pad, ys)
    return out.reshape(batch, seq, d)


def kernel(x, attn_norm_w, w_in, q_norm_w, k_norm_w, attn_sinks, conv_w, conv_b, dt_bias, a_log, d_skip, ssm_norm_w,
           attn_out_norm_w, w_out, ffn_norm_w, w_router, b_router, w_gate, b_gate, w_up, b_up, w_down, b_down):
    for i in range(attn_norm_w.shape[0]):
        x = hybrid_layer(x, attn_norm_w[i], w_in[i], q_norm_w[i], k_norm_w[i], attn_sinks[i], conv_w[i], conv_b[i],
                         dt_bias[i], a_log[i], d_skip[i], ssm_norm_w[i], attn_out_norm_w[i], w_out[i], ffn_norm_w[i],
                         w_router[i], b_router[i], w_gate[i], b_gate[i], w_up[i], b_up[i], w_down[i], b_down[i])
    return x
```
